```python
import jax, jax.numpy as jnp
from jax import lax
import numpy as np

D_MODEL = 1024
BATCH = 8
SEQ = 4096
DEPTH = 4

N_MIXERS = 3
HEAD_DIM = 64
N_HEADS = 12
MIX_WIDTH = N_HEADS * HEAD_DIM
N_MEM = 256
N_MEM_HEADS = 4
MEM_WIDTH = N_MEM_HEADS * HEAD_DIM
ROPE_DIM = HEAD_DIM // 4
ROPE_THETA = 500000.0
POS_OFFSET_RANGE = 8192
IDX_HEADS = 8
IDX_DIM = 64
TOPK_MAX = 256
QBLOCK_A = 64
DIL_PAIRS = ((128, 1), (512, 4), (2048, 16))
B_GROUP_HEADS = 4
B_V_DIM = MIX_WIDTH // B_GROUP_HEADS
MOBA_BLOCK = 256
MOBA_TOPK = 3
QBLOCK_C = 16
D_FF = -(-8 * D_MODEL // (3 * 256)) * 256
ALPHA = (2 * DEPTH) ** 0.25
BETA = (8 * DEPTH) ** -0.25
LN_EPS = 1e-5

WIDTH_A = 3 * MIX_WIDTH + IDX_HEADS * IDX_DIM + IDX_HEADS + IDX_DIM + MEM_WIDTH
WIDTH_BC = 3 * MIX_WIDTH + MEM_WIDTH

kernel_name = 'hybrid_dsa_dilated_moba_deepnorm'


def layer_norm(x, g, b):
    xf = x.astype(jnp.float32)
    mu = jnp.mean(xf, -1, keepdims=True)
    var = jnp.mean(jnp.square(xf - mu), -1, keepdims=True)
    return ((xf - mu) * lax.rsqrt(var + LN_EPS) * g + b).astype(x.dtype)


def split_cols(h, widths):
    cuts = [int(c) for c in np.cumsum(widths)[:-1]]
    return jnp.split(h, cuts, axis=-1)


def rope_tables(positions):
    half = ROPE_DIM // 2
    inv = ROPE_THETA ** (-jnp.arange(half, dtype=jnp.float32) / half)
    ang = positions.astype(jnp.float32)[..., None] * inv
    return jnp.cos(ang)[:, :, None, :], jnp.sin(ang)[:, :, None, :]


def apply_rope(x, cos, sin):
    half = ROPE_DIM // 2
    c, s = cos.astype(x.dtype), sin.astype(x.dtype)
    x1, x2, rest = x[..., :half], x[..., half:ROPE_DIM], x[..., ROPE_DIM:]
    return jnp.concatenate([x1 * c - x2 * s, x2 * c + x1 * s, rest], axis=-1)


def dsa_attention(q, k, v, q_idx, w_idx, k_idx):
    Bsz, S, H, dh = q.shape
    topk = min(TOPK_MAX, S // 4)
    key_pos = jnp.arange(S)
    scale = dh ** -0.5

    def block(i):
        t0 = i * QBLOCK_A
        qpos = t0 + jnp.arange(QBLOCK_A)
        qi = lax.dynamic_slice_in_dim(q_idx, t0, QBLOCK_A, axis=1)
        wi = lax.dynamic_slice_in_dim(w_idx, t0, QBLOCK_A, axis=1)
        dots = jnp.einsum('bqhd,bsd->bqhs', qi, k_idx).astype(jnp.float32)
        score = jnp.einsum('bqhs,bqh->bqs', jax.nn.relu(dots), wi.astype(jnp.float32))
        causal = key_pos[None, :] <= qpos[:, None]
        score = jnp.where(causal[None], score, -jnp.inf)
        _, sel = lax.top_k(score, topk)
        valid = sel <= qpos[None, :, None]
        ks = jax.vmap(lambda kb, ib: kb[ib])(k, sel)
        vs = jax.vmap(lambda vb, ib: vb[ib])(v, sel)
        qb = lax.dynamic_slice_in_dim(q, t0, QBLOCK_A, axis=1)
        logits = jnp.einsum('bqhd,bqkhd->bhqk', qb, ks).astype(jnp.float32) * scale
        logits = jnp.where(valid[:, None], logits, -jnp.inf)
        p = jax.nn.softmax(logits, axis=-1).astype(v.dtype)
        return jnp.einsum('bhqk,bqkhd->bqhd', p, vs)

    out = lax.map(block, jnp.arange(S // QBLOCK_A))
    return out.transpose(1, 0, 2, 3, 4).reshape(Bsz, S, H, dh)


def dilated_group(q, k, v, window, dilation):
    Bsz, S, G, dh = q.shape
    dv = v.shape[-1]
    n = window // dilation
    L = S // dilation
    nb = -(-L // n)
    Lp = nb * n

    def to_sub(a):
        a = a.reshape(Bsz, L, dilation, G, a.shape[-1]).transpose(0, 2, 1, 3, 4)
        a = jnp.pad(a, ((0, 0), (0, 0), (0, Lp - L), (0, 0), (0, 0)))
        return a.reshape(Bsz, dilation, nb, n, G, a.shape[-1])

    qs, ks, vs = to_sub(q), to_sub(k), to_sub(v)
    shift = ((0, 0), (0, 0), (1, 0), (0, 0), (0, 0), (0, 0))
    kk = jnp.concatenate([jnp.pad(ks[:, :, :-1], shift), ks], axis=3)
    vv = jnp.concatenate([jnp.pad(vs[:, :, :-1], shift), vs], axis=3)
    qi = jnp.arange(n)[:, None] + n
    ki = jnp.arange(2 * n)[None, :]
    dist = qi - ki
    ksub = jnp.arange(nb)[:, None, None] * n + ki[None] - n
    mask = ((dist >= 0) & (dist <= n))[None] & (ksub >= 0)
    logits = jnp.einsum('brnqgd,brnkgd->brngqk', qs, kk).astype(jnp.float32) * (dh ** -0.5)
    logits = jnp.where(mask[None, None, :, None], logits, -jnp.inf)
    lse = jax.nn.logsumexp(logits, axis=-1)
    p = jnp.exp(logits - lse[..., None]).astype(v.dtype)
    out = jnp.einsum('brngqk,brnkgd->brnqgd', p, vv)
    out = out.reshape(Bsz, dilation, Lp, G, dv)[:, :, :L]
    out = out.transpose(0, 2, 1, 3, 4).reshape(Bsz, S, G, dv)
    lse = lse.transpose(0, 1, 2, 4, 3).reshape(Bsz, dilation, Lp, G)[:, :, :L]
    lse = lse.transpose(0, 2, 1, 3).reshape(Bsz, S, G)
    return out, lse


def dilated_attention(q, k, v):
    outs, lses = [], []
    for g, (window, dilation) in enumerate(DIL_PAIRS):
        hs = slice(g * B_GROUP_HEADS, (g + 1) * B_GROUP_HEADS)
        o, l = dilated_group(q[:, :, hs], k[:, :, hs], v, window, dilation)
        outs.append(o)
        lses.append(l)
    wts = jax.nn.softmax(jnp.stack(lses, 0), axis=0).astype(v.dtype)
    return jnp.einsum('gbsh,gbshd->bshd', wts, jnp.stack(outs, 0))


def moba_attention(q, k, v):
    Bsz, S, H, dh = q.shape
    nblk = -(-S // MOBA_BLOCK)
    Sp = nblk * MOBA_BLOCK
    pad = lambda a: jnp.pad(a, ((0, 0), (0, Sp - S), (0, 0), (0, 0)))
    qp = pad(q)
    kbt = pad(k).reshape(Bsz, nblk, MOBA_BLOCK, H, dh).transpose(0, 3, 1, 2, 4)
    vbt = pad(v).reshape(Bsz, nblk, MOBA_BLOCK, H, dh).transpose(0, 3, 1, 2, 4)
    kmean = jnp.mean(kbt.astype(jnp.float32), axis=3)
    ksel = min(MOBA_TOPK, nblk)
    scale = dh ** -0.5
    bidx = jnp.arange(Bsz)[:, None, None, None]
    hidx = jnp.arange(H)[None, :, None, None]

    def block(i):
        t0 = i * QBLOCK_C
        own = t0 // MOBA_BLOCK
        qpos = t0 + jnp.arange(QBLOCK_C)
        qh = lax.dynamic_slice_in_dim(qp, t0, QBLOCK_C, axis=1).transpose(0, 2, 1, 3)
        gate = jnp.einsum('bhqd,bhnd->bhqn', qh.astype(jnp.float32), kmean)
        gate = jnp.where(jnp.arange(nblk) < own, gate, -jnp.inf)
        _, sel = lax.top_k(gate, ksel)
        valid = jnp.repeat(jnp.arange(ksel) < own, MOBA_BLOCK)
        kg = kbt[bidx, hidx, sel]
        vg = vbt[bidx, hidx, sel]
        ko = lax.dynamic_index_in_dim(kbt, own, axis=2, keepdims=False)
        vo = lax.dynamic_index_in_dim(vbt, own, axis=2, keepdims=False)
        kpos = own * MOBA_BLOCK + jnp.arange(MOBA_BLOCK)
        s_sel = jnp.einsum('bhqd,bhqnkd->bhqnk', qh, kg).astype(jnp.float32)
        s_sel = jnp.where(valid, s_sel.reshape(Bsz, H, QBLOCK_C, ksel * MOBA_BLOCK), -jnp.inf)
        s_own = jnp.einsum('bhqd,bhkd->bhqk', qh, ko).astype(jnp.float32)
        s_own = jnp.where(kpos[None, :] <= qpos[:, None], s_own, -jnp.inf)
        p = jax.nn.softmax(jnp.concatenate([s_sel, s_own], -1) * scale, axis=-1).astype(v.dtype)
        p_sel = p[..., :ksel * MOBA_BLOCK].reshape(Bsz, H, QBLOCK_C, ksel, MOBA_BLOCK)
        o = jnp.einsum('bhqnk,bhqnkd->bhqd', p_sel, vg) + jnp.einsum('bhqk,bhkd->bhqd', p[..., ksel * MOBA_BLOCK:], vo)
        return o.transpose(0, 2, 1, 3)

    out = lax.map(block, jnp.arange(Sp // QBLOCK_C))
    return out.transpose(1, 0, 2, 3, 4).reshape(Bsz, Sp, H, dh)[:, :S]


def memory_attention(q, mk, mv):
    logits = jnp.einsum('bshd,bnhd->bhsn', q, mk).astype(jnp.float32) * (HEAD_DIM ** -0.5)
    p = jax.nn.softmax(logits, axis=-1).astype(mv.dtype)
    return jnp.einsum('bhsn,bnhd->bshd', p, mv)


def swiglu(x, w_gate_up, w_down):
    g, u = jnp.split(x @ w_gate_up, 2, axis=-1)
    return (jax.nn.silu(g) * u) @ w_down


def setup_inputs(seed: int = 0) -> dict:
    key = jax.random.key(seed)
    ks = jax.random.split(key, 18)
    n_a, n_b, n_c = (DEPTH + 2) // 3, (DEPTH + 1) // 3, DEPTH // 3
    nrm = lambda k, shape, s: jax.random.normal(k, shape, jnp.float32) * s
    gain = lambda k, shape: 1.0 + nrm(k, shape, 0.02)
    offs = jax.random.randint(ks[2], (BATCH, 1), 0, POS_OFFSET_RANGE, dtype=jnp.int32)
    positions = offs + jnp.arange(SEQ, dtype=jnp.int32)[None, :]
    return {
        'x': nrm(ks[0], (BATCH, SEQ, D_MODEL), 1.0),
        'mem': nrm(ks[1], (BATCH, N_MEM, D_MODEL), 1.0),
        'positions': positions,
        'mem_ln_g': gain(ks[3], (D_MODEL,)),
        'mem_ln_b': nrm(ks[4], (D_MODEL,), 0.02),
        'w_in_a': nrm(ks[5], (n_a, D_MODEL, WIDTH_A), D_MODEL ** -0.5),
        'idx_kn_g': gain(ks[6], (n_a, IDX_DIM)),
        'idx_kn_b': nrm(ks[7], (n_a, IDX_DIM), 0.02),
        'w_in_b': nrm(ks[8], (n_b, D_MODEL, WIDTH_BC), D_MODEL ** -0.5),
        'w_in_c': nrm(ks[9], (n_c, D_MODEL, WIDTH_BC), D_MODEL ** -0.5),
        'w_mem_kv': nrm(ks[10], (DEPTH, D_MODEL, 2 * MEM_WIDTH), D_MODEL ** -0.5),
        'w_out': nrm(ks[11], (DEPTH, MIX_WIDTH + MEM_WIDTH, D_MODEL), BETA * (MIX_WIDTH + MEM_WIDTH) ** -0.5),
        'ln1_g': gain(ks[12], (DEPTH, D_MODEL)),
        'ln1_b': nrm(ks[13], (DEPTH, D_MODEL), 0.02),
        'w_gate_up': nrm(ks[14], (DEPTH, D_MODEL, 2 * D_FF), D_MODEL ** -0.5),
        'w_down': nrm(ks[15], (DEPTH, D_FF, D_MODEL), BETA * D_FF ** -0.5),
        'ln2_g': gain(ks[16], (DEPTH, D_MODEL)),
        'ln2_b': nrm(ks[17], (DEPTH, D_MODEL), 0.02),
    }


def reference(x, mem, positions, mem_ln_g, mem_ln_b, w_in_a, idx_kn_g, idx_kn_b, w_in_b, w_in_c,
              w_mem_kv, w_out, ln1_g, ln1_b, w_gate_up, w_down, ln2_g, ln2_b):
    Bsz, S, _ = x.shape
    cos, sin = rope_tables(positions)
    mem_n = layer_norm(mem, mem_ln_g, mem_ln_b)
    heads = lambda a, h, d: a.reshape(Bsz, S, h, d)
    for i in range(DEPTH):
        kind, j = i % N_MIXERS, i // N_MIXERS
        if kind == 0:
            q, k, v, qi, wi, ki, qm = split_cols(
                x @ w_in_a[j],
                (MIX_WIDTH, MIX_WIDTH, MIX_WIDTH, IDX_HEADS * IDX_DIM, IDX_HEADS, IDX_DIM, MEM_WIDTH))
            q = apply_rope(heads(q, N_HEADS, HEAD_DIM), cos, sin)
            k = apply_rope(heads(k, N_HEADS, HEAD_DIM), cos, sin)
            qi = apply_rope(heads(qi, IDX_HEADS, IDX_DIM), cos, sin)
            ki = apply_rope(layer_norm(ki, idx_kn_g[j], idx_kn_b[j])[:, :, None, :], cos, sin)[:, :, 0]
            wi = wi * (IDX_HEADS * IDX_DIM) ** -0.5
            mix = dsa_attention(q, k, heads(v, N_HEADS, HEAD_DIM), qi, wi, ki)
        elif kind == 1:
            q, k, v, qm = split_cols(x @ w_in_b[j], (MIX_WIDTH, MIX_WIDTH, MIX_WIDTH, MEM_WIDTH))
            q = apply_rope(heads(q, N_HEADS, HEAD_DIM), cos, sin)
            k = apply_rope(heads(k, N_HEADS, HEAD_DIM), cos, sin)
            mix = dilated_attention(q, k, heads(v, B_GROUP_HEADS, B_V_DIM))
        else:
            q, k, v, qm = split_cols(x @ w_in_c[j], (MIX_WIDTH, MIX_WIDTH, MIX_WIDTH, MEM_WIDTH))
            q = apply_rope(heads(q, N_HEADS, HEAD_DIM), cos, sin)
            k = apply_rope(heads(k, N_HEADS, HEAD_DIM), cos, sin)
            mix = moba_attention(q, k, heads(v, N_HEADS, HEAD_DIM))
        mk, mv = jnp.split(mem_n @ w_mem_kv[i], 2, axis=-1)
        mo = memory_attention(heads(qm, N_MEM_HEADS, HEAD_DIM),
                              mk.reshape(Bsz, N_MEM, N_MEM_HEADS, HEAD_DIM),
                              mv.reshape(Bsz, N_MEM, N_MEM_HEADS, HEAD_DIM))
        mixed = jnp.concatenate([mix.reshape(Bsz, S, MIX_WIDTH), mo.reshape(Bsz, S, MEM_WIDTH)], -1) @ w_out[i]
        x = layer_norm(ALPHA * x + mixed, ln1_g[i], ln1_b[i])
        x = layer_norm(ALPHA * x + swiglu(x, w_gate_up[i], w_down[i]), ln2_g[i], ln2_b[i])
    return x
```

```python
import functools

import jax
import jax.numpy as jnp
import numpy as np
from jax import lax
from jax.experimental import pallas as pl
from jax.experimental.pallas import tpu as pltpu

F32 = jnp.float32
BF16 = jnp.bfloat16
I32 = jnp.int32

HEAD_DIM = 64
N_HEADS = 12
MIX_WIDTH = N_HEADS * HEAD_DIM
N_MEM_HEADS = 4
MEM_WIDTH = N_MEM_HEADS * HEAD_DIM
ROPE_DIM = HEAD_DIM // 4
ROPE_HALF = ROPE_DIM // 2
ROPE_THETA = 500000.0
IDX_HEADS = 8
IDX_DIM = 64
TOPK_MAX = 256
DIL_PAIRS = ((128, 1), (512, 4), (2048, 16))
B_GROUP_HEADS = 4
B_V_DIM = MIX_WIDTH // B_GROUP_HEADS
MOBA_BLOCK = 256
MOBA_TOPK = 3
LN_EPS = 1e-5
SCALE = HEAD_DIM ** -0.5

LANES = 128
V_HEAD_PAD = 256
VMEM_LIMIT_BYTES = 56 * 1024 * 1024

TILE = 256
NEG = -1e30
INT_MIN = np.int32(-2 ** 31)


def _cparams(n_axes):
    return pltpu.CompilerParams(dimension_semantics=("arbitrary",) * n_axes,
                                vmem_limit_bytes=VMEM_LIMIT_BYTES)


def _layer_norm(y, g, b):
    mu = jnp.mean(y, axis=-1, keepdims=True)
    yc = y - mu
    var = jnp.mean(yc * yc, axis=-1, keepdims=True)
    return yc * lax.rsqrt(var + LN_EPS) * g + b


def _dot_nt(a, b):
    return lax.dot_general(a, b, (((1,), (1,)), ((), ())), preferred_element_type=F32)


def _half_masks(rows):
    lane = lax.broadcasted_iota(I32, (rows, LANES), 1)
    return lane < HEAD_DIM, lane >= HEAD_DIM


def _flash_step(s, m, l, acc, v):
    m_new = jnp.maximum(m, jnp.max(s, axis=1, keepdims=True))
    p = jnp.exp(s - m_new)
    alpha = jnp.exp(m - m_new)
    l = alpha * l + jnp.sum(p, axis=1, keepdims=True)
    acc = alpha * acc + jnp.dot(p.astype(BF16), v, preferred_element_type=F32)
    return m_new, l, acc


def _flash_init(rows, width):
    return (jnp.full((rows, 1), NEG, F32), jnp.zeros((rows, 1), F32), jnp.zeros((rows, width), F32))


def _memkv_kernel(mem_ref, g_ref, b_ref, w_ref, o_ref):
    mn = _layer_norm(mem_ref[0], g_ref[...], b_ref[...])
    o_ref[0, 0] = jnp.dot(mn.astype(BF16), w_ref[0], preferred_element_type=F32).astype(BF16)


def _memkv(mem, g, b, w_bf16):
    depth = w_bf16.shape[0]
    bsz, n_mem, d = mem.shape
    wout = w_bf16.shape[2]
    return pl.pallas_call(
        _memkv_kernel,
        grid=(depth, bsz),
        in_specs=[pl.BlockSpec((1, n_mem, d), lambda i, b_: (b_, 0, 0)),
                  pl.BlockSpec((1, d), lambda i, b_: (0, 0)),
                  pl.BlockSpec((1, d), lambda i, b_: (0, 0)),
                  pl.BlockSpec((1, d, wout), lambda i, b_: (i, 0, 0))],
        out_specs=pl.BlockSpec((1, 1, n_mem, wout), lambda i, b_: (i, b_, 0, 0)),
        out_shape=jax.ShapeDtypeStruct((depth, bsz, n_mem, wout), BF16),
        compiler_params=_cparams(2),
    )(mem, g.reshape(1, d), b.reshape(1, d), w_bf16)


def _rope(h, c, s1, s2):
    return h * c + pltpu.roll(h, ROPE_HALF, 1) * s1 + pltpu.roll(h, LANES - ROPE_HALF, 1) * s2


def _inproj_kernel(sections, x_ref, w_ref, c_ref, s1_ref, s2_ref, lng_ref, lnb_ref, *out_refs):
    x = x_ref[...]
    c, s1, s2 = c_ref[...], s1_ref[...], s2_ref[...]
    for (start, width, kind, scale), o_ref in zip(sections, out_refs):
        h = jnp.dot(x, w_ref[:, start:start + width], preferred_element_type=F32)
        if kind == "ln_rope":
            h = _layer_norm(h, lng_ref[...], lnb_ref[...])
        for ch in range(width // LANES):
            hc = h[:, ch * LANES:(ch + 1) * LANES]
            if kind in ("rope", "ln_rope"):
                hc = _rope(hc, c, s1, s2)
            if scale != 1.0:
                hc = hc * scale
            o_ref[:, ch * LANES:(ch + 1) * LANES] = hc.astype(o_ref.dtype)


def _inproj(xb, w_bf16, tables, lng, lnb, sections, out_dtypes, tm=512):
    t, d = xb.shape
    wtot = w_bf16.shape[1]
    row = lambda i: (i, 0)
    fixed = lambda i: (0, 0)
    out_shape = [jax.ShapeDtypeStruct((t, sec[1]), dt) for sec, dt in zip(sections, out_dtypes)]
    out_specs = [pl.BlockSpec((tm, sec[1]), row) for sec in sections]
    return pl.pallas_call(
        functools.partial(_inproj_kernel, tuple(sections)),
        grid=(t // tm,),
        in_specs=[pl.BlockSpec((tm, d), row),
                  pl.BlockSpec((d, wtot), fixed),
                  pl.BlockSpec((tm, LANES), row),
                  pl.BlockSpec((tm, LANES), row),
                  pl.BlockSpec((tm, LANES), row),
                  pl.BlockSpec((1, LANES), fixed),
                  pl.BlockSpec((1, LANES), fixed)],
        out_specs=out_specs,
        out_shape=out_shape,
        compiler_params=_cparams(1),
    )(xb, w_bf16, *tables, lng, lnb)


def _dsa_kernel(topk, q_ref, k_ref, v_ref, qi_ref, wi_ref, ki_ref, o_ref, keys_ref, wb_ref, bias_ref):
    i = pl.program_id(1)
    nkt = i + 1
    row = lax.broadcasted_iota(I32, (TILE, TILE), 0)
    col = lax.broadcasted_iota(I32, (TILE, TILE), 1)
    lo, hi = _half_masks(TILE)

    w = wi_ref[0]
    for h in range(IDX_HEADS):
        wb_ref[h] = jnp.broadcast_to(w[:, h:h + 1], (TILE, TILE))
    qi_heads = []
    for p in range(IDX_HEADS // 2):
        qp = qi_ref[0, :, p * LANES:(p + 1) * LANES]
        qi_heads.append(jnp.where(lo, qp, jnp.zeros_like(qp)))
        qi_heads.append(jnp.where(hi, qp, jnp.zeros_like(qp)))

    def score_tile(j, carry):
        kt = ki_ref[0, pl.ds(pl.multiple_of(j * TILE, TILE), TILE), :]
        sc = jnp.zeros((TILE, TILE), F32)
        for h in range(IDX_HEADS):
            sc = sc + wb_ref[h] * jnp.maximum(_dot_nt(qi_heads[h], kt), 0.0)
        bits = lax.bitcast_convert_type(sc, I32)
        key = bits ^ (lax.shift_right_arithmetic(bits, 31) & np.int32(0x7FFFFFFF))
        causal = (j * TILE + col) <= (i * TILE + row)
        keys_ref[j] = jnp.where(causal, key, INT_MIN)
        return carry

    lax.fori_loop(0, nkt, score_tile, 0)

    def count(pred):
        def body(j, acc):
            return acc + jnp.where(pred(keys_ref[j]), 1.0, 0.0)
        acc = lax.fori_loop(0, nkt, body, jnp.zeros((TILE, TILE), F32))
        return jnp.sum(acc, axis=1, keepdims=True)

    t_row = i * TILE + lax.broadcasted_iota(I32, (TILE, 1), 0)
    k_row = jnp.minimum(topk, t_row + 1).astype(F32)
    zero = jnp.zeros((TILE, 1), I32)
    thr0 = jnp.where(count(lambda kt: kt >= zero) >= k_row, zero, zero + INT_MIN)

    def bisect(it, thr):
        cand = thr | lax.shift_left(np.int32(1), 30 - it)
        return jnp.where(count(lambda kt: kt >= cand) >= k_row, cand, thr)

    thr = lax.fori_loop(0, 31, bisect, thr0)

    need = k_row - count(lambda kt: kt > thr)
    n_eq = count(lambda kt: kt == thr)
    has_tie = jnp.max(jnp.where(n_eq > need, 1.0, 0.0))

    @pl.when(has_tie > 0.0)
    def _():
        tri = jnp.where(row <= col, 1.0, 0.0).astype(BF16)

        def fix(j, seen):
            kt = keys_ref[j]
            eq = kt == thr
            eqf = jnp.where(eq, 1.0, 0.0)
            pref = jnp.dot(eqf.astype(BF16), tri, preferred_element_type=F32) + seen
            keys_ref[j] = jnp.where(eq & (pref > need), INT_MIN, kt)
            return seen + jnp.sum(eqf, axis=1, keepdims=True)

        lax.fori_loop(0, nkt, fix, jnp.zeros((TILE, 1), F32))

    def to_bias(j, carry):
        bias_ref[j] = jnp.where(keys_ref[j] >= thr, 0.0, NEG)
        return carry

    lax.fori_loop(0, nkt, to_bias, 0)

    for p in range(N_HEADS // 2):
        cs = slice(p * LANES, (p + 1) * LANES)
        qp = q_ref[0, :, cs]
        outs = []
        for half in (lo, hi):
            qh = jnp.where(half, qp, jnp.zeros_like(qp))

            def body(j, carry, qh=qh, cs=cs):
                ks = pl.ds(pl.multiple_of(j * TILE, TILE), TILE)
                s = _dot_nt(qh, k_ref[0, ks, cs]) + bias_ref[j]
                return _flash_step(s, *carry, v_ref[0, ks, cs])

            _, l, acc = lax.fori_loop(0, nkt, body, _flash_init(TILE, LANES))
            outs.append(acc / l)
        o_ref[0, :, cs] = jnp.where(lo, outs[0], outs[1]).astype(o_ref.dtype)


def _dsa(q, k, v, qi, wi, ki):
    bsz, s, _ = q.shape
    topk = min(TOPK_MAX, s // 4)
    nt = s // TILE
    qtile = lambda w: pl.BlockSpec((1, TILE, w), lambda b, i: (b, i, 0))
    full = lambda w: pl.BlockSpec((1, s, w), lambda b, i: (b, 0, 0))
    return pl.pallas_call(
        functools.partial(_dsa_kernel, topk),
        grid=(bsz, nt),
        in_specs=[qtile(MIX_WIDTH), full(MIX_WIDTH), full(MIX_WIDTH),
                  qtile(IDX_HEADS * IDX_DIM), qtile(LANES), full(LANES)],
        out_specs=qtile(MIX_WIDTH),
        out_shape=jax.ShapeDtypeStruct((bsz, s, MIX_WIDTH), BF16),
        scratch_shapes=[pltpu.VMEM((nt, TILE, TILE), I32),
                        pltpu.VMEM((IDX_HEADS, TILE, TILE), F32),
                        pltpu.VMEM((nt, TILE, TILE), F32)],
        compiler_params=_cparams(2),
    )(q, k, v, qi, wi, ki)


def _dilated_kernel(q_ref, k_ref, v_ref, o_ref):
    i = pl.program_id(1)
    lo, hi = _half_masks(TILE)
    rel = (lax.broadcasted_iota(I32, (TILE, TILE), 0)
           - lax.broadcasted_iota(I32, (TILE, TILE), 1))
    for hv in range(B_GROUP_HEADS):
        vs = slice(hv * V_HEAD_PAD, (hv + 1) * V_HEAD_PAD)
        carry = _flash_init(TILE, V_HEAD_PAD)
        for g, (window, dilation) in enumerate(DIL_PAIRS):
            head = g * B_GROUP_HEADS + hv
            cs = slice((head // 2) * LANES, (head // 2 + 1) * LANES)
            qp = q_ref[0, :, cs]
            qh = jnp.where(hi if head % 2 else lo, qp, jnp.zeros_like(qp))
            nback = -(-window // TILE)

            def body(j, carry, qh=qh, cs=cs, vs=vs, window=window, dilation=dilation):
                ks = pl.ds(pl.multiple_of(j * TILE, TILE), TILE)
                dist = rel + (i - j) * TILE
                ok = (dist >= 0) & (dist <= window) & ((dist & (dilation - 1)) == 0)
                s = jnp.where(ok, _dot_nt(qh, k_ref[0, ks, cs]), NEG)
                return _flash_step(s, *carry, v_ref[0, ks, vs])

            carry = lax.fori_loop(jnp.maximum(i - nback, 0), i + 1, body, carry)
        _, l, acc = carry
        o_ref[0, :, vs] = (acc / l).astype(o_ref.dtype)


def _dilated(q, k, vpad):
    bsz, s, _ = q.shape
    wv = vpad.shape[2]
    return pl.pallas_call(
        _dilated_kernel,
        grid=(bsz, s // TILE),
        in_specs=[pl.BlockSpec((1, TILE, MIX_WIDTH), lambda b, i: (b, i, 0)),
                  pl.BlockSpec((1, s, MIX_WIDTH), lambda b, i: (b, 0, 0)),
                  pl.BlockSpec((1, s, wv), lambda b, i: (b, 0, 0))],
        out_specs=pl.BlockSpec((1, TILE, wv), lambda b, i: (b, i, 0)),
        out_shape=jax.ShapeDtypeStruct((bsz, s, wv), BF16),
        compiler_params=_cparams(2),
    )(q, k, vpad)


def _kmean_kernel(nblk, k_ref, o_ref):
    o_ref[0] = jnp.zeros(o_ref.shape[1:], o_ref.dtype)
    for n in range(nblk):
        kb = k_ref[0, n * MOBA_BLOCK:(n + 1) * MOBA_BLOCK, :].astype(F32)
        o_ref[0, n:n + 1, :] = jnp.mean(kb, axis=0, keepdims=True).astype(o_ref.dtype)


def _kmean(k):
    bsz, s, w = k.shape
    return pl.pallas_call(
        functools.partial(_kmean_kernel, s // MOBA_BLOCK),
        grid=(bsz,),
        in_specs=[pl.BlockSpec((1, s, w), lambda b: (b, 0, 0))],
        out_specs=pl.BlockSpec((1, LANES, w), lambda b: (b, 0, 0)),
        out_shape=jax.ShapeDtypeStruct((bsz, LANES, w), BF16),
        compiler_params=_cparams(1),
    )(k)


def _moba_kernel(q_ref, k_ref, v_ref, km_ref, o_ref):
    i = pl.program_id(1)
    lo, hi = _half_masks(TILE)
    lane = lax.broadcasted_iota(I32, (TILE, LANES), 1)
    lane_f = lane.astype(F32)
    row = lax.broadcasted_iota(I32, (TILE, TILE), 0)
    col = lax.broadcasted_iota(I32, (TILE, TILE), 1)
    causal_bias = jnp.where(col <= row, 0.0, NEG)
    own = pl.ds(pl.multiple_of(i * TILE, TILE), TILE)
    for p in range(N_HEADS // 2):
        cs = slice(p * LANES, (p + 1) * LANES)
        qp = q_ref[0, :, cs]
        outs = []
        for half in (lo, hi):
            qh = jnp.where(half, qp, jnp.zeros_like(qp))
            g = jnp.where(lane < i, _dot_nt(qh, km_ref[0, :, cs]), -jnp.inf)
            sel = jnp.zeros((TILE, LANES), F32)
            for r in range(MOBA_TOPK):
                mx = jnp.max(g, axis=1, keepdims=True)
                idx = jnp.min(jnp.where(g == mx, lane_f, float(LANES)), axis=1, keepdims=True)
                hit = lane_f == idx
                sel = jnp.where(hit & (r < i), 1.0, sel)
                g = jnp.where(hit, -jnp.inf, g)

            def body(n, carry, qh=qh, cs=cs, sel=sel):
                ks = pl.ds(pl.multiple_of(n * TILE, TILE), TILE)
                picked = jnp.sum(jnp.where(lane == n, sel, 0.0), axis=1, keepdims=True)
                s = _dot_nt(qh, k_ref[0, ks, cs]) + jnp.where(picked > 0.5, 0.0, NEG)
                return _flash_step(s, *carry, v_ref[0, ks, cs])

            carry = lax.fori_loop(0, i, body, _flash_init(TILE, LANES))
            s_own = _dot_nt(qh, k_ref[0, own, cs]) + causal_bias
            _, l, acc = _flash_step(s_own, *carry, v_ref[0, own, cs])
            outs.append(acc / l)
        o_ref[0, :, cs] = jnp.where(lo, outs[0], outs[1]).astype(o_ref.dtype)


def _moba(q, k, v, kmean):
    bsz, s, w = q.shape
    return pl.pallas_call(
        _moba_kernel,
        grid=(bsz, s // TILE),
        in_specs=[pl.BlockSpec((1, TILE, w), lambda b, i: (b, i, 0)),
                  pl.BlockSpec((1, s, w), lambda b, i: (b, 0, 0)),
                  pl.BlockSpec((1, s, w), lambda b, i: (b, 0, 0)),
                  pl.BlockSpec((1, LANES, w), lambda b, i: (b, 0, 0))],
        out_specs=pl.BlockSpec((1, TILE, w), lambda b, i: (b, i, 0)),
        out_shape=jax.ShapeDtypeStruct((bsz, s, w), BF16),
        compiler_params=_cparams(2),
    )(q, k, v, kmean)


def _post_kernel(alpha, mix_ref, qm_ref, mkv_ref, wmix_ref, wmem_ref, x_ref, g_ref, b_ref,
                 xo_ref, xb_ref):
    rows = mix_ref.shape[0]
    lo, hi = _half_masks(rows)
    mo = []
    for p in range(MEM_WIDTH // LANES):
        qp = qm_ref[:, p * LANES:(p + 1) * LANES]
        mk = mkv_ref[0, :, p * LANES:(p + 1) * LANES]
        mv = mkv_ref[0, :, MEM_WIDTH + p * LANES:MEM_WIDTH + (p + 1) * LANES]
        outs = []
        for half in (lo, hi):
            s = _dot_nt(jnp.where(half, qp, jnp.zeros_like(qp)), mk)
            e = jnp.exp(s - jnp.max(s, axis=1, keepdims=True))
            pv = jnp.dot(e.astype(BF16), mv, preferred_element_type=F32)
            outs.append(pv / jnp.sum(e, axis=1, keepdims=True))
        mo.append(jnp.where(lo, outs[0], outs[1]).astype(BF16))
    mixed = jnp.dot(mix_ref[...], wmix_ref[...], preferred_element_type=F32)
    for p, mo_p in enumerate(mo):
        mixed = mixed + jnp.dot(mo_p, wmem_ref[p * LANES:(p + 1) * LANES, :],
                                preferred_element_type=F32)
    y = _layer_norm(alpha * x_ref[...] + mixed, g_ref[...], b_ref[...])
    xo_ref[...] = y
    xb_ref[...] = y.astype(BF16)


def _post(alpha, mix, qm, mkv, wmix, wmem, xf, g, b, seq, tm=512):
    t, d = xf.shape
    wm = mix.shape[1]
    per_batch = seq // tm
    row = lambda i: (i, 0)
    fixed = lambda i: (0, 0)
    return pl.pallas_call(
        functools.partial(_post_kernel, alpha),
        grid=(t // tm,),
        in_specs=[pl.BlockSpec((tm, wm), row),
                  pl.BlockSpec((tm, MEM_WIDTH), row),
                  pl.BlockSpec((1,) + mkv.shape[1:], lambda i: (i // per_batch, 0, 0)),
                  pl.BlockSpec(wmix.shape, fixed),
                  pl.BlockSpec(wmem.shape, fixed),
                  pl.BlockSpec((tm, d), row),
                  pl.BlockSpec((1, d), fixed),
                  pl.BlockSpec((1, d), fixed)],
        out_specs=[pl.BlockSpec((tm, d), row), pl.BlockSpec((tm, d), row)],
        out_shape=[jax.ShapeDtypeStruct((t, d), F32), jax.ShapeDtypeStruct((t, d), BF16)],
        compiler_params=_cparams(1),
    )(mix, qm, mkv, wmix, wmem, xf, g.reshape(1, d), b.reshape(1, d))


def _ffn_kernel(alpha, xb_ref, wg_ref, wu_ref, wd_ref, x_ref, g_ref, b_ref, xo_ref, xbo_ref, acc_ref):
    j = pl.program_id(1)

    @pl.when(j == 0)
    def _():
        acc_ref[...] = jnp.zeros_like(acc_ref)

    xb = xb_ref[...]
    gate = jnp.dot(xb, wg_ref[...], preferred_element_type=F32)
    up = jnp.dot(xb, wu_ref[...], preferred_element_type=F32)
    h = (gate * jax.nn.sigmoid(gate) * up).astype(BF16)
    acc_ref[...] += jnp.dot(h, wd_ref[...], preferred_element_type=F32)

    @pl.when(j == pl.num_programs(1) - 1)
    def _():
        y = _layer_norm(alpha * x_ref[...] + acc_ref[...], g_ref[...], b_ref[...])
        xo_ref[...] = y
        xbo_ref[...] = y.astype(BF16)


def _ffn(alpha, xb, xf, wgu, wd, g, b, tm=1024, tf=256):
    t, d = xf.shape
    dff = wd.shape[0]
    nff = dff // tf
    row = lambda i, j: (i, 0)
    fixed = lambda i, j: (0, 0)
    return pl.pallas_call(
        functools.partial(_ffn_kernel, alpha),
        grid=(t // tm, nff),
        in_specs=[pl.BlockSpec((tm, d), row),
                  pl.BlockSpec((d, tf), lambda i, j: (0, j)),
                  pl.BlockSpec((d, tf), lambda i, j: (0, nff + j)),
                  pl.BlockSpec((tf, d), lambda i, j: (j, 0)),
                  pl.BlockSpec((tm, d), row),
                  pl.BlockSpec((1, d), fixed),
                  pl.BlockSpec((1, d), fixed)],
        out_specs=[pl.BlockSpec((tm, d), row), pl.BlockSpec((tm, d), row)],
        out_shape=[jax.ShapeDtypeStruct((t, d), F32), jax.ShapeDtypeStruct((t, d), BF16)],
        scratch_shapes=[pltpu.VMEM((tm, d), F32)],
        compiler_params=_cparams(2),
    )(xb, wgu, wgu, wd, xf, g.reshape(1, d), b.reshape(1, d))


def _rope_tables(positions):
    inv = ROPE_THETA ** (-jnp.arange(ROPE_HALF, dtype=F32) / ROPE_HALF)
    ang = positions.astype(F32)[..., None] * inv
    cos, sin = jnp.cos(ang), jnp.sin(ang)
    rest = HEAD_DIM - ROPE_DIM
    pad = lambda a, before, after, val: jnp.pad(a, ((0, 0), (0, 0), (before, after)), constant_values=val)
    c = pad(jnp.concatenate([cos, cos], -1), 0, rest, 1.0)
    s1 = pad(sin, ROPE_HALF, rest, 0.0)
    s2 = pad(-sin, 0, ROPE_HALF + rest, 0.0)
    t = positions.shape[0] * positions.shape[1]
    return tuple(jnp.tile(a, (1, 1, LANES // HEAD_DIM)).reshape(t, LANES) for a in (c, s1, s2))


def _pad_cols(w, width):
    return jnp.pad(w, ((0, 0), (0, width - w.shape[1])))


def kernel(x, mem, positions, mem_ln_g, mem_ln_b, w_in_a, idx_kn_g, idx_kn_b, w_in_b, w_in_c,
           w_mem_kv, w_out, ln1_g, ln1_b, w_gate_up, w_down, ln2_g, ln2_b):
    bsz, seq, d = x.shape
    t = bsz * seq
    depth = w_out.shape[0]
    alpha = (2 * depth) ** 0.25
    assert seq % TILE == 0 and TILE == MOBA_BLOCK

    tables = _rope_tables(positions)
    mkv_all = _memkv(mem, mem_ln_g, mem_ln_b, w_mem_kv.astype(BF16))
    xf = x.reshape(t, d)
    xb = xf.astype(BF16)
    dummy_ln = jnp.zeros((1, LANES), F32)
    m3 = 3 * MIX_WIDTH
    b3 = lambda a: a.reshape(bsz, seq, a.shape[-1])

    for i in range(depth):
        kind, j = i % 3, i // 3
        wo = w_out[i]
        if kind == 0:
            w = w_in_a[j]
            c_qi, c_wi = m3, m3 + IDX_HEADS * IDX_DIM
            c_ki, c_qm = c_wi + IDX_HEADS, c_wi + IDX_HEADS + IDX_DIM
            w_ki = w[:, c_ki:c_qm]
            w_new = jnp.concatenate(
                [w[:, :c_wi], w[:, c_qm:], _pad_cols(w[:, c_wi:c_ki], LANES), w_ki, w_ki], axis=1).astype(BF16)
            o_qm = c_wi
            sections = [(0, MIX_WIDTH, "rope", SCALE), (MIX_WIDTH, MIX_WIDTH, "rope", 1.0),
                        (2 * MIX_WIDTH, MIX_WIDTH, "plain", 1.0),
                        (c_qi, IDX_HEADS * IDX_DIM, "rope", 1.0),
                        (o_qm, MEM_WIDTH, "plain", SCALE),
                        (o_qm + MEM_WIDTH, LANES, "plain", float((IDX_HEADS * IDX_DIM) ** -0.5)),
                        (o_qm + MEM_WIDTH + LANES, LANES, "ln_rope", 1.0)]
            dts = [BF16, BF16, BF16, BF16, BF16, F32, BF16]
            lng = jnp.tile(idx_kn_g[j], 2).reshape(1, LANES)
            lnb = jnp.tile(idx_kn_b[j], 2).reshape(1, LANES)
            q, k, v, qi, qm, wi, ki = _inproj(xb, w_new, tables, lng, lnb, sections, dts)
            mix = _dsa(b3(q), b3(k), b3(v), b3(qi), b3(wi), b3(ki)).reshape(t, MIX_WIDTH)
            wmix = wo[:MIX_WIDTH]
        elif kind == 1:
            w = w_in_b[j]
            vpad_w = B_GROUP_HEADS * V_HEAD_PAD
            wv = jnp.pad(w[:, 2 * MIX_WIDTH:m3].reshape(d, B_GROUP_HEADS, B_V_DIM),
                         ((0, 0), (0, 0), (0, V_HEAD_PAD - B_V_DIM))).reshape(d, vpad_w)
            w_new = jnp.concatenate([w[:, :2 * MIX_WIDTH], wv, w[:, m3:]], axis=1).astype(BF16)
            sections = [(0, MIX_WIDTH, "rope", SCALE), (MIX_WIDTH, MIX_WIDTH, "rope", 1.0),
                        (2 * MIX_WIDTH, vpad_w, "plain", 1.0),
                        (2 * MIX_WIDTH + vpad_w, MEM_WIDTH, "plain", SCALE)]
            q, k, v, qm = _inproj(xb, w_new, tables, dummy_ln, dummy_ln, sections, [BF16] * 4)
            mix = _dilated(b3(q), b3(k), b3(v)).reshape(t, vpad_w)
            wmix = jnp.pad(wo[:MIX_WIDTH].reshape(B_GROUP_HEADS, B_V_DIM, d),
                           ((0, 0), (0, V_HEAD_PAD - B_V_DIM), (0, 0))).reshape(vpad_w, d)
        else:
            w_new = w_in_c[j].astype(BF16)
            sections = [(0, MIX_WIDTH, "rope", SCALE), (MIX_WIDTH, MIX_WIDTH, "rope", 1.0),
                        (2 * MIX_WIDTH, MIX_WIDTH, "plain", 1.0), (m3, MEM_WIDTH, "plain", SCALE)]
            q, k, v, qm = _inproj(xb, w_new, tables, dummy_ln, dummy_ln, sections, [BF16] * 4)
            k3 = b3(k)
            mix = _moba(b3(q), k3, b3(v), _kmean(k3)).reshape(t, MIX_WIDTH)
            wmix = wo[:MIX_WIDTH]
        xf, xb = _post(alpha, mix, qm, mkv_all[i], wmix.astype(BF16), wo[MIX_WIDTH:].astype(BF16),
                       xf, ln1_g[i], ln1_b[i], seq)
        xf, xb = _ffn(alpha, xb, xf, w_gate_up[i].astype(BF16), w_down[i].astype(BF16),
                      ln2_g[i], ln2_b[i])
    return xf.reshape(bsz, seq, d)
```

```python
import functools

import jax
import jax.numpy as jnp
import numpy as np
from jax import lax
from jax.experimental import pallas as pl
from jax.experimental.pallas import tpu as pltpu

F32 = jnp.float32
BF16 = jnp.bfloat16
I32 = jnp.int32

HEAD_DIM = 64
N_HEADS = 12
MIX_WIDTH = N_HEADS * HEAD_DIM
N_MEM_HEADS = 4
MEM_WIDTH = N_MEM_HEADS * HEAD_DIM
ROPE_DIM = HEAD_DIM // 4
ROPE_HALF = ROPE_DIM // 2
ROPE_THETA = 500000.0
IDX_HEADS = 8
IDX_DIM = 64
TOPK_MAX = 256
DIL_PAIRS = ((128, 1), (512, 4), (2048, 16))
B_GROUP_HEADS = 4
B_V_DIM = MIX_WIDTH // B_GROUP_HEADS
MOBA_BLOCK = 256
MOBA_TOPK = 3
LN_EPS = 1e-5
SCALE = HEAD_DIM ** -0.5
LOG2E = 1.4426950408889634
Q_SCALE = SCALE * LOG2E

LANES = 128
SUBLANES = 8
VMEM_LIMIT_BYTES = 56 * 1024 * 1024

TILE = 256
ONES_ROWS = 16
HEADS_PER_STEP = 12
NEG = -1e30
INT_MIN = np.int32(-2 ** 31)


def _cparams(n_axes):
    return pltpu.CompilerParams(dimension_semantics=("arbitrary",) * n_axes,
                                vmem_limit_bytes=VMEM_LIMIT_BYTES)


def _layer_norm(y, g, b):
    mu = jnp.mean(y, axis=-1, keepdims=True)
    yc = y - mu
    var = jnp.mean(yc * yc, axis=-1, keepdims=True)
    return yc * lax.rsqrt(var + LN_EPS) * g + b


def _dot_nt(a, b):
    return lax.dot_general(a, b, (((1,), (1,)), ((), ())), preferred_element_type=F32)


def _half_masks(rows):
    lane = lax.broadcasted_iota(I32, (rows, LANES), 1)
    return lane < HEAD_DIM, lane >= HEAD_DIM


def _store_heads(q_ref, qh_ref):
    lo, hi = _half_masks(q_ref.shape[1])
    for head in range(qh_ref.shape[0]):
        qp = q_ref[0, :, _head_cols(head)]
        qh_ref[head] = jnp.where(hi if head % 2 else lo, qp, jnp.zeros_like(qp))


def _head_cols(head):
    return slice((head // 2) * LANES, (head // 2 + 1) * LANES)


def _key_tile(j):
    return pl.ds(pl.multiple_of(j * TILE, TILE), TILE)


def _flash_init(dv):
    return (jnp.full((1, TILE), NEG, F32), jnp.zeros((dv + ONES_ROWS, TILE), F32))


def _flash_step(states, qk, biases, vts):
    n = len(states)
    ones = jnp.ones((ONES_ROWS, TILE), BF16)
    logits = [_dot_nt(*qk[h]) for h in range(n)]
    mids = []
    for h, (m_old, acc) in enumerate(states):
        s = logits[h] + biases[h]
        m_new = jnp.maximum(m_old, jnp.max(s, axis=0, keepdims=True))
        p = jnp.exp2(s - m_new).astype(BF16)
        mids.append((m_new, jnp.exp2(m_old - m_new), p))
    out = []
    for (m_new, alpha, p), (_, acc), vt in zip(mids, states, vts):
        pv = jnp.dot(jnp.concatenate([vt, ones], axis=0), p, preferred_element_type=F32)
        out.append((m_new, alpha * acc + pv))
    return tuple(out)


def _flash_finish(o_ref, states, col0):
    out_t = jnp.concatenate([acc[:-ONES_ROWS] * (1.0 / acc[-1:]) for _, acc in states], axis=0)
    for c in range(out_t.shape[0] // LANES):
        o_ref[0, :, col0 + c * LANES:col0 + (c + 1) * LANES] = (
            out_t[c * LANES:(c + 1) * LANES, :].T.astype(o_ref.dtype))


def _memkv_kernel(mem_ref, g_ref, b_ref, w_ref, o_ref):
    mn = _layer_norm(mem_ref[0], g_ref[...], b_ref[...])
    o_ref[0, 0] = jnp.dot(mn.astype(BF16), w_ref[0], preferred_element_type=F32).astype(BF16)


def _memkv(mem, g, b, w_bf16):
    depth = w_bf16.shape[0]
    bsz, n_mem, d = mem.shape
    wout = w_bf16.shape[2]
    return pl.pallas_call(
        _memkv_kernel,
        grid=(depth, bsz),
        in_specs=[pl.BlockSpec((1, n_mem, d), lambda i, b_: (b_, 0, 0)),
                  pl.BlockSpec((1, d), lambda i, b_: (0, 0)),
                  pl.BlockSpec((1, d), lambda i, b_: (0, 0)),
                  pl.BlockSpec((1, d, wout), lambda i, b_: (i, 0, 0))],
        out_specs=pl.BlockSpec((1, 1, n_mem, wout), lambda i, b_: (i, b_, 0, 0)),
        out_shape=jax.ShapeDtypeStruct((depth, bsz, n_mem, wout), BF16),
        compiler_params=_cparams(2), name="memkv",
    )(mem, g.reshape(1, d), b.reshape(1, d), w_bf16)


def _rope(h, c, s1, s2):
    return h * c + pltpu.roll(h, ROPE_HALF, 1) * s1 + pltpu.roll(h, LANES - ROPE_HALF, 1) * s2


def _inproj_kernel(sections, tsections, x_ref, w_ref, wt_ref, c_ref, s1_ref, s2_ref, lng_ref, lnb_ref,
                   *out_refs):
    x = x_ref[...]
    c, s1, s2 = c_ref[...], s1_ref[...], s2_ref[...]
    for (start, width, kind, scale), o_ref in zip(sections, out_refs):
        h = jnp.dot(x, w_ref[:, start:start + width], preferred_element_type=F32)
        if kind == "ln_rope":
            h = _layer_norm(h, lng_ref[...], lnb_ref[...])
        for ch in range(width // LANES):
            hc = h[:, ch * LANES:(ch + 1) * LANES]
            if kind in ("rope", "ln_rope"):
                hc = _rope(hc, c, s1, s2)
            if scale != 1.0:
                hc = hc * scale
            o_ref[:, ch * LANES:(ch + 1) * LANES] = hc.astype(o_ref.dtype)
    for (start, rows, kind, scale), o_ref in zip(tsections, out_refs[len(sections):]):
        ht = _dot_nt(wt_ref[start:start + rows, :], x)
        if scale != 1.0:
            ht = ht * scale
        if kind == "tiles":
            for tl in range(o_ref.shape[0]):
                o_ref[tl] = ht[:, tl * TILE:(tl + 1) * TILE].astype(o_ref.dtype)
        else:
            o_ref[...] = ht.astype(o_ref.dtype)


def _inproj(xb, w_bf16, wt_bf16, tables, lng, lnb, sections, out_dtypes, tsections, tout_dtypes, tm=512):
    t, d = xb.shape
    row = lambda i: (i, 0)
    fixed = lambda i: (0, 0)
    out_shape = [jax.ShapeDtypeStruct((t, sec[1]), dt) for sec, dt in zip(sections, out_dtypes)]
    out_specs = [pl.BlockSpec((tm, sec[1]), row) for sec in sections]
    for (_, rows, kind, _), dt in zip(tsections, tout_dtypes):
        if kind == "tiles":
            out_shape.append(jax.ShapeDtypeStruct((t // TILE, rows, TILE), dt))
            out_specs.append(pl.BlockSpec((tm // TILE, rows, TILE), lambda i: (i, 0, 0)))
        else:
            out_shape.append(jax.ShapeDtypeStruct((rows, t), dt))
            out_specs.append(pl.BlockSpec((rows, tm), lambda i: (0, i)))
    return pl.pallas_call(
        functools.partial(_inproj_kernel, tuple(sections), tuple(tsections)),
        grid=(t // tm,),
        in_specs=[pl.BlockSpec((tm, d), row),
                  pl.BlockSpec(w_bf16.shape, fixed),
                  pl.BlockSpec(wt_bf16.shape, fixed),
                  pl.BlockSpec((tm, LANES), row),
                  pl.BlockSpec((tm, LANES), row),
                  pl.BlockSpec((tm, LANES), row),
                  pl.BlockSpec((1, LANES), fixed),
                  pl.BlockSpec((1, LANES), fixed)],
        out_specs=out_specs,
        out_shape=out_shape,
        compiler_params=_cparams(1), name="inproj",
    )(xb, w_bf16, wt_bf16, *tables, lng, lnb)


def _dsa_kernel(topk, q_ref, k_ref, vt_ref, qi_ref, wit_ref, ki_ref, o_ref, keys_ref, qh_ref, qih_ref):
    i = pl.program_id(1)
    nkt = i + 1
    krow = lax.broadcasted_iota(I32, (TILE, TILE), 0)
    qcol = lax.broadcasted_iota(I32, (TILE, TILE), 1)
    _store_heads(q_ref, qh_ref)
    _store_heads(qi_ref, qih_ref)

    wt = wit_ref[...]

    def score_tile(j, carry):
        kt = ki_ref[0, _key_tile(j), :]
        sc = jnp.zeros((TILE, TILE), F32)
        for h in range(IDX_HEADS):
            sc = sc + wt[h:h + 1, :] * jnp.maximum(_dot_nt(kt, qih_ref[h]), 0.0)
        bits = lax.bitcast_convert_type(sc, I32)
        key = bits ^ (lax.shift_right_arithmetic(bits, 31) & np.int32(0x7FFFFFFF))
        causal = (j * TILE + krow) <= (i * TILE + qcol)
        keys_ref[j] = jnp.where(causal, key, INT_MIN)
        return carry

    lax.fori_loop(0, nkt, score_tile, 0)

    def count(pred):
        def body(j, acc):
            hits = jnp.where(pred(keys_ref[j]), 1.0, 0.0)
            return acc + jnp.sum(hits.reshape(TILE // SUBLANES, SUBLANES, TILE), axis=0)
        acc = lax.fori_loop(0, nkt, body, jnp.zeros((SUBLANES, TILE), F32))
        return jnp.sum(acc, axis=0, keepdims=True)

    t_q = i * TILE + lax.broadcasted_iota(I32, (1, TILE), 1)
    k_q = jnp.minimum(topk, t_q + 1).astype(F32)
    zero = jnp.zeros((1, TILE), I32)
    thr0 = jnp.where(count(lambda kt: kt >= zero) >= k_q, zero, zero + INT_MIN)

    def bisect(it, thr):
        cand = thr | lax.shift_left(np.int32(1), 30 - it)
        return jnp.where(count(lambda kt: kt >= cand) >= k_q, cand, thr)

    thr = lax.fori_loop(0, 31, bisect, thr0)

    need = k_q - count(lambda kt: kt > thr)
    n_eq = count(lambda kt: kt == thr)
    has_tie = jnp.max(jnp.where(n_eq > need, 1.0, 0.0))

    @pl.when(has_tie > 0.0)
    def _():
        tril = jnp.where(qcol <= krow, 1.0, 0.0).astype(BF16)

        def fix(j, seen):
            kt = keys_ref[j]
            eq = kt == thr
            eqf = jnp.where(eq, 1.0, 0.0)
            pref = jnp.dot(tril, eqf.astype(BF16), preferred_element_type=F32) + seen
            keys_ref[j] = jnp.where(eq & (pref > need), INT_MIN, kt)
            return seen + jnp.sum(eqf, axis=0, keepdims=True)

        lax.fori_loop(0, nkt, fix, jnp.zeros((1, TILE), F32))

    for h0 in range(0, N_HEADS, HEADS_PER_STEP):
        heads = range(h0, h0 + HEADS_PER_STEP)

        def body(j, states, heads=heads):
            ks = _key_tile(j)
            bias = jnp.where(keys_ref[j] >= thr, 0.0, NEG)
            return _flash_step(states, [(k_ref[0, ks, _head_cols(h)], qh_ref[h]) for h in heads],
                               [bias] * len(heads),
                               [vt_ref[j, h * HEAD_DIM:(h + 1) * HEAD_DIM, :] for h in heads])

        states = lax.fori_loop(0, nkt, body, tuple(_flash_init(HEAD_DIM) for _ in heads))
        _flash_finish(o_ref, states, h0 * HEAD_DIM)


def _dsa(q, k, vt, qi, wit, ki):
    bsz, s, _ = q.shape
    topk = min(TOPK_MAX, s // 4)
    nt = s // TILE
    qtile = lambda w: pl.BlockSpec((1, TILE, w), lambda b, i: (b, i, 0))
    full = lambda w: pl.BlockSpec((1, s, w), lambda b, i: (b, 0, 0))
    return pl.pallas_call(
        functools.partial(_dsa_kernel, topk),
        grid=(bsz, nt),
        in_specs=[qtile(MIX_WIDTH), full(MIX_WIDTH),
                  pl.BlockSpec((nt, MIX_WIDTH, TILE), lambda b, i: (b, 0, 0)),
                  qtile(IDX_HEADS * IDX_DIM),
                  pl.BlockSpec((IDX_HEADS, TILE), lambda b, i: (0, b * nt + i)),
                  full(LANES)],
        out_specs=qtile(MIX_WIDTH),
        out_shape=jax.ShapeDtypeStruct((bsz, s, MIX_WIDTH), BF16),
        scratch_shapes=[pltpu.VMEM((nt, TILE, TILE), I32),
                        pltpu.VMEM((N_HEADS, TILE, LANES), BF16),
                        pltpu.VMEM((IDX_HEADS, TILE, LANES), BF16)],
        compiler_params=_cparams(2), name="dsa",
    )(q, k, vt, qi, wit, ki)


def _dilated_kernel(q_ref, k_ref, vt_ref, o_ref, qh_ref):
    i = pl.program_id(1)
    rel = (lax.broadcasted_iota(I32, (TILE, TILE), 1)
           - lax.broadcasted_iota(I32, (TILE, TILE), 0))
    _store_heads(q_ref, qh_ref)
    states = tuple(_flash_init(B_V_DIM) for _ in range(B_GROUP_HEADS))
    for g, (window, dilation) in enumerate(DIL_PAIRS):
        heads = [g * B_GROUP_HEADS + hv for hv in range(B_GROUP_HEADS)]
        nback = -(-window // TILE)

        def body(j, states, heads=heads, window=window, dilation=dilation):
            ks = _key_tile(j)
            dist = rel + (i - j) * TILE
            ok = (dist >= 0) & (dist <= window) & ((dist & (dilation - 1)) == 0)
            bias = jnp.where(ok, 0.0, NEG)
            return _flash_step(states, [(k_ref[0, ks, _head_cols(h)], qh_ref[h]) for h in heads],
                               [bias] * len(heads),
                               [vt_ref[j, hv * B_V_DIM:(hv + 1) * B_V_DIM, :] for hv in range(B_GROUP_HEADS)])

        states = lax.fori_loop(jnp.maximum(i - nback, 0), i + 1, body, states)
    _flash_finish(o_ref, states, 0)


def _dilated(q, k, vt):
    bsz, s, w = q.shape
    nt = s // TILE
    return pl.pallas_call(
        _dilated_kernel,
        grid=(bsz, nt),
        in_specs=[pl.BlockSpec((1, TILE, w), lambda b, i: (b, i, 0)),
                  pl.BlockSpec((1, s, w), lambda b, i: (b, 0, 0)),
                  pl.BlockSpec((nt, w, TILE), lambda b, i: (b, 0, 0))],
        out_specs=pl.BlockSpec((1, TILE, w), lambda b, i: (b, i, 0)),
        out_shape=jax.ShapeDtypeStruct((bsz, s, w), BF16),
        scratch_shapes=[pltpu.VMEM((N_HEADS, TILE, LANES), BF16)],
        compiler_params=_cparams(2), name="dilated",
    )(q, k, vt)


def _kmean_kernel(nblk, k_ref, o_ref):
    o_ref[0] = jnp.zeros(o_ref.shape[1:], o_ref.dtype)
    for n in range(nblk):
        kb = k_ref[0, n * MOBA_BLOCK:(n + 1) * MOBA_BLOCK, :].astype(F32)
        o_ref[0, n:n + 1, :] = jnp.mean(kb, axis=0, keepdims=True).astype(o_ref.dtype)


def _kmean(k):
    bsz, s, w = k.shape
    return pl.pallas_call(
        functools.partial(_kmean_kernel, s // MOBA_BLOCK),
        grid=(bsz,),
        in_specs=[pl.BlockSpec((1, s, w), lambda b: (b, 0, 0))],
        out_specs=pl.BlockSpec((1, LANES, w), lambda b: (b, 0, 0)),
        out_shape=jax.ShapeDtypeStruct((bsz, LANES, w), BF16),
        compiler_params=_cparams(1), name="kmean",
    )(k)


def _moba_kernel(nbp, q_ref, k_ref, vt_ref, km_ref, o_ref, sel_ref, qh_ref):
    i = pl.program_id(1)
    blk = lax.broadcasted_iota(I32, (nbp, TILE), 0)
    blk_f = blk.astype(F32)
    krow = lax.broadcasted_iota(I32, (TILE, TILE), 0)
    qcol = lax.broadcasted_iota(I32, (TILE, TILE), 1)
    causal_bias = jnp.where(krow <= qcol, 0.0, NEG)
    _store_heads(q_ref, qh_ref)

    for h in range(N_HEADS):
        g = jnp.where(blk < i, _dot_nt(km_ref[0, :, _head_cols(h)], qh_ref[h])[:nbp], -jnp.inf)
        sel = jnp.zeros((nbp, TILE), F32)
        for r in range(MOBA_TOPK):
            mx = jnp.max(g, axis=0, keepdims=True)
            idx = jnp.min(jnp.where(g == mx, blk_f, float(nbp)), axis=0, keepdims=True)
            hit = blk_f == idx
            sel = jnp.where(hit & (jnp.full((nbp, TILE), r, I32) < i), 1.0, sel)
            g = jnp.where(hit, -jnp.inf, g)
        sel_ref[h] = sel

    for h0 in range(0, N_HEADS, HEADS_PER_STEP):
        heads = range(h0, h0 + HEADS_PER_STEP)

        def step(j, states, bias_of, heads=heads):
            ks = _key_tile(j)
            return _flash_step(states, [(k_ref[0, ks, _head_cols(h)], qh_ref[h]) for h in heads],
                               [bias_of(h) for h in heads],
                               [vt_ref[j, h * HEAD_DIM:(h + 1) * HEAD_DIM, :] for h in heads])

        def body(n, states, step=step):
            return step(n, states, lambda h: jnp.where(sel_ref[h, pl.ds(n, 1), :] > 0.5, 0.0, NEG))

        states = lax.fori_loop(0, i, body, tuple(_flash_init(HEAD_DIM) for _ in heads))
        states = step(i, states, lambda h: causal_bias)
        _flash_finish(o_ref, states, h0 * HEAD_DIM)


def _moba(q, k, vt, kmean):
    bsz, s, w = q.shape
    nt = s // TILE
    nbp = -(-nt // SUBLANES) * SUBLANES
    return pl.pallas_call(
        functools.partial(_moba_kernel, nbp),
        grid=(bsz, nt),
        in_specs=[pl.BlockSpec((1, TILE, w), lambda b, i: (b, i, 0)),
                  pl.BlockSpec((1, s, w), lambda b, i: (b, 0, 0)),
                  pl.BlockSpec((nt, w, TILE), lambda b, i: (b, 0, 0)),
                  pl.BlockSpec((1, LANES, w), lambda b, i: (b, 0, 0))],
        out_specs=pl.BlockSpec((1, TILE, w), lambda b, i: (b, i, 0)),
        out_shape=jax.ShapeDtypeStruct((bsz, s, w), BF16),
        scratch_shapes=[pltpu.VMEM((N_HEADS, nbp, TILE), F32),
                        pltpu.VMEM((N_HEADS, TILE, LANES), BF16)],
        compiler_params=_cparams(2), name="moba",
    )(q, k, vt, kmean)


def _post_kernel(alpha, mix_ref, qm_ref, mkv_ref, wmix_ref, wmem_ref, x_ref, g_ref, b_ref,
                 xo_ref, xb_ref):
    rows = mix_ref.shape[0]
    lo, hi = _half_masks(rows)
    mo = []
    for p in range(MEM_WIDTH // LANES):
        qp = qm_ref[:, p * LANES:(p + 1) * LANES]
        mk = mkv_ref[0, :, p * LANES:(p + 1) * LANES]
        mv = mkv_ref[0, :, MEM_WIDTH + p * LANES:MEM_WIDTH + (p + 1) * LANES]
        outs = []
        for half in (lo, hi):
            s = _dot_nt(jnp.where(half, qp, jnp.zeros_like(qp)), mk)
            e = jnp.exp(s - jnp.max(s, axis=1, keepdims=True))
            pv = jnp.dot(e.astype(BF16), mv, preferred_element_type=F32)
            outs.append(pv / jnp.sum(e, axis=1, keepdims=True))
        mo.append(jnp.where(lo, outs[0], outs[1]).astype(BF16))
    mixed = jnp.dot(mix_ref[...], wmix_ref[...], preferred_element_type=F32)
    for p, mo_p in enumerate(mo):
        mixed = mixed + jnp.dot(mo_p, wmem_ref[p * LANES:(p + 1) * LANES, :],
                                preferred_element_type=F32)
    y = _layer_norm(alpha * x_ref[...] + mixed, g_ref[...], b_ref[...])
    xo_ref[...] = y
    xb_ref[...] = y.astype(BF16)


def _post(alpha, mix, qm, mkv, wmix, wmem, xf, g, b, seq, tm=512):
    t, d = xf.shape
    wm = mix.shape[1]
    per_batch = seq // tm
    row = lambda i: (i, 0)
    fixed = lambda i: (0, 0)
    return pl.pallas_call(
        functools.partial(_post_kernel, alpha),
        grid=(t // tm,),
        in_specs=[pl.BlockSpec((tm, wm), row),
                  pl.BlockSpec((tm, MEM_WIDTH), row),
                  pl.BlockSpec((1,) + mkv.shape[1:], lambda i: (i // per_batch, 0, 0)),
                  pl.BlockSpec(wmix.shape, fixed),
                  pl.BlockSpec(wmem.shape, fixed),
                  pl.BlockSpec((tm, d), row),
                  pl.BlockSpec((1, d), fixed),
                  pl.BlockSpec((1, d), fixed)],
        out_specs=[pl.BlockSpec((tm, d), row), pl.BlockSpec((tm, d), row)],
        out_shape=[jax.ShapeDtypeStruct((t, d), F32), jax.ShapeDtypeStruct((t, d), BF16)],
        compiler_params=_cparams(1), name="post",
    )(mix, qm, mkv, wmix, wmem, xf, g.reshape(1, d), b.reshape(1, d))


def _ffn_kernel(alpha, xb_ref, wg_ref, wu_ref, wd_ref, x_ref, g_ref, b_ref, xo_ref, xbo_ref, acc_ref):
    j = pl.program_id(1)

    @pl.when(j == 0)
    def _():
        acc_ref[...] = jnp.zeros_like(acc_ref)

    xb = xb_ref[...]
    gate = jnp.dot(xb, wg_ref[...], preferred_element_type=F32)
    up = jnp.dot(xb, wu_ref[...], preferred_element_type=F32)
    h = (gate * jax.nn.sigmoid(gate) * up).astype(BF16)
    acc_ref[...] += jnp.dot(h, wd_ref[...], preferred_element_type=F32)

    @pl.when(j == pl.num_programs(1) - 1)
    def _():
        y = _layer_norm(alpha * x_ref[...] + acc_ref[...], g_ref[...], b_ref[...])
        xo_ref[...] = y
        xbo_ref[...] = y.astype(BF16)


def _ffn(alpha, xb, xf, wgu, wd, g, b, tm=1024, tf=256):
    t, d = xf.shape
    dff = wd.shape[0]
    nff = dff // tf
    row = lambda i, j: (i, 0)
    fixed = lambda i, j: (0, 0)
    return pl.pallas_call(
        functools.partial(_ffn_kernel, alpha),
        grid=(t // tm, nff),
        in_specs=[pl.BlockSpec((tm, d), row),
                  pl.BlockSpec((d, tf), lambda i, j: (0, j)),
                  pl.BlockSpec((d, tf), lambda i, j: (0, nff + j)),
                  pl.BlockSpec((tf, d), lambda i, j: (j, 0)),
                  pl.BlockSpec((tm, d), row),
                  pl.BlockSpec((1, d), fixed),
                  pl.BlockSpec((1, d), fixed)],
        out_specs=[pl.BlockSpec((tm, d), row), pl.BlockSpec((tm, d), row)],
        out_shape=[jax.ShapeDtypeStruct((t, d), F32), jax.ShapeDtypeStruct((t, d), BF16)],
        scratch_shapes=[pltpu.VMEM((tm, d), F32)],
        compiler_params=_cparams(2), name="ffn",
    )(xb, wgu, wgu, wd, xf, g.reshape(1, d), b.reshape(1, d))


def _rope_tables(positions):
    inv = ROPE_THETA ** (-jnp.arange(ROPE_HALF, dtype=F32) / ROPE_HALF)
    ang = positions.astype(F32)[..., None] * inv
    cos, sin = jnp.cos(ang), jnp.sin(ang)
    rest = HEAD_DIM - ROPE_DIM
    pad = lambda a, before, after, val: jnp.pad(a, ((0, 0), (0, 0), (before, after)), constant_values=val)
    c = pad(jnp.concatenate([cos, cos], -1), 0, rest, 1.0)
    s1 = pad(sin, ROPE_HALF, rest, 0.0)
    s2 = pad(-sin, 0, ROPE_HALF + rest, 0.0)
    t = positions.shape[0] * positions.shape[1]
    return tuple(jnp.tile(a, (1, 1, LANES // HEAD_DIM)).reshape(t, LANES) for a in (c, s1, s2))


def kernel(x, mem, positions, mem_ln_g, mem_ln_b, w_in_a, idx_kn_g, idx_kn_b, w_in_b, w_in_c,
           w_mem_kv, w_out, ln1_g, ln1_b, w_gate_up, w_down, ln2_g, ln2_b):
    bsz, seq, d = x.shape
    t = bsz * seq
    depth = w_out.shape[0]
    alpha = (2 * depth) ** 0.25
    assert seq % TILE == 0 and TILE == MOBA_BLOCK

    tables = _rope_tables(positions)
    mkv_all = _memkv(mem, mem_ln_g, mem_ln_b, w_mem_kv.astype(BF16))
    xf = x.reshape(t, d)
    xb = xf.astype(BF16)
    dummy_ln = jnp.zeros((1, LANES), F32)
    m2, m3 = 2 * MIX_WIDTH, 3 * MIX_WIDTH
    b3 = lambda a: a.reshape(bsz, seq, a.shape[-1])
    qk_sections = [(0, MIX_WIDTH, "rope", Q_SCALE), (MIX_WIDTH, MIX_WIDTH, "rope", 1.0)]

    for i in range(depth):
        kind, j = i % 3, i // 3
        wo = w_out[i]
        if kind == 0:
            w = w_in_a[j]
            c_wi = m3 + IDX_HEADS * IDX_DIM
            c_ki, c_qm = c_wi + IDX_HEADS, c_wi + IDX_HEADS + IDX_DIM
            w_ki = w[:, c_ki:c_qm]
            w_new = jnp.concatenate([w[:, :m2], w[:, m3:c_wi], w[:, c_qm:], w_ki, w_ki], axis=1).astype(BF16)
            wt_new = jnp.concatenate([w[:, m2:m3], w[:, c_wi:c_ki]], axis=1).T.astype(BF16)
            o_qm = m2 + IDX_HEADS * IDX_DIM
            sections = qk_sections + [(m2, IDX_HEADS * IDX_DIM, "rope", 1.0),
                                      (o_qm, MEM_WIDTH, "plain", SCALE),
                                      (o_qm + MEM_WIDTH, LANES, "ln_rope", 1.0)]
            tsections = [(0, MIX_WIDTH, "tiles", 1.0),
                         (MIX_WIDTH, IDX_HEADS, "flat", float((IDX_HEADS * IDX_DIM) ** -0.5))]
            lng = jnp.tile(idx_kn_g[j], 2).reshape(1, LANES)
            lnb = jnp.tile(idx_kn_b[j], 2).reshape(1, LANES)
            q, k, qi, qm, ki, vt, wit = _inproj(xb, w_new, wt_new, tables, lng, lnb,
                                                sections, [BF16] * 5, tsections, [BF16, F32])
            mix = _dsa(b3(q), b3(k), vt, b3(qi), wit, b3(ki))
        else:
            w = w_in_b[j] if kind == 1 else w_in_c[j]
            w_new = jnp.concatenate([w[:, :m2], w[:, m3:]], axis=1).astype(BF16)
            wt_new = w[:, m2:m3].T.astype(BF16)
            sections = qk_sections + [(m2, MEM_WIDTH, "plain", SCALE)]
            q, k, qm, vt = _inproj(xb, w_new, wt_new, tables, dummy_ln, dummy_ln,
                                   sections, [BF16] * 3, [(0, MIX_WIDTH, "tiles", 1.0)], [BF16])
            if kind == 1:
                mix = _dilated(b3(q), b3(k), vt)
            else:
                k3 = b3(k)
                mix = _moba(b3(q), k3, vt, _kmean(k3))
        xf, xb = _post(alpha, mix.reshape(t, MIX_WIDTH), qm, mkv_all[i], wo[:MIX_WIDTH].astype(BF16),
                       wo[MIX_WIDTH:].astype(BF16), xf, ln1_g[i], ln1_b[i], seq)
        xf, xb = _ffn(alpha, xb, xf, w_gate_up[i].astype(BF16), w_down[i].astype(BF16),
                      ln2_g[i], ln2_b[i])
    return xf.reshape(bsz, seq, d)
```

```python
import functools

import jax
import jax.numpy as jnp
import numpy as np
from jax import lax
from jax.experimental import pallas as pl
from jax.experimental.pallas import tpu as pltpu

F32 = jnp.float32
BF16 = jnp.bfloat16
I32 = jnp.int32
I16 = jnp.int16

HEAD_DIM = 64
N_HEADS = 12
MIX_WIDTH = N_HEADS * HEAD_DIM
N_MEM_HEADS = 4
MEM_WIDTH = N_MEM_HEADS * HEAD_DIM
ROPE_DIM = HEAD_DIM // 4
ROPE_HALF = ROPE_DIM // 2
ROPE_THETA = 500000.0
IDX_HEADS = 8
IDX_DIM = 64
TOPK_MAX = 256
DIL_PAIRS = ((128, 1), (512, 4), (2048, 16))
B_GROUP_HEADS = 4
B_V_DIM = MIX_WIDTH // B_GROUP_HEADS
MOBA_BLOCK = 256
MOBA_TOPK = 3
LN_EPS = 1e-5
SCALE = HEAD_DIM ** -0.5
LOG2E = 1.4426950408889634
Q_SCALE = SCALE * LOG2E

LANES = 128
SUBLANES = 8
VMEM_LIMIT_BYTES = 56 * 1024 * 1024

TILE = 256
ONES_ROWS = 16
HEADS_PER_STEP = 12
NEG = -1e30
INT_MIN = np.int32(-2 ** 31)
I16_MIN = -2 ** 15
PACKED_ROWS = 2 * SUBLANES


def _cparams(n_axes):
    return pltpu.CompilerParams(dimension_semantics=("arbitrary",) * n_axes,
                                vmem_limit_bytes=VMEM_LIMIT_BYTES)


def _layer_norm(y, g, b):
    mu = jnp.mean(y, axis=-1, keepdims=True)
    yc = y - mu
    var = jnp.mean(yc * yc, axis=-1, keepdims=True)
    return yc * lax.rsqrt(var + LN_EPS) * g + b


def _dot_nt(a, b):
    return lax.dot_general(a, b, (((1,), (1,)), ((), ())), preferred_element_type=F32)


def _half_masks(rows):
    lane = lax.broadcasted_iota(I32, (rows, LANES), 1)
    return lane < HEAD_DIM, lane >= HEAD_DIM


def _store_heads(q_ref, qh_ref):
    lo, hi = _half_masks(q_ref.shape[1])
    for head in range(qh_ref.shape[0]):
        qp = q_ref[0, :, _head_cols(head)]
        qh_ref[head] = jnp.where(hi if head % 2 else lo, qp, jnp.zeros_like(qp))


def _head_cols(head):
    return slice((head // 2) * LANES, (head // 2 + 1) * LANES)


def _key_tile(j, ntiles=1):
    return pl.ds(pl.multiple_of(j * TILE, TILE), ntiles * TILE)


def _flash_over_tiles(states, lo, hi, step):
    npairs = (hi - lo) // 2
    states = lax.fori_loop(0, npairs, lambda jj, st: step(st, lo + 2 * jj, 2), states)
    return lax.cond((hi - lo) % 2 == 1, lambda st: step(st, hi - 1, 1), lambda st: st, states)


def _values_t(vt_ref, j, ntiles, rows):
    return jnp.concatenate([vt_ref[j + c, rows, :] for c in range(ntiles)], axis=1)


def _flash_init(dv):
    return (jnp.full((1, TILE), NEG, F32), jnp.zeros((dv + ONES_ROWS, TILE), F32))


def _flash_step(states, qk, biases, vts):
    n = len(states)
    ones = jnp.ones((ONES_ROWS, vts[0].shape[1]), BF16)
    logits = []
    for h in range(n):
        raw = _dot_nt(*qk[h])
        logits.append(jnp.concatenate([raw[c * TILE:(c + 1) * TILE] + b for c, b in enumerate(biases[h])],
                                      axis=0))
    mids = []
    for (m_old, acc), s in zip(states, logits):
        m_new = jnp.maximum(m_old, jnp.max(s, axis=0, keepdims=True))
        p = jnp.exp2(s - m_new).astype(BF16)
        mids.append((m_new, jnp.exp2(m_old - m_new), p))
    out = []
    for (m_new, alpha, p), (_, acc), vt in zip(mids, states, vts):
        pv = jnp.dot(jnp.concatenate([vt, ones], axis=0), p, preferred_element_type=F32)
        out.append((m_new, alpha * acc + pv))
    return tuple(out)


def _flash_finish(o_ref, states, col0):
    out_t = jnp.concatenate([acc[:-ONES_ROWS] * (1.0 / acc[-1:]) for _, acc in states], axis=0)
    for c in range(out_t.shape[0] // LANES):
        o_ref[0, :, col0 + c * LANES:col0 + (c + 1) * LANES] = (
            out_t[c * LANES:(c + 1) * LANES, :].T.astype(o_ref.dtype))


def _memkv_kernel(mem_ref, g_ref, b_ref, w_ref, o_ref):
    mn = _layer_norm(mem_ref[0], g_ref[...], b_ref[...])
    o_ref[0, 0] = jnp.dot(mn.astype(BF16), w_ref[0], preferred_element_type=F32).astype(BF16)


def _memkv(mem, g, b, w_bf16):
    depth = w_bf16.shape[0]
    bsz, n_mem, d = mem.shape
    wout = w_bf16.shape[2]
    return pl.pallas_call(
        _memkv_kernel,
        grid=(depth, bsz),
        in_specs=[pl.BlockSpec((1, n_mem, d), lambda i, b_: (b_, 0, 0)),
                  pl.BlockSpec((1, d), lambda i, b_: (0, 0)),
                  pl.BlockSpec((1, d), lambda i, b_: (0, 0)),
                  pl.BlockSpec((1, d, wout), lambda i, b_: (i, 0, 0))],
        out_specs=pl.BlockSpec((1, 1, n_mem, wout), lambda i, b_: (i, b_, 0, 0)),
        out_shape=jax.ShapeDtypeStruct((depth, bsz, n_mem, wout), BF16),
        compiler_params=_cparams(2), name="memkv",
    )(mem, g.reshape(1, d), b.reshape(1, d), w_bf16)


def _rope(h, c, s1, s2):
    return h * c + pltpu.roll(h, ROPE_HALF, 1) * s1 + pltpu.roll(h, LANES - ROPE_HALF, 1) * s2


def _inproj_kernel(sections, tsections, x_ref, w_ref, wt_ref, c_ref, s1_ref, s2_ref, lng_ref, lnb_ref,
                   *out_refs):
    x = x_ref[...]
    c, s1, s2 = c_ref[...], s1_ref[...], s2_ref[...]
    for (start, width, kind, scale), o_ref in zip(sections, out_refs):
        h = jnp.dot(x, w_ref[:, start:start + width], preferred_element_type=F32)
        if kind == "ln_rope":
            h = _layer_norm(h, lng_ref[...], lnb_ref[...])
        for ch in range(width // LANES):
            hc = h[:, ch * LANES:(ch + 1) * LANES]
            if kind in ("rope", "ln_rope"):
                hc = _rope(hc, c, s1, s2)
            if scale != 1.0:
                hc = hc * scale
            o_ref[:, ch * LANES:(ch + 1) * LANES] = hc.astype(o_ref.dtype)
    for (start, rows, kind, scale), o_ref in zip(tsections, out_refs[len(sections):]):
        ht = _dot_nt(wt_ref[start:start + rows, :], x)
        if scale != 1.0:
            ht = ht * scale
        if kind == "tiles":
            for tl in range(o_ref.shape[0]):
                o_ref[tl] = ht[:, tl * TILE:(tl + 1) * TILE].astype(o_ref.dtype)
        else:
            o_ref[...] = ht.astype(o_ref.dtype)


def _inproj(xb, w_bf16, wt_bf16, tables, lng, lnb, sections, out_dtypes, tsections, tout_dtypes, tm=512):
    t, d = xb.shape
    row = lambda i: (i, 0)
    fixed = lambda i: (0, 0)
    out_shape = [jax.ShapeDtypeStruct((t, sec[1]), dt) for sec, dt in zip(sections, out_dtypes)]
    out_specs = [pl.BlockSpec((tm, sec[1]), row) for sec in sections]
    for (_, rows, kind, _), dt in zip(tsections, tout_dtypes):
        if kind == "tiles":
            out_shape.append(jax.ShapeDtypeStruct((t // TILE, rows, TILE), dt))
            out_specs.append(pl.BlockSpec((tm // TILE, rows, TILE), lambda i: (i, 0, 0)))
        else:
            out_shape.append(jax.ShapeDtypeStruct((rows, t), dt))
            out_specs.append(pl.BlockSpec((rows, tm), lambda i: (0, i)))
    return pl.pallas_call(
        functools.partial(_inproj_kernel, tuple(sections), tuple(tsections)),
        grid=(t // tm,),
        in_specs=[pl.BlockSpec((tm, d), row),
                  pl.BlockSpec(w_bf16.shape, fixed),
                  pl.BlockSpec(wt_bf16.shape, fixed),
                  pl.BlockSpec((tm, LANES), row),
                  pl.BlockSpec((tm, LANES), row),
                  pl.BlockSpec((tm, LANES), row),
                  pl.BlockSpec((1, LANES), fixed),
                  pl.BlockSpec((1, LANES), fixed)],
        out_specs=out_specs,
        out_shape=out_shape,
        compiler_params=_cparams(1), name="inproj",
    )(xb, w_bf16, wt_bf16, *tables, lng, lnb)


def _dsa_kernel(topk, q_ref, k_ref, vt_ref, qi_ref, wit_ref, ki_ref, o_ref,
                keys_ref, hi_ref, lo_ref, qh_ref, qih_ref):
    i = pl.program_id(1)
    nkt = i + 1
    krow = lax.broadcasted_iota(I32, (TILE, TILE), 0)
    qcol = lax.broadcasted_iota(I32, (TILE, TILE), 1)
    _store_heads(q_ref, qh_ref)
    _store_heads(qi_ref, qih_ref)

    wt = wit_ref[...]

    def score_tile(j, carry):
        kt = ki_ref[0, _key_tile(j), :]
        sc = jnp.zeros((TILE, TILE), F32)
        for h in range(IDX_HEADS):
            sc = sc + wt[h:h + 1, :] * jnp.maximum(_dot_nt(kt, qih_ref[h]), 0.0)
        bits = lax.bitcast_convert_type(sc, I32)
        key = bits ^ (lax.shift_right_arithmetic(bits, 31) & np.int32(0x7FFFFFFF))
        causal = (j * TILE + krow) <= (i * TILE + qcol)
        key = jnp.where(causal, key, INT_MIN)
        keys_ref[j] = key
        hi_ref[j] = lax.shift_right_arithmetic(key, 16).astype(I16)
        lo_ref[j] = ((key & 0xFFFF) + I16_MIN).astype(I16)
        return carry

    lax.fori_loop(0, nkt, score_tile, 0)

    @pl.when(nkt % 2 == 1)
    def _():
        hi_ref[nkt] = jnp.full((TILE, TILE), I16_MIN, I16)
        lo_ref[nkt] = jnp.full((TILE, TILE), I16_MIN, I16)

    def count16(ref, pred):
        def body(jj, acc):
            part = None
            for c in range(2):
                hits = jnp.where(pred(ref[2 * jj + c]), jnp.ones((), BF16), jnp.zeros((), BF16))
                hits = hits.reshape(TILE // PACKED_ROWS, PACKED_ROWS, TILE)
                for r in range(TILE // PACKED_ROWS):
                    part = hits[r] if part is None else part + hits[r]
            return acc + part.astype(F32)
        acc = lax.fori_loop(0, (nkt + 1) // 2, body, jnp.zeros((PACKED_ROWS, TILE), F32))
        return jnp.sum(acc, axis=0, keepdims=True)

    def kth16(ref, k_q):
        zero = jnp.zeros((1, TILE), I32)
        t0 = jnp.where(count16(ref, lambda x: x >= zero.astype(I16)) >= k_q, zero, zero + I16_MIN)

        def bisect(it, t):
            cand = t | lax.shift_left(np.int32(1), 14 - it)
            return jnp.where(count16(ref, lambda x: x >= cand.astype(I16)) >= k_q, cand, t)

        return lax.fori_loop(0, 15, bisect, t0)

    t_q = i * TILE + lax.broadcasted_iota(I32, (1, TILE), 1)
    k_q = jnp.minimum(topk, t_q + 1).astype(F32)
    thr_hi = kth16(hi_ref, k_q)
    thr_hi16 = thr_hi.astype(I16)
    k_lo = k_q - count16(hi_ref, lambda x: x > thr_hi16)

    def park(j, carry):
        lo_ref[j] = jnp.where(hi_ref[j] == thr_hi16, lo_ref[j], jnp.full((), I16_MIN, I16))
        return carry

    lax.fori_loop(0, nkt, park, 0)
    thr_lo = kth16(lo_ref, k_lo)
    thr_lo16 = thr_lo.astype(I16)
    thr = lax.shift_left(thr_hi, 16) | ((thr_lo - I16_MIN) & 0xFFFF)

    need = k_lo - count16(lo_ref, lambda x: x > thr_lo16)
    n_eq = count16(lo_ref, lambda x: x == thr_lo16)
    has_tie = jnp.max(jnp.where(n_eq > need, 1.0, 0.0))

    @pl.when(has_tie > 0.0)
    def _():
        tril = jnp.where(qcol <= krow, 1.0, 0.0).astype(BF16)

        def fix(j, seen):
            kt = keys_ref[j]
            eq = kt == thr
            eqf = jnp.where(eq, 1.0, 0.0)
            pref = jnp.dot(tril, eqf.astype(BF16), preferred_element_type=F32) + seen
            keys_ref[j] = jnp.where(eq & (pref > need), INT_MIN, kt)
            return seen + jnp.sum(eqf, axis=0, keepdims=True)

        lax.fori_loop(0, nkt, fix, jnp.zeros((1, TILE), F32))

    for h0 in range(0, N_HEADS, HEADS_PER_STEP):
        heads = range(h0, h0 + HEADS_PER_STEP)

        def step(states, j, ntiles, heads=heads):
            ks = _key_tile(j, ntiles)
            bias = tuple(jnp.where(keys_ref[j + c] >= thr, 0.0, NEG) for c in range(ntiles))
            return _flash_step(states, [(k_ref[0, ks, _head_cols(h)], qh_ref[h]) for h in heads],
                               [bias] * len(heads),
                               [_values_t(vt_ref, j, ntiles, slice(h * HEAD_DIM, (h + 1) * HEAD_DIM))
                                for h in heads])

        states = _flash_over_tiles(tuple(_flash_init(HEAD_DIM) for _ in heads), 0, nkt, step)
        _flash_finish(o_ref, states, h0 * HEAD_DIM)


def _dsa(q, k, vt, qi, wit, ki):
    bsz, s, _ = q.shape
    topk = min(TOPK_MAX, s // 4)
    nt = s // TILE
    qtile = lambda w: pl.BlockSpec((1, TILE, w), lambda b, i: (b, i, 0))
    full = lambda w: pl.BlockSpec((1, s, w), lambda b, i: (b, 0, 0))
    return pl.pallas_call(
        functools.partial(_dsa_kernel, topk),
        grid=(bsz, nt),
        in_specs=[qtile(MIX_WIDTH), full(MIX_WIDTH),
                  pl.BlockSpec((nt, MIX_WIDTH, TILE), lambda b, i: (b, 0, 0)),
                  qtile(IDX_HEADS * IDX_DIM),
                  pl.BlockSpec((IDX_HEADS, TILE), lambda b, i: (0, b * nt + i)),
                  full(LANES)],
        out_specs=qtile(MIX_WIDTH),
        out_shape=jax.ShapeDtypeStruct((bsz, s, MIX_WIDTH), BF16),
        scratch_shapes=[pltpu.VMEM((nt, TILE, TILE), I32),
                        pltpu.VMEM((nt + nt % 2, TILE, TILE), I16),
                        pltpu.VMEM((nt + nt % 2, TILE, TILE), I16),
                        pltpu.VMEM((N_HEADS, TILE, LANES), BF16),
                        pltpu.VMEM((IDX_HEADS, TILE, LANES), BF16)],
        compiler_params=_cparams(2), name="dsa",
    )(q, k, vt, qi, wit, ki)


def _dilated_kernel(q_ref, k_ref, vt_ref, o_ref, qh_ref):
    i = pl.program_id(1)
    rel = (lax.broadcasted_iota(I32, (TILE, TILE), 1)
           - lax.broadcasted_iota(I32, (TILE, TILE), 0))
    _store_heads(q_ref, qh_ref)
    states = tuple(_flash_init(B_V_DIM) for _ in range(B_GROUP_HEADS))
    for g, (window, dilation) in enumerate(DIL_PAIRS):
        heads = [g * B_GROUP_HEADS + hv for hv in range(B_GROUP_HEADS)]
        nback = -(-window // TILE)

        def step(states, j, ntiles, heads=heads, window=window, dilation=dilation):
            ks = _key_tile(j, ntiles)
            bias = []
            for c in range(ntiles):
                dist = rel + (i - j - c) * TILE
                ok = (dist >= 0) & (dist <= window) & ((dist & (dilation - 1)) == 0)
                bias.append(jnp.where(ok, 0.0, NEG))
            return _flash_step(states, [(k_ref[0, ks, _head_cols(h)], qh_ref[h]) for h in heads],
                               [tuple(bias)] * len(heads),
                               [_values_t(vt_ref, j, ntiles, slice(hv * B_V_DIM, (hv + 1) * B_V_DIM))
                                for hv in range(B_GROUP_HEADS)])

        states = _flash_over_tiles(states, jnp.maximum(i - nback, 0), i + 1, step)
    _flash_finish(o_ref, states, 0)


def _dilated(q, k, vt):
    bsz, s, w = q.shape
    nt = s // TILE
    return pl.pallas_call(
        _dilated_kernel,
        grid=(bsz, nt),
        in_specs=[pl.BlockSpec((1, TILE, w), lambda b, i: (b, i, 0)),
                  pl.BlockSpec((1, s, w), lambda b, i: (b, 0, 0)),
                  pl.BlockSpec((nt, w, TILE), lambda b, i: (b, 0, 0))],
        out_specs=pl.BlockSpec((1, TILE, w), lambda b, i: (b, i, 0)),
        out_shape=jax.ShapeDtypeStruct((bsz, s, w), BF16),
        scratch_shapes=[pltpu.VMEM((N_HEADS, TILE, LANES), BF16)],
        compiler_params=_cparams(2), name="dilated",
    )(q, k, vt)


def _kmean_kernel(nblk, k_ref, o_ref):
    o_ref[0] = jnp.zeros(o_ref.shape[1:], o_ref.dtype)
    for n in range(nblk):
        kb = k_ref[0, n * MOBA_BLOCK:(n + 1) * MOBA_BLOCK, :].astype(F32)
        o_ref[0, n:n + 1, :] = jnp.mean(kb, axis=0, keepdims=True).astype(o_ref.dtype)


def _kmean(k):
    bsz, s, w = k.shape
    return pl.pallas_call(
        functools.partial(_kmean_kernel, s // MOBA_BLOCK),
        grid=(bsz,),
        in_specs=[pl.BlockSpec((1, s, w), lambda b: (b, 0, 0))],
        out_specs=pl.BlockSpec((1, LANES, w), lambda b: (b, 0, 0)),
        out_shape=jax.ShapeDtypeStruct((bsz, LANES, w), BF16),
        compiler_params=_cparams(1), name="kmean",
    )(k)


def _moba_kernel(nbp, q_ref, k_ref, vt_ref, km_ref, o_ref, sel_ref, qh_ref):
    i = pl.program_id(1)
    blk = lax.broadcasted_iota(I32, (nbp, TILE), 0)
    blk_f = blk.astype(F32)
    krow = lax.broadcasted_iota(I32, (TILE, TILE), 0)
    qcol = lax.broadcasted_iota(I32, (TILE, TILE), 1)
    causal_bias = jnp.where(krow <= qcol, 0.0, NEG)
    _store_heads(q_ref, qh_ref)

    for h in range(N_HEADS):
        g = jnp.where(blk < i, _dot_nt(km_ref[0, :, _head_cols(h)], qh_ref[h])[:nbp], -jnp.inf)
        sel = jnp.zeros((nbp, TILE), F32)
        for r in range(MOBA_TOPK):
            mx = jnp.max(g, axis=0, keepdims=True)
            idx = jnp.min(jnp.where(g == mx, blk_f, float(nbp)), axis=0, keepdims=True)
            hit = blk_f == idx
            sel = jnp.where(hit & (jnp.full((nbp, TILE), r, I32) < i), 1.0, sel)
            g = jnp.where(hit, -jnp.inf, g)
        sel_ref[h] = sel

    for h0 in range(0, N_HEADS, HEADS_PER_STEP):
        heads = range(h0, h0 + HEADS_PER_STEP)

        def step(states, j, ntiles, bias_of, heads=heads):
            ks = _key_tile(j, ntiles)
            return _flash_step(states, [(k_ref[0, ks, _head_cols(h)], qh_ref[h]) for h in heads],
                               [tuple(bias_of(h, j + c) for c in range(ntiles)) for h in heads],
                               [_values_t(vt_ref, j, ntiles, slice(h * HEAD_DIM, (h + 1) * HEAD_DIM))
                                for h in heads])

        def past(states, n, ntiles, step=step):
            picked = lambda h, blk_n: jnp.where(sel_ref[h, pl.ds(blk_n, 1), :] > 0.5, 0.0, NEG)
            return step(states, n, ntiles, picked)

        states = _flash_over_tiles(tuple(_flash_init(HEAD_DIM) for _ in heads), 0, i, past)
        states = step(states, i, 1, lambda h, blk_n: causal_bias)
        _flash_finish(o_ref, states, h0 * HEAD_DIM)


def _moba(q, k, vt, kmean):
    bsz, s, w = q.shape
    nt = s // TILE
    nbp = -(-nt // SUBLANES) * SUBLANES
    return pl.pallas_call(
        functools.partial(_moba_kernel, nbp),
        grid=(bsz, nt),
        in_specs=[pl.BlockSpec((1, TILE, w), lambda b, i: (b, i, 0)),
                  pl.BlockSpec((1, s, w), lambda b, i: (b, 0, 0)),
                  pl.BlockSpec((nt, w, TILE), lambda b, i: (b, 0, 0)),
                  pl.BlockSpec((1, LANES, w), lambda b, i: (b, 0, 0))],
        out_specs=pl.BlockSpec((1, TILE, w), lambda b, i: (b, i, 0)),
        out_shape=jax.ShapeDtypeStruct((bsz, s, w), BF16),
        scratch_shapes=[pltpu.VMEM((N_HEADS, nbp, TILE), F32),
                        pltpu.VMEM((N_HEADS, TILE, LANES), BF16)],
        compiler_params=_cparams(2), name="moba",
    )(q, k, vt, kmean)


def _post_kernel(alpha, mix_ref, qm_ref, mkv_ref, wmix_ref, wmem_ref, x_ref, g_ref, b_ref,
                 xo_ref, xb_ref):
    rows = mix_ref.shape[0]
    lo, hi = _half_masks(rows)
    mo = []
    for p in range(MEM_WIDTH // LANES):
        qp = qm_ref[:, p * LANES:(p + 1) * LANES]
        mk = mkv_ref[0, :, p * LANES:(p + 1) * LANES]
        mv = mkv_ref[0, :, MEM_WIDTH + p * LANES:MEM_WIDTH + (p + 1) * LANES]
        outs = []
        for half in (lo, hi):
            s = _dot_nt(jnp.where(half, qp, jnp.zeros_like(qp)), mk)
            e = jnp.exp(s - jnp.max(s, axis=1, keepdims=True))
            pv = jnp.dot(e.astype(BF16), mv, preferred_element_type=F32)
            outs.append(pv / jnp.sum(e, axis=1, keepdims=True))
        mo.append(jnp.where(lo, outs[0], outs[1]).astype(BF16))
    mixed = jnp.dot(mix_ref[...], wmix_ref[...], preferred_element_type=F32)
    for p, mo_p in enumerate(mo):
        mixed = mixed + jnp.dot(mo_p, wmem_ref[p * LANES:(p + 1) * LANES, :],
                                preferred_element_type=F32)
    y = _layer_norm(alpha * x_ref[...] + mixed, g_ref[...], b_ref[...])
    xo_ref[...] = y
    xb_ref[...] = y.astype(BF16)


def _post(alpha, mix, qm, mkv, wmix, wmem, xf, g, b, seq, tm=512):
    t, d = xf.shape
    wm = mix.shape[1]
    per_batch = seq // tm
    row = lambda i: (i, 0)
    fixed = lambda i: (0, 0)
    return pl.pallas_call(
        functools.partial(_post_kernel, alpha),
        grid=(t // tm,),
        in_specs=[pl.BlockSpec((tm, wm), row),
                  pl.BlockSpec((tm, MEM_WIDTH), row),
                  pl.BlockSpec((1,) + mkv.shape[1:], lambda i: (i // per_batch, 0, 0)),
                  pl.BlockSpec(wmix.shape, fixed),
                  pl.BlockSpec(wmem.shape, fixed),
                  pl.BlockSpec((tm, d), row),
                  pl.BlockSpec((1, d), fixed),
                  pl.BlockSpec((1, d), fixed)],
        out_specs=[pl.BlockSpec((tm, d), row), pl.BlockSpec((tm, d), row)],
        out_shape=[jax.ShapeDtypeStruct((t, d), F32), jax.ShapeDtypeStruct((t, d), BF16)],
        compiler_params=_cparams(1), name="post",
    )(mix, qm, mkv, wmix, wmem, xf, g.reshape(1, d), b.reshape(1, d))


def _ffn_kernel(alpha, xb_ref, wg_ref, wu_ref, wd_ref, x_ref, g_ref, b_ref, xo_ref, xbo_ref, acc_ref):
    j = pl.program_id(1)

    @pl.when(j == 0)
    def _():
        acc_ref[...] = jnp.zeros_like(acc_ref)

    xb = xb_ref[...]
    gate = jnp.dot(xb, wg_ref[...], preferred_element_type=F32)
    up = jnp.dot(xb, wu_ref[...], preferred_element_type=F32)
    h = (gate * jax.nn.sigmoid(gate) * up).astype(BF16)
    acc_ref[...] += jnp.dot(h, wd_ref[...], preferred_element_type=F32)

    @pl.when(j == pl.num_programs(1) - 1)
    def _():
        y = _layer_norm(alpha * x_ref[...] + acc_ref[...], g_ref[...], b_ref[...])
        xo_ref[...] = y
        xbo_ref[...] = y.astype(BF16)


def _ffn(alpha, xb, xf, wgu, wd, g, b, tm=1024, tf=256):
    t, d = xf.shape
    dff = wd.shape[0]
    nff = dff // tf
    row = lambda i, j: (i, 0)
    fixed = lambda i, j: (0, 0)
    return pl.pallas_call(
        functools.partial(_ffn_kernel, alpha),
        grid=(t // tm, nff),
        in_specs=[pl.BlockSpec((tm, d), row),
                  pl.BlockSpec((d, tf), lambda i, j: (0, j)),
                  pl.BlockSpec((d, tf), lambda i, j: (0, nff + j)),
                  pl.BlockSpec((tf, d), lambda i, j: (j, 0)),
                  pl.BlockSpec((tm, d), row),
                  pl.BlockSpec((1, d), fixed),
                  pl.BlockSpec((1, d), fixed)],
        out_specs=[pl.BlockSpec((tm, d), row), pl.BlockSpec((tm, d), row)],
        out_shape=[jax.ShapeDtypeStruct((t, d), F32), jax.ShapeDtypeStruct((t, d), BF16)],
        scratch_shapes=[pltpu.VMEM((tm, d), F32)],
        compiler_params=_cparams(2), name="ffn",
    )(xb, wgu, wgu, wd, xf, g.reshape(1, d), b.reshape(1, d))


def _rope_tables(positions):
    inv = ROPE_THETA ** (-jnp.arange(ROPE_HALF, dtype=F32) / ROPE_HALF)
    ang = positions.astype(F32)[..., None] * inv
    cos, sin = jnp.cos(ang), jnp.sin(ang)
    rest = HEAD_DIM - ROPE_DIM
    pad = lambda a, before, after, val: jnp.pad(a, ((0, 0), (0, 0), (before, after)), constant_values=val)
    c = pad(jnp.concatenate([cos, cos], -1), 0, rest, 1.0)
    s1 = pad(sin, ROPE_HALF, rest, 0.0)
    s2 = pad(-sin, 0, ROPE_HALF + rest, 0.0)
    t = positions.shape[0] * positions.shape[1]
    return tuple(jnp.tile(a, (1, 1, LANES // HEAD_DIM)).reshape(t, LANES) for a in (c, s1, s2))


def kernel(x, mem, positions, mem_ln_g, mem_ln_b, w_in_a, idx_kn_g, idx_kn_b, w_in_b, w_in_c,
           w_mem_kv, w_out, ln1_g, ln1_b, w_gate_up, w_down, ln2_g, ln2_b):
    bsz, seq, d = x.shape
    t = bsz * seq
    depth = w_out.shape[0]
    alpha = (2 * depth) ** 0.25
    assert seq % TILE == 0 and TILE == MOBA_BLOCK

    tables = _rope_tables(positions)
    mkv_all = _memkv(mem, mem_ln_g, mem_ln_b, w_mem_kv.astype(BF16))
    xf = x.reshape(t, d)
    xb = xf.astype(BF16)
    dummy_ln = jnp.zeros((1, LANES), F32)
    m2, m3 = 2 * MIX_WIDTH, 3 * MIX_WIDTH
    b3 = lambda a: a.reshape(bsz, seq, a.shape[-1])
    qk_sections = [(0, MIX_WIDTH, "rope", Q_SCALE), (MIX_WIDTH, MIX_WIDTH, "rope", 1.0)]

    for i in range(depth):
        kind, j = i % 3, i // 3
        wo = w_out[i]
        if kind == 0:
            w = w_in_a[j]
            c_wi = m3 + IDX_HEADS * IDX_DIM
            c_ki, c_qm = c_wi + IDX_HEADS, c_wi + IDX_HEADS + IDX_DIM
            w_ki = w[:, c_ki:c_qm]
            w_new = jnp.concatenate([w[:, :m2], w[:, m3:c_wi], w[:, c_qm:], w_ki, w_ki], axis=1).astype(BF16)
            wt_new = jnp.concatenate([w[:, m2:m3], w[:, c_wi:c_ki]], axis=1).T.astype(BF16)
            o_qm = m2 + IDX_HEADS * IDX_DIM
            sections = qk_sections + [(m2, IDX_HEADS * IDX_DIM, "rope", 1.0),
                                      (o_qm, MEM_WIDTH, "plain", SCALE),
                                      (o_qm + MEM_WIDTH, LANES, "ln_rope", 1.0)]
            tsections = [(0, MIX_WIDTH, "tiles", 1.0),
                         (MIX_WIDTH, IDX_HEADS, "flat", float((IDX_HEADS * IDX_DIM) ** -0.5))]
            lng = jnp.tile(idx_kn_g[j], 2).reshape(1, LANES)
            lnb = jnp.tile(idx_kn_b[j], 2).reshape(1, LANES)
            q, k, qi, qm, ki, vt, wit = _inproj(xb, w_new, wt_new, tables, lng, lnb,
                                                sections, [BF16] * 5, tsections, [BF16, F32])
            mix = _dsa(b3(q), b3(k), vt, b3(qi), wit, b3(ki))
        else:
            w = w_in_b[j] if kind == 1 else w_in_c[j]
            w_new = jnp.concatenate([w[:, :m2], w[:, m3:]], axis=1).astype(BF16)
            wt_new = w[:, m2:m3].T.astype(BF16)
            sections = qk_sections + [(m2, MEM_WIDTH, "plain", SCALE)]
            q, k, qm, vt = _inproj(xb, w_new, wt_new, tables, dummy_ln, dummy_ln,
                                   sections, [BF16] * 3, [(0, MIX_WIDTH, "tiles", 1.0)], [BF16])
            if kind == 1:
                mix = _dilated(b3(q), b3(k), vt)
            else:
                k3 = b3(k)
                mix = _moba(b3(q), k3, vt, _kmean(k3))
        xf, xb = _post(alpha, mix.reshape(t, MIX_WIDTH), qm, mkv_all[i], wo[:MIX_WIDTH].astype(BF16),
                       wo[MIX_WIDTH:].astype(BF16), xf, ln1_g[i], ln1_b[i], seq)
        xf, xb = _ffn(alpha, xb, xf, w_gate_up[i].astype(BF16), w_down[i].astype(BF16),
                      ln2_g[i], ln2_b[i])
    return xf.reshape(bsz, seq, d)
```

```python
import functools

import jax
import jax.numpy as jnp
import numpy as np
from jax import lax
from jax.experimental import pallas as pl
from jax.experimental.pallas import tpu as pltpu

F32 = jnp.float32
BF16 = jnp.bfloat16
I32 = jnp.int32
I16 = jnp.int16

HEAD_DIM = 64
N_HEADS = 12
MIX_WIDTH = N_HEADS * HEAD_DIM
N_MEM_HEADS = 4
MEM_WIDTH = N_MEM_HEADS * HEAD_DIM
ROPE_DIM = HEAD_DIM // 4
ROPE_HALF = ROPE_DIM // 2
ROPE_THETA = 500000.0
IDX_HEADS = 8
IDX_DIM = 64
TOPK_MAX = 256
DIL_PAIRS = ((128, 1), (512, 4), (2048, 16))
B_GROUP_HEADS = 4
B_V_DIM = MIX_WIDTH // B_GROUP_HEADS
MOBA_BLOCK = 256
MOBA_TOPK = 3
LN_EPS = 1e-5
SCALE = HEAD_DIM ** -0.5
LOG2E = 1.4426950408889634
Q_SCALE = SCALE * LOG2E

LANES = 128
SUBLANES = 8
VMEM_LIMIT_BYTES = 56 * 1024 * 1024

TILE = 256
ONES_ROWS = 16
HEADS_PER_STEP = 12
NEG = -1e30
INT_MIN = np.int32(-2 ** 31)
I16_MIN = -2 ** 15
PACKED_ROWS = 2 * SUBLANES


def _cparams(n_axes):
    return pltpu.CompilerParams(dimension_semantics=("arbitrary",) * n_axes,
                                vmem_limit_bytes=VMEM_LIMIT_BYTES)


def _layer_norm(y, g, b):
    mu = jnp.mean(y, axis=-1, keepdims=True)
    yc = y - mu
    var = jnp.mean(yc * yc, axis=-1, keepdims=True)
    return yc * lax.rsqrt(var + LN_EPS) * g + b


def _dot_nt(a, b):
    return lax.dot_general(a, b, (((1,), (1,)), ((), ())), preferred_element_type=F32)


def _half_masks(rows):
    lane = lax.broadcasted_iota(I32, (rows, LANES), 1)
    return lane < HEAD_DIM, lane >= HEAD_DIM


def _store_heads(q_ref, qh_ref):
    lo, hi = _half_masks(q_ref.shape[1])
    for head in range(qh_ref.shape[0]):
        qp = q_ref[0, :, _head_cols(head)]
        qh_ref[head] = jnp.where(hi if head % 2 else lo, qp, jnp.zeros_like(qp))


def _head_cols(head):
    return slice((head // 2) * LANES, (head // 2 + 1) * LANES)


def _key_tile(j, ntiles=1):
    return pl.ds(pl.multiple_of(j * TILE, TILE), ntiles * TILE)


def _flash_over_tiles(states, lo, hi, step):
    npairs = (hi - lo) // 2
    states = lax.fori_loop(0, npairs, lambda jj, st: step(st, lo + 2 * jj, 2), states)
    return lax.cond((hi - lo) % 2 == 1, lambda st: step(st, hi - 1, 1), lambda st: st, states)


def _values_t(vt_ref, j, ntiles, rows):
    return jnp.concatenate([vt_ref[j + c, rows, :] for c in range(ntiles)], axis=1)


def _flash_init(dv):
    return (jnp.full((1, TILE), NEG, F32), jnp.zeros((dv + ONES_ROWS, TILE), F32))


def _flash_step(states, qk, biases, vts):
    n = len(states)
    ones = jnp.ones((ONES_ROWS, vts[0].shape[1]), BF16)
    logits = []
    for h in range(n):
        raw = _dot_nt(*qk[h])
        logits.append(jnp.concatenate([raw[c * TILE:(c + 1) * TILE] + b for c, b in enumerate(biases[h])],
                                      axis=0))
    mids = []
    for (m_old, acc), s in zip(states, logits):
        m_new = jnp.maximum(m_old, jnp.max(s, axis=0, keepdims=True))
        p = jnp.exp2(s - m_new).astype(BF16)
        mids.append((m_new, jnp.exp2(m_old - m_new), p))
    out = []
    for (m_new, alpha, p), (_, acc), vt in zip(mids, states, vts):
        pv = jnp.dot(jnp.concatenate([vt, ones], axis=0), p, preferred_element_type=F32)
        out.append((m_new, alpha * acc + pv))
    return tuple(out)


def _flash_finish(o_ref, states, col0):
    out_t = jnp.concatenate([acc[:-ONES_ROWS] * (1.0 / acc[-1:]) for _, acc in states], axis=0)
    for c in range(out_t.shape[0] // LANES):
        o_ref[0, :, col0 + c * LANES:col0 + (c + 1) * LANES] = (
            out_t[c * LANES:(c + 1) * LANES, :].T.astype(o_ref.dtype))


def _memkv_kernel(mem_ref, g_ref, b_ref, w_ref, o_ref):
    mn = _layer_norm(mem_ref[0], g_ref[...], b_ref[...])
    o_ref[0, 0] = jnp.dot(mn.astype(BF16), w_ref[0], preferred_element_type=F32).astype(BF16)


def _memkv(mem, g, b, w_bf16):
    depth = w_bf16.shape[0]
    bsz, n_mem, d = mem.shape
    wout = w_bf16.shape[2]
    return pl.pallas_call(
        _memkv_kernel,
        grid=(depth, bsz),
        in_specs=[pl.BlockSpec((1, n_mem, d), lambda i, b_: (b_, 0, 0)),
                  pl.BlockSpec((1, d), lambda i, b_: (0, 0)),
                  pl.BlockSpec((1, d), lambda i, b_: (0, 0)),
                  pl.BlockSpec((1, d, wout), lambda i, b_: (i, 0, 0))],
        out_specs=pl.BlockSpec((1, 1, n_mem, wout), lambda i, b_: (i, b_, 0, 0)),
        out_shape=jax.ShapeDtypeStruct((depth, bsz, n_mem, wout), BF16),
        compiler_params=_cparams(2), name="memkv",
    )(mem, g.reshape(1, d), b.reshape(1, d), w_bf16)


def _rope(h, c, s1, s2):
    return h * c + pltpu.roll(h, ROPE_HALF, 1) * s1 + pltpu.roll(h, LANES - ROPE_HALF, 1) * s2


def _inproj_kernel(sections, tsections, x_ref, w_ref, wt_ref, c_ref, s1_ref, s2_ref, lng_ref, lnb_ref,
                   *out_refs):
    x = x_ref[...]
    c, s1, s2 = c_ref[...], s1_ref[...], s2_ref[...]
    for (start, width, kind, scale), o_ref in zip(sections, out_refs):
        h = jnp.dot(x, w_ref[:, start:start + width], preferred_element_type=F32)
        if kind == "ln_rope":
            h = _layer_norm(h, lng_ref[...], lnb_ref[...])
        for ch in range(width // LANES):
            hc = h[:, ch * LANES:(ch + 1) * LANES]
            if kind in ("rope", "ln_rope"):
                hc = _rope(hc, c, s1, s2)
            if scale != 1.0:
                hc = hc * scale
            o_ref[:, ch * LANES:(ch + 1) * LANES] = hc.astype(o_ref.dtype)
    for (start, rows, kind, scale), o_ref in zip(tsections, out_refs[len(sections):]):
        ht = _dot_nt(wt_ref[start:start + rows, :], x)
        if scale != 1.0:
            ht = ht * scale
        if kind == "tiles":
            for tl in range(o_ref.shape[0]):
                o_ref[tl] = ht[:, tl * TILE:(tl + 1) * TILE].astype(o_ref.dtype)
        else:
            o_ref[...] = ht.astype(o_ref.dtype)


def _inproj(xb, w_bf16, wt_bf16, tables, lng, lnb, sections, out_dtypes, tsections, tout_dtypes, tm=512):
    t, d = xb.shape
    row = lambda i: (i, 0)
    fixed = lambda i: (0, 0)
    out_shape = [jax.ShapeDtypeStruct((t, sec[1]), dt) for sec, dt in zip(sections, out_dtypes)]
    out_specs = [pl.BlockSpec((tm, sec[1]), row) for sec in sections]
    for (_, rows, kind, _), dt in zip(tsections, tout_dtypes):
        if kind == "tiles":
            out_shape.append(jax.ShapeDtypeStruct((t // TILE, rows, TILE), dt))
            out_specs.append(pl.BlockSpec((tm // TILE, rows, TILE), lambda i: (i, 0, 0)))
        else:
            out_shape.append(jax.ShapeDtypeStruct((rows, t), dt))
            out_specs.append(pl.BlockSpec((rows, tm), lambda i: (0, i)))
    return pl.pallas_call(
        functools.partial(_inproj_kernel, tuple(sections), tuple(tsections)),
        grid=(t // tm,),
        in_specs=[pl.BlockSpec((tm, d), row),
                  pl.BlockSpec(w_bf16.shape, fixed),
                  pl.BlockSpec(wt_bf16.shape, fixed),
                  pl.BlockSpec((tm, LANES), row),
                  pl.BlockSpec((tm, LANES), row),
                  pl.BlockSpec((tm, LANES), row),
                  pl.BlockSpec((1, LANES), fixed),
                  pl.BlockSpec((1, LANES), fixed)],
        out_specs=out_specs,
        out_shape=out_shape,
        compiler_params=_cparams(1), name="inproj",
    )(xb, w_bf16, wt_bf16, *tables, lng, lnb)


def _dsa_kernel(topk, q_ref, k_ref, vt_ref, qi_ref, wit_ref, ki_ref, o_ref,
                keys_ref, hi_ref, lo_ref, qh_ref, qih_ref):
    i = pl.program_id(1)
    nkt = i + 1
    krow = lax.broadcasted_iota(I32, (TILE, TILE), 0)
    qcol = lax.broadcasted_iota(I32, (TILE, TILE), 1)
    _store_heads(q_ref, qh_ref)
    _store_heads(qi_ref, qih_ref)

    wt = wit_ref[...]

    def score_tile(j, carry):
        kt = ki_ref[0, _key_tile(j), :]
        sc = jnp.zeros((TILE, TILE), F32)
        for h in range(IDX_HEADS):
            sc = sc + wt[h:h + 1, :] * jnp.maximum(_dot_nt(kt, qih_ref[h]), 0.0)
        bits = lax.bitcast_convert_type(sc, I32)
        key = bits ^ (lax.shift_right_arithmetic(bits, 31) & np.int32(0x7FFFFFFF))
        causal = (j * TILE + krow) <= (i * TILE + qcol)
        key = jnp.where(causal, key, INT_MIN)
        keys_ref[j] = key
        hi_ref[j] = lax.shift_right_arithmetic(key, 16).astype(I16)
        lo_ref[j] = ((key & 0xFFFF) + I16_MIN).astype(I16)
        return carry

    lax.fori_loop(0, nkt, score_tile, 0)

    @pl.when(nkt % 2 == 1)
    def _():
        hi_ref[nkt] = jnp.full((TILE, TILE), I16_MIN, I16)
        lo_ref[nkt] = jnp.full((TILE, TILE), I16_MIN, I16)

    def count16(ref, pred):
        def body(jj, acc):
            part = None
            for c in range(2):
                hits = jnp.where(pred(ref[2 * jj + c]), jnp.ones((), BF16), jnp.zeros((), BF16))
                hits = hits.reshape(TILE // PACKED_ROWS, PACKED_ROWS, TILE)
                for r in range(TILE // PACKED_ROWS):
                    part = hits[r] if part is None else part + hits[r]
            return acc + part.astype(F32)
        acc = lax.fori_loop(0, (nkt + 1) // 2, body, jnp.zeros((PACKED_ROWS, TILE), F32))
        return jnp.sum(acc, axis=0, keepdims=True)

    def kth16(ref, k_q):
        zero = jnp.zeros((1, TILE), I32)
        t0 = jnp.where(count16(ref, lambda x: x >= zero.astype(I16)) >= k_q, zero, zero + I16_MIN)

        def bisect(it, t):
            cand = t | lax.shift_left(np.int32(1), 14 - it)
            return jnp.where(count16(ref, lambda x: x >= cand.astype(I16)) >= k_q, cand, t)

        return lax.fori_loop(0, 15, bisect, t0)

    t_q = i * TILE + lax.broadcasted_iota(I32, (1, TILE), 1)
    k_q = jnp.minimum(topk, t_q + 1).astype(F32)
    thr_hi = kth16(hi_ref, k_q)
    thr_hi16 = thr_hi.astype(I16)
    k_lo = k_q - count16(hi_ref, lambda x: x > thr_hi16)

    def park(j, carry):
        lo_ref[j] = jnp.where(hi_ref[j] == thr_hi16, lo_ref[j], jnp.full((), I16_MIN, I16))
        return carry

    lax.fori_loop(0, nkt, park, 0)
    thr_lo = kth16(lo_ref, k_lo)
    thr_lo16 = thr_lo.astype(I16)
    thr = lax.shift_left(thr_hi, 16) | ((thr_lo - I16_MIN) & 0xFFFF)

    need = k_lo - count16(lo_ref, lambda x: x > thr_lo16)
    n_eq = count16(lo_ref, lambda x: x == thr_lo16)
    has_tie = jnp.max(jnp.where(n_eq > need, 1.0, 0.0))

    @pl.when(has_tie > 0.0)
    def _():
        tril = jnp.where(qcol <= krow, 1.0, 0.0).astype(BF16)

        def fix(j, seen):
            kt = keys_ref[j]
            eq = kt == thr
            eqf = jnp.where(eq, 1.0, 0.0)
            pref = jnp.dot(tril, eqf.astype(BF16), preferred_element_type=F32) + seen
            keys_ref[j] = jnp.where(eq & (pref > need), INT_MIN, kt)
            return seen + jnp.sum(eqf, axis=0, keepdims=True)

        lax.fori_loop(0, nkt, fix, jnp.zeros((1, TILE), F32))

    for h0 in range(0, N_HEADS, HEADS_PER_STEP):
        heads = range(h0, h0 + HEADS_PER_STEP)

        def step(states, j, ntiles, heads=heads):
            ks = _key_tile(j, ntiles)
            bias = tuple(jnp.where(keys_ref[j + c] >= thr, 0.0, NEG) for c in range(ntiles))
            return _flash_step(states, [(k_ref[0, ks, _head_cols(h)], qh_ref[h]) for h in heads],
                               [bias] * len(heads),
                               [_values_t(vt_ref, j, ntiles, slice(h * HEAD_DIM, (h + 1) * HEAD_DIM))
                                for h in heads])

        states = _flash_over_tiles(tuple(_flash_init(HEAD_DIM) for _ in heads), 0, nkt, step)
        _flash_finish(o_ref, states, h0 * HEAD_DIM)


def _dsa(q, k, vt, qi, wit, ki):
    bsz, s, _ = q.shape
    topk = min(TOPK_MAX, s // 4)
    nt = s // TILE
    qtile = lambda w: pl.BlockSpec((1, TILE, w), lambda b, i: (b, i, 0))
    full = lambda w: pl.BlockSpec((1, s, w), lambda b, i: (b, 0, 0))
    return pl.pallas_call(
        functools.partial(_dsa_kernel, topk),
        grid=(bsz, nt),
        in_specs=[qtile(MIX_WIDTH), full(MIX_WIDTH),
                  pl.BlockSpec((nt, MIX_WIDTH, TILE), lambda b, i: (b, 0, 0)),
                  qtile(IDX_HEADS * IDX_DIM),
                  pl.BlockSpec((IDX_HEADS, TILE), lambda b, i: (0, b * nt + i)),
                  full(LANES)],
        out_specs=qtile(MIX_WIDTH),
        out_shape=jax.ShapeDtypeStruct((bsz, s, MIX_WIDTH), BF16),
        scratch_shapes=[pltpu.VMEM((nt, TILE, TILE), I32),
                        pltpu.VMEM((nt + nt % 2, TILE, TILE), I16),
                        pltpu.VMEM((nt + nt % 2, TILE, TILE), I16),
                        pltpu.VMEM((N_HEADS, TILE, LANES), BF16),
                        pltpu.VMEM((IDX_HEADS, TILE, LANES), BF16)],
        compiler_params=_cparams(2), name="dsa",
    )(q, k, vt, qi, wit, ki)


def _dilated_kernel(q_ref, k_ref, vt_ref, o_ref, qh_ref):
    i = pl.program_id(1)
    rel = (lax.broadcasted_iota(I32, (TILE, TILE), 1)
           - lax.broadcasted_iota(I32, (TILE, TILE), 0))
    _store_heads(q_ref, qh_ref)
    states = tuple(_flash_init(B_V_DIM) for _ in range(B_GROUP_HEADS))
    for g, (window, dilation) in enumerate(DIL_PAIRS):
        heads = [g * B_GROUP_HEADS + hv for hv in range(B_GROUP_HEADS)]
        nback = -(-window // TILE)

        def step(states, j, ntiles, heads=heads, window=window, dilation=dilation):
            ks = _key_tile(j, ntiles)
            bias = []
            for c in range(ntiles):
                dist = rel + (i - j - c) * TILE
                ok = (dist >= 0) & (dist <= window) & ((dist & (dilation - 1)) == 0)
                bias.append(jnp.where(ok, 0.0, NEG))
            return _flash_step(states, [(k_ref[0, ks, _head_cols(h)], qh_ref[h]) for h in heads],
                               [tuple(bias)] * len(heads),
                               [_values_t(vt_ref, j, ntiles, slice(hv * B_V_DIM, (hv + 1) * B_V_DIM))
                                for hv in range(B_GROUP_HEADS)])

        states = _flash_over_tiles(states, jnp.maximum(i - nback, 0), i + 1, step)
    _flash_finish(o_ref, states, 0)


def _dilated(q, k, vt):
    bsz, s, w = q.shape
    nt = s // TILE
    return pl.pallas_call(
        _dilated_kernel,
        grid=(bsz, nt),
        in_specs=[pl.BlockSpec((1, TILE, w), lambda b, i: (b, i, 0)),
                  pl.BlockSpec((1, s, w), lambda b, i: (b, 0, 0)),
                  pl.BlockSpec((nt, w, TILE), lambda b, i: (b, 0, 0))],
        out_specs=pl.BlockSpec((1, TILE, w), lambda b, i: (b, i, 0)),
        out_shape=jax.ShapeDtypeStruct((bsz, s, w), BF16),
        scratch_shapes=[pltpu.VMEM((N_HEADS, TILE, LANES), BF16)],
        compiler_params=_cparams(2), name="dilated",
    )(q, k, vt)


def _kmean_kernel(nblk, k_ref, o_ref):
    o_ref[0] = jnp.zeros(o_ref.shape[1:], o_ref.dtype)
    for n in range(nblk):
        kb = k_ref[0, n * MOBA_BLOCK:(n + 1) * MOBA_BLOCK, :].astype(F32)
        o_ref[0, n:n + 1, :] = jnp.mean(kb, axis=0, keepdims=True).astype(o_ref.dtype)


def _kmean(k):
    bsz, s, w = k.shape
    return pl.pallas_call(
        functools.partial(_kmean_kernel, s // MOBA_BLOCK),
        grid=(bsz,),
        in_specs=[pl.BlockSpec((1, s, w), lambda b: (b, 0, 0))],
        out_specs=pl.BlockSpec((1, LANES, w), lambda b: (b, 0, 0)),
        out_shape=jax.ShapeDtypeStruct((bsz, LANES, w), BF16),
        compiler_params=_cparams(1), name="kmean",
    )(k)


def _moba_kernel(nbp, q_ref, k_ref, vt_ref, km_ref, o_ref, sel_ref, qh_ref):
    i = pl.program_id(1)
    blk = lax.broadcasted_iota(I32, (nbp, TILE), 0)
    blk_f = blk.astype(F32)
    krow = lax.broadcasted_iota(I32, (TILE, TILE), 0)
    qcol = lax.broadcasted_iota(I32, (TILE, TILE), 1)
    causal_bias = jnp.where(krow <= qcol, 0.0, NEG)
    _store_heads(q_ref, qh_ref)

    for h in range(N_HEADS):
        g = jnp.where(blk < i, _dot_nt(km_ref[0, :, _head_cols(h)], qh_ref[h])[:nbp], -jnp.inf)
        sel = jnp.zeros((nbp, TILE), F32)
        for r in range(MOBA_TOPK):
            mx = jnp.max(g, axis=0, keepdims=True)
            idx = jnp.min(jnp.where(g == mx, blk_f, float(nbp)), axis=0, keepdims=True)
            hit = blk_f == idx
            sel = jnp.where(hit & (jnp.full((nbp, TILE), r, I32) < i), 1.0, sel)
            g = jnp.where(hit, -jnp.inf, g)
        sel_ref[h] = sel

    for h0 in range(0, N_HEADS, HEADS_PER_STEP):
        heads = range(h0, h0 + HEADS_PER_STEP)

        def step(states, j, ntiles, bias_of, heads=heads):
            ks = _key_tile(j, ntiles)
            return _flash_step(states, [(k_ref[0, ks, _head_cols(h)], qh_ref[h]) for h in heads],
                               [tuple(bias_of(h, j + c) for c in range(ntiles)) for h in heads],
                               [_values_t(vt_ref, j, ntiles, slice(h * HEAD_DIM, (h + 1) * HEAD_DIM))
                                for h in heads])

        def past(states, n, ntiles, step=step):
            picked = lambda h, blk_n: jnp.where(sel_ref[h, pl.ds(blk_n, 1), :] > 0.5, 0.0, NEG)
            return step(states, n, ntiles, picked)

        states = _flash_over_tiles(tuple(_flash_init(HEAD_DIM) for _ in heads), 0, i, past)
        states = step(states, i, 1, lambda h, blk_n: causal_bias)
        _flash_finish(o_ref, states, h0 * HEAD_DIM)


def _moba(q, k, vt, kmean):
    bsz, s, w = q.shape
    nt = s // TILE
    nbp = -(-nt // SUBLANES) * SUBLANES
    return pl.pallas_call(
        functools.partial(_moba_kernel, nbp),
        grid=(bsz, nt),
        in_specs=[pl.BlockSpec((1, TILE, w), lambda b, i: (b, i, 0)),
                  pl.BlockSpec((1, s, w), lambda b, i: (b, 0, 0)),
                  pl.BlockSpec((nt, w, TILE), lambda b, i: (b, 0, 0)),
                  pl.BlockSpec((1, LANES, w), lambda b, i: (b, 0, 0))],
        out_specs=pl.BlockSpec((1, TILE, w), lambda b, i: (b, i, 0)),
        out_shape=jax.ShapeDtypeStruct((bsz, s, w), BF16),
        scratch_shapes=[pltpu.VMEM((N_HEADS, nbp, TILE), F32),
                        pltpu.VMEM((N_HEADS, TILE, LANES), BF16)],
        compiler_params=_cparams(2), name="moba",
    )(q, k, vt, kmean)


def _tail_kernel(alpha, tf, mix_ref, qm_ref, mkv_ref, wmix_ref, wmem_ref, x_ref, g1_ref, b1_ref,
                 wgu_ref, wd_ref, g2_ref, b2_ref, xo_ref, xb_ref, h_ref):
    rows = mix_ref.shape[0]
    lo, hi = _half_masks(rows)
    mo = []
    for p in range(MEM_WIDTH // LANES):
        qp = qm_ref[:, p * LANES:(p + 1) * LANES]
        mk = mkv_ref[0, :, p * LANES:(p + 1) * LANES]
        mv = mkv_ref[0, :, MEM_WIDTH + p * LANES:MEM_WIDTH + (p + 1) * LANES]
        outs = []
        for half in (lo, hi):
            s = _dot_nt(jnp.where(half, qp, jnp.zeros_like(qp)), mk)
            e = jnp.exp(s - jnp.max(s, axis=1, keepdims=True))
            pv = jnp.dot(e.astype(BF16), mv, preferred_element_type=F32)
            outs.append(pv / jnp.sum(e, axis=1, keepdims=True))
        mo.append(jnp.where(lo, outs[0], outs[1]).astype(BF16))
    mixed = jnp.dot(mix_ref[...], wmix_ref[...], preferred_element_type=F32)
    for p, mo_p in enumerate(mo):
        mixed = mixed + jnp.dot(mo_p, wmem_ref[p * LANES:(p + 1) * LANES, :],
                                preferred_element_type=F32)
    x1 = _layer_norm(alpha * x_ref[...] + mixed, g1_ref[...], b1_ref[...])

    x1b = x1.astype(BF16)
    dff = wd_ref.shape[0]
    for c in range(dff // tf):
        gate = jnp.dot(x1b, wgu_ref[:, c * tf:(c + 1) * tf], preferred_element_type=F32)
        up = jnp.dot(x1b, wgu_ref[:, dff + c * tf:dff + (c + 1) * tf], preferred_element_type=F32)
        h_ref[:, c * tf:(c + 1) * tf] = (gate * jax.nn.sigmoid(gate) * up).astype(BF16)
    y = jnp.dot(h_ref[...], wd_ref[...], preferred_element_type=F32)
    x2 = _layer_norm(alpha * x1 + y, g2_ref[...], b2_ref[...])
    xo_ref[...] = x2
    xb_ref[...] = x2.astype(BF16)


def _tail(alpha, mix, qm, mkv, wmix, wmem, xf, g1, b1, wgu, wd, g2, b2, seq, tm=512, tf=256):
    t, d = xf.shape
    wm = mix.shape[1]
    dff = wd.shape[0]
    per_batch = seq // tm
    row = lambda i: (i, 0)
    fixed = lambda i: (0, 0)
    resident = lambda shape: pl.BlockSpec(shape, fixed, pipeline_mode=pl.Buffered(1))
    vec = lambda a: a.reshape(1, d)
    return pl.pallas_call(
        functools.partial(_tail_kernel, alpha, tf),
        grid=(t // tm,),
        in_specs=[pl.BlockSpec((tm, wm), row),
                  pl.BlockSpec((tm, MEM_WIDTH), row),
                  pl.BlockSpec((1,) + mkv.shape[1:], lambda i: (i // per_batch, 0, 0)),
                  resident(wmix.shape), resident(wmem.shape),
                  pl.BlockSpec((tm, d), row),
                  resident((1, d)), resident((1, d)),
                  resident(wgu.shape), resident(wd.shape),
                  resident((1, d)), resident((1, d))],
        out_specs=[pl.BlockSpec((tm, d), row), pl.BlockSpec((tm, d), row)],
        out_shape=[jax.ShapeDtypeStruct((t, d), F32), jax.ShapeDtypeStruct((t, d), BF16)],
        scratch_shapes=[pltpu.VMEM((tm, dff), BF16)],
        compiler_params=_cparams(1), name="tail",
    )(mix, qm, mkv, wmix, wmem, xf, vec(g1), vec(b1), wgu, wd, vec(g2), vec(b2))


def _rope_tables(positions):
    inv = ROPE_THETA ** (-jnp.arange(ROPE_HALF, dtype=F32) / ROPE_HALF)
    ang = positions.astype(F32)[..., None] * inv
    cos, sin = jnp.cos(ang), jnp.sin(ang)
    rest = HEAD_DIM - ROPE_DIM
    pad = lambda a, before, after, val: jnp.pad(a, ((0, 0), (0, 0), (before, after)), constant_values=val)
    c = pad(jnp.concatenate([cos, cos], -1), 0, rest, 1.0)
    s1 = pad(sin, ROPE_HALF, rest, 0.0)
    s2 = pad(-sin, 0, ROPE_HALF + rest, 0.0)
    t = positions.shape[0] * positions.shape[1]
    return tuple(jnp.tile(a, (1, 1, LANES // HEAD_DIM)).reshape(t, LANES) for a in (c, s1, s2))


def kernel(x, mem, positions, mem_ln_g, mem_ln_b, w_in_a, idx_kn_g, idx_kn_b, w_in_b, w_in_c,
           w_mem_kv, w_out, ln1_g, ln1_b, w_gate_up, w_down, ln2_g, ln2_b):
    bsz, seq, d = x.shape
    t = bsz * seq
    depth = w_out.shape[0]
    alpha = (2 * depth) ** 0.25
    assert seq % TILE == 0 and TILE == MOBA_BLOCK

    tables = _rope_tables(positions)
    mkv_all = _memkv(mem, mem_ln_g, mem_ln_b, w_mem_kv.astype(BF16))
    xf = x.reshape(t, d)
    xb = xf.astype(BF16)
    dummy_ln = jnp.zeros((1, LANES), F32)
    m2, m3 = 2 * MIX_WIDTH, 3 * MIX_WIDTH
    b3 = lambda a: a.reshape(bsz, seq, a.shape[-1])
    qk_sections = [(0, MIX_WIDTH, "rope", Q_SCALE), (MIX_WIDTH, MIX_WIDTH, "rope", 1.0)]

    for i in range(depth):
        kind, j = i % 3, i // 3
        wo = w_out[i]
        if kind == 0:
            w = w_in_a[j]
            c_wi = m3 + IDX_HEADS * IDX_DIM
            c_ki, c_qm = c_wi + IDX_HEADS, c_wi + IDX_HEADS + IDX_DIM
            w_ki = w[:, c_ki:c_qm]
            w_new = jnp.concatenate([w[:, :m2], w[:, m3:c_wi], w[:, c_qm:], w_ki, w_ki], axis=1).astype(BF16)
            wt_new = jnp.concatenate([w[:, m2:m3], w[:, c_wi:c_ki]], axis=1).T.astype(BF16)
            o_qm = m2 + IDX_HEADS * IDX_DIM
            sections = qk_sections + [(m2, IDX_HEADS * IDX_DIM, "rope", 1.0),
                                      (o_qm, MEM_WIDTH, "plain", SCALE),
                                      (o_qm + MEM_WIDTH, LANES, "ln_rope", 1.0)]
            tsections = [(0, MIX_WIDTH, "tiles", 1.0),
                         (MIX_WIDTH, IDX_HEADS, "flat", float((IDX_HEADS * IDX_DIM) ** -0.5))]
            lng = jnp.tile(idx_kn_g[j], 2).reshape(1, LANES)
            lnb = jnp.tile(idx_kn_b[j], 2).reshape(1, LANES)
            q, k, qi, qm, ki, vt, wit = _inproj(xb, w_new, wt_new, tables, lng, lnb,
                                                sections, [BF16] * 5, tsections, [BF16, F32])
            mix = _dsa(b3(q), b3(k), vt, b3(qi), wit, b3(ki))
        else:
            w = w_in_b[j] if kind == 1 else w_in_c[j]
            w_new = jnp.concatenate([w[:, :m2], w[:, m3:]], axis=1).astype(BF16)
            wt_new = w[:, m2:m3].T.astype(BF16)
            sections = qk_sections + [(m2, MEM_WIDTH, "plain", SCALE)]
            q, k, qm, vt = _inproj(xb, w_new, wt_new, tables, dummy_ln, dummy_ln,
                                   sections, [BF16] * 3, [(0, MIX_WIDTH, "tiles", 1.0)], [BF16])
            if kind == 1:
                mix = _dilated(b3(q), b3(k), vt)
            else:
                k3 = b3(k)
                mix = _moba(b3(q), k3, vt, _kmean(k3))
        xf, xb = _tail(alpha, mix.reshape(t, MIX_WIDTH), qm, mkv_all[i], wo[:MIX_WIDTH].astype(BF16),
                       wo[MIX_WIDTH:].astype(BF16), xf, ln1_g[i], ln1_b[i],
                       w_gate_up[i].astype(BF16), w_down[i].astype(BF16), ln2_g[i], ln2_b[i], seq)
    return xf.reshape(bsz, seq, d)
```

```python
import functools

import jax
import jax.numpy as jnp
import numpy as np
from jax import lax
from jax.experimental import pallas as pl
from jax.experimental.pallas import tpu as pltpu

F32 = jnp.float32
BF16 = jnp.bfloat16
I32 = jnp.int32
I16 = jnp.int16

HEAD_DIM = 64
N_HEADS = 12
MIX_WIDTH = N_HEADS * HEAD_DIM
N_MEM_HEADS = 4
MEM_WIDTH = N_MEM_HEADS * HEAD_DIM
ROPE_DIM = HEAD_DIM // 4
ROPE_HALF = ROPE_DIM // 2
ROPE_THETA = 500000.0
IDX_HEADS = 8
IDX_DIM = 64
TOPK_MAX = 256
DIL_PAIRS = ((128, 1), (512, 4), (2048, 16))
B_GROUP_HEADS = 4
B_V_DIM = MIX_WIDTH // B_GROUP_HEADS
MOBA_BLOCK = 256
MOBA_TOPK = 3
LN_EPS = 1e-5
SCALE = HEAD_DIM ** -0.5
LOG2E = 1.4426950408889634
Q_SCALE = SCALE * LOG2E

LANES = 128
SUBLANES = 8
VMEM_LIMIT_BYTES = 56 * 1024 * 1024

TILE = 256
ONES_ROWS = 16
HEADS_PER_STEP = 12
NEG = -1e30
INT_MIN = np.int32(-2 ** 31)
I16_MIN = -2 ** 15
PACKED_ROWS = 2 * SUBLANES


def _cparams(n_axes):
    return pltpu.CompilerParams(dimension_semantics=("arbitrary",) * n_axes,
                                vmem_limit_bytes=VMEM_LIMIT_BYTES)


def _layer_norm(y, g, b):
    mu = jnp.mean(y, axis=-1, keepdims=True)
    yc = y - mu
    var = jnp.mean(yc * yc, axis=-1, keepdims=True)
    return yc * lax.rsqrt(var + LN_EPS) * g + b


def _dot_nt(a, b):
    return lax.dot_general(a, b, (((1,), (1,)), ((), ())), preferred_element_type=F32)


def _half_masks(rows):
    lane = lax.broadcasted_iota(I32, (rows, LANES), 1)
    return lane < HEAD_DIM, lane >= HEAD_DIM


def _store_heads(q_ref, qh_ref):
    lo, hi = _half_masks(q_ref.shape[1])
    for head in range(qh_ref.shape[0]):
        qp = q_ref[0, :, _head_cols(head)]
        qh_ref[head] = jnp.where(hi if head % 2 else lo, qp, jnp.zeros_like(qp))


def _head_cols(head):
    return slice((head // 2) * LANES, (head // 2 + 1) * LANES)


def _key_tile(j, ntiles=1):
    return pl.ds(pl.multiple_of(j * TILE, TILE), ntiles * TILE)


def _flash_over_tiles(states, lo, hi, step):
    npairs = (hi - lo) // 2
    states = lax.fori_loop(0, npairs, lambda jj, st: step(st, lo + 2 * jj, 2), states)
    return lax.cond((hi - lo) % 2 == 1, lambda st: step(st, hi - 1, 1), lambda st: st, states)


def _values_t(vt_ref, j, ntiles, rows):
    return jnp.concatenate([vt_ref[j + c, rows, :] for c in range(ntiles)], axis=1)


def _flash_init(dv):
    return (jnp.full((1, TILE), NEG, F32), jnp.zeros((dv + ONES_ROWS, TILE), F32))


def _flash_step(states, qk, biases, vts):
    n = len(states)
    ones = jnp.ones((ONES_ROWS, vts[0].shape[1]), BF16)
    logits = []
    for h in range(n):
        raw = _dot_nt(*qk[h])
        logits.append(jnp.concatenate([raw[c * TILE:(c + 1) * TILE] + b for c, b in enumerate(biases[h])],
                                      axis=0))
    mids = []
    for (m_old, acc), s in zip(states, logits):
        m_new = jnp.maximum(m_old, jnp.max(s, axis=0, keepdims=True))
        p = jnp.exp2(s - m_new).astype(BF16)
        mids.append((m_new, jnp.exp2(m_old - m_new), p))
    out = []
    for (m_new, alpha, p), (_, acc), vt in zip(mids, states, vts):
        pv = jnp.dot(jnp.concatenate([vt, ones], axis=0), p, preferred_element_type=F32)
        out.append((m_new, alpha * acc + pv))
    return tuple(out)


def _flash_finish(o_ref, states, col0):
    out_t = jnp.concatenate([acc[:-ONES_ROWS] * (1.0 / acc[-1:]) for _, acc in states], axis=0)
    for c in range(out_t.shape[0] // LANES):
        o_ref[0, :, col0 + c * LANES:col0 + (c + 1) * LANES] = (
            out_t[c * LANES:(c + 1) * LANES, :].T.astype(o_ref.dtype))


def _memkv_kernel(mem_ref, g_ref, b_ref, w_ref, o_ref):
    mn = _layer_norm(mem_ref[0], g_ref[...], b_ref[...])
    o_ref[0, 0] = jnp.dot(mn.astype(BF16), w_ref[0], preferred_element_type=F32).astype(BF16)


def _memkv(mem, g, b, w_bf16):
    depth = w_bf16.shape[0]
    bsz, n_mem, d = mem.shape
    wout = w_bf16.shape[2]
    return pl.pallas_call(
        _memkv_kernel,
        grid=(depth, bsz),
        in_specs=[pl.BlockSpec((1, n_mem, d), lambda i, b_: (b_, 0, 0)),
                  pl.BlockSpec((1, d), lambda i, b_: (0, 0)),
                  pl.BlockSpec((1, d), lambda i, b_: (0, 0)),
                  pl.BlockSpec((1, d, wout), lambda i, b_: (i, 0, 0))],
        out_specs=pl.BlockSpec((1, 1, n_mem, wout), lambda i, b_: (i, b_, 0, 0)),
        out_shape=jax.ShapeDtypeStruct((depth, bsz, n_mem, wout), BF16),
        compiler_params=_cparams(2), name="memkv",
    )(mem, g.reshape(1, d), b.reshape(1, d), w_bf16)


def _rope(h, c, s1, s2):
    return h * c + pltpu.roll(h, ROPE_HALF, 1) * s1 + pltpu.roll(h, LANES - ROPE_HALF, 1) * s2


def _inproj_kernel(sections, tsections, x_ref, w_ref, wt_ref, c_ref, s1_ref, s2_ref, lng_ref, lnb_ref,
                   *out_refs):
    x = x_ref[...].astype(BF16)
    c, s1, s2 = c_ref[...], s1_ref[...], s2_ref[...]
    for (start, width, kind, scale), o_ref in zip(sections, out_refs):
        h = jnp.dot(x, w_ref[:, start:start + width], preferred_element_type=F32)
        if kind == "ln_rope":
            h = _layer_norm(h, lng_ref[...], lnb_ref[...])
        for ch in range(width // LANES):
            hc = h[:, ch * LANES:(ch + 1) * LANES]
            if kind in ("rope", "ln_rope"):
                hc = _rope(hc, c, s1, s2)
            if scale != 1.0:
                hc = hc * scale
            o_ref[:, ch * LANES:(ch + 1) * LANES] = hc.astype(o_ref.dtype)
    for (start, rows, kind, scale), o_ref in zip(tsections, out_refs[len(sections):]):
        ht = _dot_nt(wt_ref[start:start + rows, :], x)
        if scale != 1.0:
            ht = ht * scale
        if kind == "tiles":
            for tl in range(o_ref.shape[0]):
                o_ref[tl] = ht[:, tl * TILE:(tl + 1) * TILE].astype(o_ref.dtype)
        else:
            o_ref[...] = ht.astype(o_ref.dtype)


def _inproj(xb, w_bf16, wt_bf16, tables, lng, lnb, sections, out_dtypes, tsections, tout_dtypes, tm=512):
    t, d = xb.shape
    row = lambda i: (i, 0)
    fixed = lambda i: (0, 0)
    out_shape = [jax.ShapeDtypeStruct((t, sec[1]), dt) for sec, dt in zip(sections, out_dtypes)]
    out_specs = [pl.BlockSpec((tm, sec[1]), row) for sec in sections]
    for (_, rows, kind, _), dt in zip(tsections, tout_dtypes):
        if kind == "tiles":
            out_shape.append(jax.ShapeDtypeStruct((t // TILE, rows, TILE), dt))
            out_specs.append(pl.BlockSpec((tm // TILE, rows, TILE), lambda i: (i, 0, 0)))
        else:
            out_shape.append(jax.ShapeDtypeStruct((rows, t), dt))
            out_specs.append(pl.BlockSpec((rows, tm), lambda i: (0, i)))
    return pl.pallas_call(
        functools.partial(_inproj_kernel, tuple(sections), tuple(tsections)),
        grid=(t // tm,),
        in_specs=[pl.BlockSpec((tm, d), row),
                  pl.BlockSpec(w_bf16.shape, fixed),
                  pl.BlockSpec(wt_bf16.shape, fixed),
                  pl.BlockSpec((tm, LANES), row),
                  pl.BlockSpec((tm, LANES), row),
                  pl.BlockSpec((tm, LANES), row),
                  pl.BlockSpec((1, LANES), fixed),
                  pl.BlockSpec((1, LANES), fixed)],
        out_specs=out_specs,
        out_shape=out_shape,
        compiler_params=_cparams(1), name="inproj",
    )(xb, w_bf16, wt_bf16, *tables, lng, lnb)


def _dsa_kernel(topk, q_ref, k_ref, vt_ref, qi_ref, wit_ref, ki_ref, o_ref,
                keys_ref, hi_ref, lo_ref, qh_ref, qih_ref):
    i = pl.program_id(1)
    nkt = i + 1
    krow = lax.broadcasted_iota(I32, (TILE, TILE), 0)
    qcol = lax.broadcasted_iota(I32, (TILE, TILE), 1)
    _store_heads(q_ref, qh_ref)
    _store_heads(qi_ref, qih_ref)

    wt = wit_ref[...]

    last_tile = ki_ref.shape[1] // TILE - 1

    def score_pair(jj, carry):
        tiles = [2 * jj + c for c in range(2)]
        dots = [[_dot_nt(ki_ref[0, _key_tile(jnp.minimum(j, last_tile)), :], qih_ref[h])
                 for h in range(IDX_HEADS)] for j in tiles]
        for j, dj in zip(tiles, dots):
            sc = jnp.zeros((TILE, TILE), F32)
            for h in range(IDX_HEADS):
                sc = sc + wt[h:h + 1, :] * jnp.maximum(dj[h], 0.0)
            bits = lax.bitcast_convert_type(sc, I32)
            key = bits ^ (lax.shift_right_arithmetic(bits, 31) & np.int32(0x7FFFFFFF))
            causal = (j * TILE + krow) <= (i * TILE + qcol)
            key = jnp.where(causal, key, INT_MIN)
            keys_ref[j] = key
            hi_ref[j] = lax.shift_right_arithmetic(key, 16).astype(I16)
            lo_ref[j] = ((key & 0xFFFF) + I16_MIN).astype(I16)
        return carry

    lax.fori_loop(0, (nkt + 1) // 2, score_pair, 0)

    def count16(ref, pred):
        def body(jj, acc):
            part = None
            for c in range(2):
                hits = jnp.where(pred(ref[2 * jj + c]), jnp.ones((), BF16), jnp.zeros((), BF16))
                hits = hits.reshape(TILE // PACKED_ROWS, PACKED_ROWS, TILE)
                for r in range(TILE // PACKED_ROWS):
                    part = hits[r] if part is None else part + hits[r]
            return acc + part.astype(F32)
        acc = lax.fori_loop(0, (nkt + 1) // 2, body, jnp.zeros((PACKED_ROWS, TILE), F32))
        return jnp.sum(acc, axis=0, keepdims=True)

    def kth16(ref, k_q):
        zero = jnp.zeros((1, TILE), I32)
        t0 = jnp.where(count16(ref, lambda x: x >= zero.astype(I16)) >= k_q, zero, zero + I16_MIN)

        def bisect(it, t):
            cand = t | lax.shift_left(np.int32(1), 14 - it)
            return jnp.where(count16(ref, lambda x: x >= cand.astype(I16)) >= k_q, cand, t)

        return lax.fori_loop(0, 15, bisect, t0)

    t_q = i * TILE + lax.broadcasted_iota(I32, (1, TILE), 1)
    k_q = jnp.minimum(topk, t_q + 1).astype(F32)
    thr_hi = kth16(hi_ref, k_q)
    thr_hi16 = thr_hi.astype(I16)
    k_lo = k_q - count16(hi_ref, lambda x: x > thr_hi16)

    def park(j, carry):
        lo_ref[j] = jnp.where(hi_ref[j] == thr_hi16, lo_ref[j], jnp.full((), I16_MIN, I16))
        return carry

    lax.fori_loop(0, nkt, park, 0)
    thr_lo = kth16(lo_ref, k_lo)
    thr_lo16 = thr_lo.astype(I16)
    thr = lax.shift_left(thr_hi, 16) | ((thr_lo - I16_MIN) & 0xFFFF)

    need = k_lo - count16(lo_ref, lambda x: x > thr_lo16)
    n_eq = count16(lo_ref, lambda x: x == thr_lo16)
    has_tie = jnp.max(jnp.where(n_eq > need, 1.0, 0.0))

    @pl.when(has_tie > 0.0)
    def _():
        tril = jnp.where(qcol <= krow, 1.0, 0.0).astype(BF16)

        def fix(j, seen):
            kt = keys_ref[j]
            eq = kt == thr
            eqf = jnp.where(eq, 1.0, 0.0)
            pref = jnp.dot(tril, eqf.astype(BF16), preferred_element_type=F32) + seen
            keys_ref[j] = jnp.where(eq & (pref > need), INT_MIN, kt)
            return seen + jnp.sum(eqf, axis=0, keepdims=True)

        lax.fori_loop(0, nkt, fix, jnp.zeros((1, TILE), F32))

    for h0 in range(0, N_HEADS, HEADS_PER_STEP):
        heads = range(h0, h0 + HEADS_PER_STEP)

        def step(states, j, ntiles, heads=heads):
            ks = _key_tile(j, ntiles)
            bias = tuple(jnp.where(keys_ref[j + c] >= thr, 0.0, NEG) for c in range(ntiles))
            return _flash_step(states, [(k_ref[0, ks, _head_cols(h)], qh_ref[h]) for h in heads],
                               [bias] * len(heads),
                               [_values_t(vt_ref, j, ntiles, slice(h * HEAD_DIM, (h + 1) * HEAD_DIM))
                                for h in heads])

        states = _flash_over_tiles(tuple(_flash_init(HEAD_DIM) for _ in heads), 0, nkt, step)
        _flash_finish(o_ref, states, h0 * HEAD_DIM)


def _dsa(q, k, vt, qi, wit, ki):
    bsz, s, _ = q.shape
    topk = min(TOPK_MAX, s // 4)
    nt = s // TILE
    qtile = lambda w: pl.BlockSpec((1, TILE, w), lambda b, i: (b, i, 0))
    full = lambda w: pl.BlockSpec((1, s, w), lambda b, i: (b, 0, 0))
    return pl.pallas_call(
        functools.partial(_dsa_kernel, topk),
        grid=(bsz, nt),
        in_specs=[qtile(MIX_WIDTH), full(MIX_WIDTH),
                  pl.BlockSpec((nt, MIX_WIDTH, TILE), lambda b, i: (b, 0, 0)),
                  qtile(IDX_HEADS * IDX_DIM),
                  pl.BlockSpec((IDX_HEADS, TILE), lambda b, i: (0, b * nt + i)),
                  full(LANES)],
        out_specs=qtile(MIX_WIDTH),
        out_shape=jax.ShapeDtypeStruct((bsz, s, MIX_WIDTH), BF16),
        scratch_shapes=[pltpu.VMEM((nt + nt % 2, TILE, TILE), I32),
                        pltpu.VMEM((nt + nt % 2, TILE, TILE), I16),
                        pltpu.VMEM((nt + nt % 2, TILE, TILE), I16),
                        pltpu.VMEM((N_HEADS, TILE, LANES), BF16),
                        pltpu.VMEM((IDX_HEADS, TILE, LANES), BF16)],
        compiler_params=_cparams(2), name="dsa",
    )(q, k, vt, qi, wit, ki)


def _dilated_kernel(q_ref, k_ref, vt_ref, o_ref, qh_ref):
    i = pl.program_id(1)
    rel = (lax.broadcasted_iota(I32, (TILE, TILE), 1)
           - lax.broadcasted_iota(I32, (TILE, TILE), 0))
    _store_heads(q_ref, qh_ref)
    states = tuple(_flash_init(B_V_DIM) for _ in range(B_GROUP_HEADS))
    for g, (window, dilation) in enumerate(DIL_PAIRS):
        heads = [g * B_GROUP_HEADS + hv for hv in range(B_GROUP_HEADS)]
        nback = -(-window // TILE)

        def step(states, j, ntiles, heads=heads, window=window, dilation=dilation):
            ks = _key_tile(j, ntiles)
            bias = []
            for c in range(ntiles):
                dist = rel + (i - j - c) * TILE
                ok = (dist >= 0) & (dist <= window) & ((dist & (dilation - 1)) == 0)
                bias.append(jnp.where(ok, 0.0, NEG))
            return _flash_step(states, [(k_ref[0, ks, _head_cols(h)], qh_ref[h]) for h in heads],
                               [tuple(bias)] * len(heads),
                               [_values_t(vt_ref, j, ntiles, slice(hv * B_V_DIM, (hv + 1) * B_V_DIM))
                                for hv in range(B_GROUP_HEADS)])

        states = _flash_over_tiles(states, jnp.maximum(i - nback, 0), i + 1, step)
    _flash_finish(o_ref, states, 0)


def _dilated(q, k, vt):
    bsz, s, w = q.shape
    nt = s // TILE
    return pl.pallas_call(
        _dilated_kernel,
        grid=(bsz, nt),
        in_specs=[pl.BlockSpec((1, TILE, w), lambda b, i: (b, i, 0)),
                  pl.BlockSpec((1, s, w), lambda b, i: (b, 0, 0)),
                  pl.BlockSpec((nt, w, TILE), lambda b, i: (b, 0, 0))],
        out_specs=pl.BlockSpec((1, TILE, w), lambda b, i: (b, i, 0)),
        out_shape=jax.ShapeDtypeStruct((bsz, s, w), BF16),
        scratch_shapes=[pltpu.VMEM((N_HEADS, TILE, LANES), BF16)],
        compiler_params=_cparams(2), name="dilated",
    )(q, k, vt)


def _kmean_kernel(nblk, k_ref, o_ref):
    o_ref[0] = jnp.zeros(o_ref.shape[1:], o_ref.dtype)
    for n in range(nblk):
        kb = k_ref[0, n * MOBA_BLOCK:(n + 1) * MOBA_BLOCK, :].astype(F32)
        o_ref[0, n:n + 1, :] = jnp.mean(kb, axis=0, keepdims=True).astype(o_ref.dtype)


def _kmean(k):
    bsz, s, w = k.shape
    return pl.pallas_call(
        functools.partial(_kmean_kernel, s // MOBA_BLOCK),
        grid=(bsz,),
        in_specs=[pl.BlockSpec((1, s, w), lambda b: (b, 0, 0))],
        out_specs=pl.BlockSpec((1, LANES, w), lambda b: (b, 0, 0)),
        out_shape=jax.ShapeDtypeStruct((bsz, LANES, w), BF16),
        compiler_params=_cparams(1), name="kmean",
    )(k)


def _moba_kernel(nbp, q_ref, k_ref, vt_ref, km_ref, o_ref, sel_ref, qh_ref):
    i = pl.program_id(1)
    blk = lax.broadcasted_iota(I32, (nbp, TILE), 0)
    blk_f = blk.astype(F32)
    krow = lax.broadcasted_iota(I32, (TILE, TILE), 0)
    qcol = lax.broadcasted_iota(I32, (TILE, TILE), 1)
    causal_bias = jnp.where(krow <= qcol, 0.0, NEG)
    _store_heads(q_ref, qh_ref)

    for h in range(N_HEADS):
        g = jnp.where(blk < i, _dot_nt(km_ref[0, :, _head_cols(h)], qh_ref[h])[:nbp], -jnp.inf)
        sel = jnp.zeros((nbp, TILE), F32)
        for r in range(MOBA_TOPK):
            mx = jnp.max(g, axis=0, keepdims=True)
            idx = jnp.min(jnp.where(g == mx, blk_f, float(nbp)), axis=0, keepdims=True)
            hit = blk_f == idx
            sel = jnp.where(hit & (jnp.full((nbp, TILE), r, I32) < i), 1.0, sel)
            g = jnp.where(hit, -jnp.inf, g)
        sel_ref[h] = sel

    for h0 in range(0, N_HEADS, HEADS_PER_STEP):
        heads = range(h0, h0 + HEADS_PER_STEP)

        def step(states, j, ntiles, bias_of, heads=heads):
            ks = _key_tile(j, ntiles)
            return _flash_step(states, [(k_ref[0, ks, _head_cols(h)], qh_ref[h]) for h in heads],
                               [tuple(bias_of(h, j + c) for c in range(ntiles)) for h in heads],
                               [_values_t(vt_ref, j, ntiles, slice(h * HEAD_DIM, (h + 1) * HEAD_DIM))
                                for h in heads])

        def past(states, n, ntiles, step=step):
            picked = lambda h, blk_n: jnp.where(sel_ref[h, pl.ds(blk_n, 1), :] > 0.5, 0.0, NEG)
            return step(states, n, ntiles, picked)

        states = _flash_over_tiles(tuple(_flash_init(HEAD_DIM) for _ in heads), 0, i, past)
        states = step(states, i, 1, lambda h, blk_n: causal_bias)
        _flash_finish(o_ref, states, h0 * HEAD_DIM)


def _moba(q, k, vt, kmean):
    bsz, s, w = q.shape
    nt = s // TILE
    nbp = -(-nt // SUBLANES) * SUBLANES
    return pl.pallas_call(
        functools.partial(_moba_kernel, nbp),
        grid=(bsz, nt),
        in_specs=[pl.BlockSpec((1, TILE, w), lambda b, i: (b, i, 0)),
                  pl.BlockSpec((1, s, w), lambda b, i: (b, 0, 0)),
                  pl.BlockSpec((nt, w, TILE), lambda b, i: (b, 0, 0)),
                  pl.BlockSpec((1, LANES, w), lambda b, i: (b, 0, 0))],
        out_specs=pl.BlockSpec((1, TILE, w), lambda b, i: (b, i, 0)),
        out_shape=jax.ShapeDtypeStruct((bsz, s, w), BF16),
        scratch_shapes=[pltpu.VMEM((N_HEADS, nbp, TILE), F32),
                        pltpu.VMEM((N_HEADS, TILE, LANES), BF16)],
        compiler_params=_cparams(2), name="moba",
    )(q, k, vt, kmean)


def _tail_kernel(alpha, tf, mix_ref, qm_ref, mkv_ref, wmix_ref, wmem_ref, x_ref, g1_ref, b1_ref,
                 wgu_ref, wd_ref, g2_ref, b2_ref, xo_ref, xb_ref, h_ref):
    rows = mix_ref.shape[0]
    lo, hi = _half_masks(rows)
    mo = []
    for p in range(MEM_WIDTH // LANES):
        qp = qm_ref[:, p * LANES:(p + 1) * LANES]
        mk = mkv_ref[0, :, p * LANES:(p + 1) * LANES]
        mv = mkv_ref[0, :, MEM_WIDTH + p * LANES:MEM_WIDTH + (p + 1) * LANES]
        outs = []
        for half in (lo, hi):
            s = _dot_nt(jnp.where(half, qp, jnp.zeros_like(qp)), mk)
            e = jnp.exp(s - jnp.max(s, axis=1, keepdims=True))
            pv = jnp.dot(e.astype(BF16), mv, preferred_element_type=F32)
            outs.append(pv / jnp.sum(e, axis=1, keepdims=True))
        mo.append(jnp.where(lo, outs[0], outs[1]).astype(BF16))
    mixed = jnp.dot(mix_ref[...], wmix_ref[...], preferred_element_type=F32)
    for p, mo_p in enumerate(mo):
        mixed = mixed + jnp.dot(mo_p, wmem_ref[p * LANES:(p + 1) * LANES, :],
                                preferred_element_type=F32)
    x1 = _layer_norm(alpha * x_ref[...] + mixed, g1_ref[...], b1_ref[...])

    x1b = x1.astype(BF16)
    dff = wd_ref.shape[0]
    for c in range(dff // tf):
        gate = jnp.dot(x1b, wgu_ref[:, c * tf:(c + 1) * tf], preferred_element_type=F32)
        up = jnp.dot(x1b, wgu_ref[:, dff + c * tf:dff + (c + 1) * tf], preferred_element_type=F32)
        h_ref[:, c * tf:(c + 1) * tf] = (gate * jax.nn.sigmoid(gate) * up).astype(BF16)
    y = jnp.dot(h_ref[...], wd_ref[...], preferred_element_type=F32)
    x2 = _layer_norm(alpha * x1 + y, g2_ref[...], b2_ref[...])
    xo_ref[...] = x2
    xb_ref[...] = x2.astype(BF16)


def _tail(alpha, mix, qm, mkv, wmix, wmem, xf, g1, b1, wgu, wd, g2, b2, seq, tm=512, tf=256):
    t, d = xf.shape
    wm = mix.shape[1]
    dff = wd.shape[0]
    per_batch = seq // tm
    row = lambda i: (i, 0)
    fixed = lambda i: (0, 0)
    resident = lambda shape: pl.BlockSpec(shape, fixed, pipeline_mode=pl.Buffered(1))
    vec = lambda a: a.reshape(1, d)
    return pl.pallas_call(
        functools.partial(_tail_kernel, alpha, tf),
        grid=(t // tm,),
        in_specs=[pl.BlockSpec((tm, wm), row),
                  pl.BlockSpec((tm, MEM_WIDTH), row),
                  pl.BlockSpec((1,) + mkv.shape[1:], lambda i: (i // per_batch, 0, 0)),
                  resident(wmix.shape), resident(wmem.shape),
                  pl.BlockSpec((tm, d), row),
                  resident((1, d)), resident((1, d)),
                  resident(wgu.shape), resident(wd.shape),
                  resident((1, d)), resident((1, d))],
        out_specs=[pl.BlockSpec((tm, d), row), pl.BlockSpec((tm, d), row)],
        out_shape=[jax.ShapeDtypeStruct((t, d), F32), jax.ShapeDtypeStruct((t, d), BF16)],
        scratch_shapes=[pltpu.VMEM((tm, dff), BF16)],
        compiler_params=_cparams(1), name="tail",
    )(mix, qm, mkv, wmix, wmem, xf, vec(g1), vec(b1), wgu, wd, vec(g2), vec(b2))


def _rope_tables(positions):
    inv = ROPE_THETA ** (-jnp.arange(ROPE_HALF, dtype=F32) / ROPE_HALF)
    ang = positions.astype(F32)[..., None] * inv
    cos, sin = jnp.cos(ang), jnp.sin(ang)
    rest = HEAD_DIM - ROPE_DIM
    pad = lambda a, before, after, val: jnp.pad(a, ((0, 0), (0, 0), (before, after)), constant_values=val)
    c = pad(jnp.concatenate([cos, cos], -1), 0, rest, 1.0)
    s1 = pad(sin, ROPE_HALF, rest, 0.0)
    s2 = pad(-sin, 0, ROPE_HALF + rest, 0.0)
    t = positions.shape[0] * positions.shape[1]
    return tuple(jnp.tile(a, (1, 1, LANES // HEAD_DIM)).reshape(t, LANES) for a in (c, s1, s2))


def kernel(x, mem, positions, mem_ln_g, mem_ln_b, w_in_a, idx_kn_g, idx_kn_b, w_in_b, w_in_c,
           w_mem_kv, w_out, ln1_g, ln1_b, w_gate_up, w_down, ln2_g, ln2_b):
    bsz, seq, d = x.shape
    t = bsz * seq
    depth = w_out.shape[0]
    alpha = (2 * depth) ** 0.25
    assert seq % TILE == 0 and TILE == MOBA_BLOCK

    tables = _rope_tables(positions)
    mkv_all = _memkv(mem, mem_ln_g, mem_ln_b, w_mem_kv.astype(BF16))
    xf = x.reshape(t, d)
    xb = xf
    dummy_ln = jnp.zeros((1, LANES), F32)
    m2, m3 = 2 * MIX_WIDTH, 3 * MIX_WIDTH
    b3 = lambda a: a.reshape(bsz, seq, a.shape[-1])
    qk_sections = [(0, MIX_WIDTH, "rope", Q_SCALE), (MIX_WIDTH, MIX_WIDTH, "rope", 1.0)]

    for i in range(depth):
        kind, j = i % 3, i // 3
        wo = w_out[i]
        if kind == 0:
            w = w_in_a[j]
            c_wi = m3 + IDX_HEADS * IDX_DIM
            c_ki, c_qm = c_wi + IDX_HEADS, c_wi + IDX_HEADS + IDX_DIM
            w_ki = w[:, c_ki:c_qm]
            w_new = jnp.concatenate([w[:, :m2], w[:, m3:c_wi], w[:, c_qm:], w_ki, w_ki], axis=1).astype(BF16)
            wt_new = jnp.concatenate([w[:, m2:m3], w[:, c_wi:c_ki]], axis=1).T.astype(BF16)
            o_qm = m2 + IDX_HEADS * IDX_DIM
            sections = qk_sections + [(m2, IDX_HEADS * IDX_DIM, "rope", 1.0),
                                      (o_qm, MEM_WIDTH, "plain", SCALE),
                                      (o_qm + MEM_WIDTH, LANES, "ln_rope", 1.0)]
            tsections = [(0, MIX_WIDTH, "tiles", 1.0),
                         (MIX_WIDTH, IDX_HEADS, "flat", float((IDX_HEADS * IDX_DIM) ** -0.5))]
            lng = jnp.tile(idx_kn_g[j], 2).reshape(1, LANES)
            lnb = jnp.tile(idx_kn_b[j], 2).reshape(1, LANES)
            q, k, qi, qm, ki, vt, wit = _inproj(xb, w_new, wt_new, tables, lng, lnb,
                                                sections, [BF16] * 5, tsections, [BF16, F32])
            mix = _dsa(b3(q), b3(k), vt, b3(qi), wit, b3(ki))
        else:
            w = w_in_b[j] if kind == 1 else w_in_c[j]
            w_new = jnp.concatenate([w[:, :m2], w[:, m3:]], axis=1).astype(BF16)
            wt_new = w[:, m2:m3].T.astype(BF16)
            sections = qk_sections + [(m2, MEM_WIDTH, "plain", SCALE)]
            q, k, qm, vt = _inproj(xb, w_new, wt_new, tables, dummy_ln, dummy_ln,
                                   sections, [BF16] * 3, [(0, MIX_WIDTH, "tiles", 1.0)], [BF16])
            if kind == 1:
                mix = _dilated(b3(q), b3(k), vt)
            else:
                k3 = b3(k)
                mix = _moba(b3(q), k3, vt, _kmean(k3))
        xf, xb = _tail(alpha, mix.reshape(t, MIX_WIDTH), qm, mkv_all[i], wo[:MIX_WIDTH].astype(BF16),
                       wo[MIX_WIDTH:].astype(BF16), xf, ln1_g[i], ln1_b[i],
                       w_gate_up[i].astype(BF16), w_down[i].astype(BF16), ln2_g[i], ln2_b[i], seq)
    return xf.reshape(bsz, seq, d)
```

```python
import functools

import jax
import jax.numpy as jnp
import numpy as np
from jax import lax
from jax.experimental import pallas as pl
from jax.experimental.pallas import tpu as pltpu

F32 = jnp.float32
BF16 = jnp.bfloat16
I32 = jnp.int32
I16 = jnp.int16

HEAD_DIM = 64
N_HEADS = 12
MIX_WIDTH = N_HEADS * HEAD_DIM
N_MEM_HEADS = 4
MEM_WIDTH = N_MEM_HEADS * HEAD_DIM
ROPE_DIM = HEAD_DIM // 4
ROPE_HALF = ROPE_DIM // 2
ROPE_THETA = 500000.0
IDX_HEADS = 8
IDX_DIM = 64
TOPK_MAX = 256
DIL_PAIRS = ((128, 1), (512, 4), (2048, 16))
B_GROUP_HEADS = 4
B_V_DIM = MIX_WIDTH // B_GROUP_HEADS
MOBA_BLOCK = 256
MOBA_TOPK = 3
LN_EPS = 1e-5
SCALE = HEAD_DIM ** -0.5
LOG2E = 1.4426950408889634
Q_SCALE = SCALE * LOG2E

LANES = 128
SUBLANES = 8
VMEM_LIMIT_BYTES = 56 * 1024 * 1024

TILE = 256
ONES_ROWS = 16
NEG = -1e30
INT_MIN = np.int32(-2 ** 31)
I16_MIN = -2 ** 15
PACKED_ROWS = 2 * SUBLANES


def _cparams(n_axes):
    return pltpu.CompilerParams(dimension_semantics=("arbitrary",) * n_axes,
                                vmem_limit_bytes=VMEM_LIMIT_BYTES)


def _layer_norm(y, g, b):
    mu = jnp.mean(y, axis=-1, keepdims=True)
    yc = y - mu
    var = jnp.mean(yc * yc, axis=-1, keepdims=True)
    return yc * lax.rsqrt(var + LN_EPS) * g + b


def _dot_nt(a, b):
    return lax.dot_general(a, b, (((1,), (1,)), ((), ())), preferred_element_type=F32)


def _half_masks(rows):
    lane = lax.broadcasted_iota(I32, (rows, LANES), 1)
    return lane < HEAD_DIM, lane >= HEAD_DIM


def _store_heads(q_ref, qh_ref):
    lo, hi = _half_masks(q_ref.shape[1])
    for head in range(qh_ref.shape[0]):
        qp = q_ref[0, :, _head_cols(head)]
        qh_ref[head] = jnp.where(hi if head % 2 else lo, qp, jnp.zeros_like(qp))


def _head_cols(head):
    return slice((head // 2) * LANES, (head // 2 + 1) * LANES)


def _key_tile(j, ntiles=1):
    return pl.ds(pl.multiple_of(j * TILE, TILE), ntiles * TILE)


def _flash_scratch(n_heads, dv):
    logits = lambda: pltpu.VMEM((n_heads, TILE, TILE), F32)
    probs = lambda: pltpu.VMEM((n_heads, TILE, TILE), BF16)
    row = lambda: pltpu.VMEM((n_heads, SUBLANES, TILE), F32)
    return [logits(), logits(), probs(), probs(), row(), pltpu.VMEM((n_heads, dv + ONES_ROWS, TILE), F32)]


def _row_store(ref, h, row):
    ref[h] = jnp.broadcast_to(row, (SUBLANES, TILE))


def _row_load(ref, h):
    return ref[h, 0:1, :]


def _flash_reset(fs):
    m_ref, acc_ref = fs[-2:]
    m_ref[...] = jnp.full(m_ref.shape, NEG, F32)
    acc_ref[...] = jnp.zeros(acc_ref.shape, F32)


def _softmax_stage(src, pdst, ms):
    new_ms, alphas = [], []
    for h, m_old in enumerate(ms):
        s = src[h]
        m_new = jnp.maximum(m_old, jnp.max(s, axis=0, keepdims=True))
        pdst[h] = jnp.exp2(s - m_new).astype(BF16)
        new_ms.append(m_new)
        alphas.append(jnp.exp2(m_old - m_new))
    return tuple(new_ms), tuple(alphas)


def _pv_stage(values, psrc, alphas, acc_ref):
    ones = jnp.ones((ONES_ROWS, TILE), BF16)
    for h, (vt, alpha) in enumerate(zip(values, alphas)):
        pv = jnp.dot(jnp.concatenate([vt, ones], axis=0), psrc[h], preferred_element_type=F32)
        acc_ref[h] = alpha * acc_ref[h] + pv


def _flash_run(fs, lo, count, logits_fn, values_fn):
    s_a, s_b, p_a, p_b, m_ref, acc_ref = fs
    n = s_a.shape[0]
    hi = lo + count
    npairs = (count + 1) // 2

    def qk(dst, j):
        for h, s in enumerate(logits_fn(jnp.clip(j, lo, hi - 1), j < hi)):
            dst[h] = s

    def pv(psrc, alphas, j):
        _pv_stage(values_fn(jnp.clip(j, lo, hi - 1)), psrc, alphas, acc_ref)

    @pl.when(count > 0)
    def _():
        p_b[...] = jnp.zeros(p_b.shape, BF16)
        qk(s_a, lo)

        def trips(jj, carry):
            ms, alphas_b = carry
            j = lo + 2 * jj
            pv(p_b, alphas_b, j - 1)
            qk(s_b, j + 1)
            ms, alphas_a = _softmax_stage(s_a, p_a, ms)
            pv(p_a, alphas_a, j)
            qk(s_a, j + 2)
            ms, alphas_b = _softmax_stage(s_b, p_b, ms)
            return ms, alphas_b

        ms = tuple(_row_load(m_ref, h) for h in range(n))
        ms, alphas_b = lax.fori_loop(0, npairs, trips, (ms, tuple(jnp.ones((1, TILE), F32) for _ in range(n))))
        pv(p_b, alphas_b, lo + 2 * npairs - 1)
        for h, m in enumerate(ms):
            _row_store(m_ref, h, m)


def _flash_single(fs, logits, values):
    s_a, _, p_a, _, m_ref, acc_ref = fs
    for h, s in enumerate(logits):
        s_a[h] = s
    ms, alphas = _softmax_stage(s_a, p_a, tuple(_row_load(m_ref, h) for h in range(len(logits))))
    _pv_stage(values, p_a, alphas, acc_ref)
    for h, m in enumerate(ms):
        _row_store(m_ref, h, m)


def _flash_finish(o_ref, fs, n_heads):
    acc_ref = fs[-1]
    out_t = jnp.concatenate([acc_ref[h, :-ONES_ROWS, :] * (1.0 / acc_ref[h, -1:, :]) for h in range(n_heads)],
                            axis=0)
    for c in range(out_t.shape[0] // LANES):
        o_ref[0, :, c * LANES:(c + 1) * LANES] = out_t[c * LANES:(c + 1) * LANES, :].T.astype(o_ref.dtype)


def _resident(block_shape, index_map):
    return pl.BlockSpec(block_shape, index_map)


def _memkv_kernel(mem_ref, g_ref, b_ref, w_ref, o_ref):
    mn = _layer_norm(mem_ref[0], g_ref[...], b_ref[...])
    o_ref[0, 0] = jnp.dot(mn.astype(BF16), w_ref[0], preferred_element_type=F32).astype(BF16)


def _memkv(mem, g, b, w_bf16):
    depth = w_bf16.shape[0]
    bsz, n_mem, d = mem.shape
    wout = w_bf16.shape[2]
    return pl.pallas_call(
        _memkv_kernel,
        grid=(depth, bsz),
        in_specs=[pl.BlockSpec((1, n_mem, d), lambda i, b_: (b_, 0, 0)),
                  pl.BlockSpec((1, d), lambda i, b_: (0, 0)),
                  pl.BlockSpec((1, d), lambda i, b_: (0, 0)),
                  pl.BlockSpec((1, d, wout), lambda i, b_: (i, 0, 0))],
        out_specs=pl.BlockSpec((1, 1, n_mem, wout), lambda i, b_: (i, b_, 0, 0)),
        out_shape=jax.ShapeDtypeStruct((depth, bsz, n_mem, wout), BF16),
        compiler_params=_cparams(2), name="memkv",
    )(mem, g.reshape(1, d), b.reshape(1, d), w_bf16)


def _rope(h, c, s1, s2):
    return h * c + pltpu.roll(h, ROPE_HALF, 1) * s1 + pltpu.roll(h, LANES - ROPE_HALF, 1) * s2


def _inproj_kernel(sections, tsections, x_ref, w_ref, wt_ref, c_ref, s1_ref, s2_ref, lng_ref, lnb_ref,
                   *out_refs):
    x = x_ref[...].astype(BF16)
    c, s1, s2 = c_ref[...], s1_ref[...], s2_ref[...]
    for (start, width, kind, scale), o_ref in zip(sections, out_refs):
        h = jnp.dot(x, w_ref[:, start:start + width], preferred_element_type=F32)
        if kind == "ln_rope":
            h = _layer_norm(h, lng_ref[...], lnb_ref[...])
        for ch in range(width // LANES):
            hc = h[:, ch * LANES:(ch + 1) * LANES]
            if kind in ("rope", "ln_rope"):
                hc = _rope(hc, c, s1, s2)
            if scale != 1.0:
                hc = hc * scale
            o_ref[:, ch * LANES:(ch + 1) * LANES] = hc.astype(o_ref.dtype)
    for (start, rows, kind, scale), o_ref in zip(tsections, out_refs[len(sections):]):
        ht = _dot_nt(wt_ref[start:start + rows, :], x)
        if scale != 1.0:
            ht = ht * scale
        if kind == "tiles":
            for tl in range(o_ref.shape[0]):
                o_ref[tl] = ht[:, tl * TILE:(tl + 1) * TILE].astype(o_ref.dtype)
        else:
            o_ref[...] = ht.astype(o_ref.dtype)


def _inproj(xb, w_bf16, wt_bf16, tables, lng, lnb, sections, out_dtypes, tsections, tout_dtypes, tm=512):
    t, d = xb.shape
    row = lambda i: (i, 0)
    fixed = lambda i: (0, 0)
    out_shape = [jax.ShapeDtypeStruct((t, sec[1]), dt) for sec, dt in zip(sections, out_dtypes)]
    out_specs = [pl.BlockSpec((tm, sec[1]), row) for sec in sections]
    for (_, rows, kind, _), dt in zip(tsections, tout_dtypes):
        if kind == "tiles":
            out_shape.append(jax.ShapeDtypeStruct((t // TILE, rows, TILE), dt))
            out_specs.append(pl.BlockSpec((tm // TILE, rows, TILE), lambda i: (i, 0, 0)))
        else:
            out_shape.append(jax.ShapeDtypeStruct((rows, t), dt))
            out_specs.append(pl.BlockSpec((rows, tm), lambda i: (0, i)))
    return pl.pallas_call(
        functools.partial(_inproj_kernel, tuple(sections), tuple(tsections)),
        grid=(t // tm,),
        in_specs=[pl.BlockSpec((tm, d), row),
                  pl.BlockSpec(w_bf16.shape, fixed),
                  pl.BlockSpec(wt_bf16.shape, fixed),
                  pl.BlockSpec((tm, LANES), row),
                  pl.BlockSpec((tm, LANES), row),
                  pl.BlockSpec((tm, LANES), row),
                  pl.BlockSpec((1, LANES), fixed),
                  pl.BlockSpec((1, LANES), fixed)],
        out_specs=out_specs,
        out_shape=out_shape,
        compiler_params=_cparams(1), name="inproj",
    )(xb, w_bf16, wt_bf16, *tables, lng, lnb)


def _dsa_kernel(topk, q_ref, k_ref, vt_ref, qi_ref, wit_ref, ki_ref, o_ref,
                keys_ref, hi_ref, lo_ref, qh_ref, qih_ref, *fs):
    i = pl.program_id(1)
    nkt = i + 1
    krow = lax.broadcasted_iota(I32, (TILE, TILE), 0)
    qcol = lax.broadcasted_iota(I32, (TILE, TILE), 1)
    _store_heads(q_ref, qh_ref)
    _store_heads(qi_ref, qih_ref)

    wt = wit_ref[...]

    last_tile = ki_ref.shape[1] // TILE - 1

    def score_pair(jj, carry):
        tiles = [2 * jj + c for c in range(2)]
        dots = [[_dot_nt(ki_ref[0, _key_tile(jnp.minimum(j, last_tile)), :], qih_ref[h])
                 for h in range(IDX_HEADS)] for j in tiles]
        for j, dj in zip(tiles, dots):
            sc = jnp.zeros((TILE, TILE), F32)
            for h in range(IDX_HEADS):
                sc = sc + wt[h:h + 1, :] * jnp.maximum(dj[h], 0.0)
            bits = lax.bitcast_convert_type(sc, I32)
            key = bits ^ (lax.shift_right_arithmetic(bits, 31) & np.int32(0x7FFFFFFF))
            causal = (j * TILE + krow) <= (i * TILE + qcol)
            key = jnp.where(causal, key, INT_MIN)
            keys_ref[j] = key
            hi_ref[j] = lax.shift_right_arithmetic(key, 16).astype(I16)
            lo_ref[j] = ((key & 0xFFFF) + I16_MIN).astype(I16)
        return carry

    lax.fori_loop(0, (nkt + 1) // 2, score_pair, 0)

    def count16(ref, pred):
        def body(jj, acc):
            part = None
            for c in range(2):
                hits = jnp.where(pred(ref[2 * jj + c]), jnp.ones((), BF16), jnp.zeros((), BF16))
                hits = hits.reshape(TILE // PACKED_ROWS, PACKED_ROWS, TILE)
                for r in range(TILE // PACKED_ROWS):
                    part = hits[r] if part is None else part + hits[r]
            return acc + part.astype(F32)
        acc = lax.fori_loop(0, (nkt + 1) // 2, body, jnp.zeros((PACKED_ROWS, TILE), F32))
        return jnp.sum(acc, axis=0, keepdims=True)

    def kth16(ref, k_q):
        zero = jnp.zeros((1, TILE), I32)
        t0 = jnp.where(count16(ref, lambda x: x >= zero.astype(I16)) >= k_q, zero, zero + I16_MIN)

        def bisect(it, t):
            cand = t | lax.shift_left(np.int32(1), 14 - it)
            return jnp.where(count16(ref, lambda x: x >= cand.astype(I16)) >= k_q, cand, t)

        return lax.fori_loop(0, 15, bisect, t0)

    t_q = i * TILE + lax.broadcasted_iota(I32, (1, TILE), 1)
    k_q = jnp.minimum(topk, t_q + 1).astype(F32)
    thr_hi = kth16(hi_ref, k_q)
    thr_hi16 = thr_hi.astype(I16)
    k_lo = k_q - count16(hi_ref, lambda x: x > thr_hi16)

    def park(j, carry):
        lo_ref[j] = jnp.where(hi_ref[j] == thr_hi16, lo_ref[j], jnp.full((), I16_MIN, I16))
        return carry

    lax.fori_loop(0, nkt, park, 0)
    thr_lo = kth16(lo_ref, k_lo)
    thr_lo16 = thr_lo.astype(I16)
    thr = lax.shift_left(thr_hi, 16) | ((thr_lo - I16_MIN) & 0xFFFF)

    need = k_lo - count16(lo_ref, lambda x: x > thr_lo16)
    n_eq = count16(lo_ref, lambda x: x == thr_lo16)
    has_tie = jnp.max(jnp.where(n_eq > need, 1.0, 0.0))

    @pl.when(has_tie > 0.0)
    def _():
        tril = jnp.where(qcol <= krow, 1.0, 0.0).astype(BF16)

        def fix(j, seen):
            kt = keys_ref[j]
            eq = kt == thr
            eqf = jnp.where(eq, 1.0, 0.0)
            pref = jnp.dot(tril, eqf.astype(BF16), preferred_element_type=F32) + seen
            keys_ref[j] = jnp.where(eq & (pref > need), INT_MIN, kt)
            return seen + jnp.sum(eqf, axis=0, keepdims=True)

        lax.fori_loop(0, nkt, fix, jnp.zeros((1, TILE), F32))

    def logits(j, live):
        bias = jnp.where(live, jnp.where(keys_ref[j] >= thr, 0.0, NEG), NEG)
        ks = _key_tile(j)
        return [_dot_nt(k_ref[0, ks, _head_cols(h)], qh_ref[h]) + bias for h in range(N_HEADS)]

    def values(j):
        return [vt_ref[j, h * HEAD_DIM:(h + 1) * HEAD_DIM, :] for h in range(N_HEADS)]

    _flash_reset(fs)
    _flash_run(fs, 0, nkt, logits, values)
    _flash_finish(o_ref, fs, N_HEADS)


def _dsa(q, k, vt, qi, wit, ki):
    bsz, s, _ = q.shape
    topk = min(TOPK_MAX, s // 4)
    nt = s // TILE
    qtile = lambda w: pl.BlockSpec((1, TILE, w), lambda b, i: (b, i, 0))
    full = lambda w: _resident((1, s, w), lambda b, i: (b, 0, 0))
    return pl.pallas_call(
        functools.partial(_dsa_kernel, topk),
        grid=(bsz, nt),
        in_specs=[qtile(MIX_WIDTH), full(MIX_WIDTH),
                  _resident((nt, MIX_WIDTH, TILE), lambda b, i: (b, 0, 0)),
                  qtile(IDX_HEADS * IDX_DIM),
                  pl.BlockSpec((IDX_HEADS, TILE), lambda b, i: (0, b * nt + i)),
                  full(LANES)],
        out_specs=qtile(MIX_WIDTH),
        out_shape=jax.ShapeDtypeStruct((bsz, s, MIX_WIDTH), BF16),
        scratch_shapes=[pltpu.VMEM((nt + nt % 2, TILE, TILE), I32),
                        pltpu.VMEM((nt + nt % 2, TILE, TILE), I16),
                        pltpu.VMEM((nt + nt % 2, TILE, TILE), I16),
                        pltpu.VMEM((N_HEADS, TILE, LANES), BF16),
                        pltpu.VMEM((IDX_HEADS, TILE, LANES), BF16)] + _flash_scratch(N_HEADS, HEAD_DIM),
        compiler_params=_cparams(2), name="dsa",
    )(q, k, vt, qi, wit, ki)


def _dilated_kernel(q_ref, k_ref, vt_ref, o_ref, qh_ref, *fs):
    i = pl.program_id(1)
    rel = (lax.broadcasted_iota(I32, (TILE, TILE), 1)
           - lax.broadcasted_iota(I32, (TILE, TILE), 0))
    _store_heads(q_ref, qh_ref)
    _flash_reset(fs)

    def values(j):
        return [vt_ref[j, hv * B_V_DIM:(hv + 1) * B_V_DIM, :] for hv in range(B_GROUP_HEADS)]

    for g, (window, dilation) in enumerate(DIL_PAIRS):
        heads = [g * B_GROUP_HEADS + hv for hv in range(B_GROUP_HEADS)]
        first = jnp.maximum(i - (-(-window // TILE)), 0)

        def logits(j, live, heads=heads, window=window, dilation=dilation):
            dist = rel + (i - j) * TILE
            ok = (dist >= 0) & (dist <= window) & ((dist & (dilation - 1)) == 0)
            bias = jnp.where(live, jnp.where(ok, 0.0, NEG), NEG)
            ks = _key_tile(j)
            return [_dot_nt(k_ref[0, ks, _head_cols(h)], qh_ref[h]) + bias for h in heads]

        _flash_run(fs, first, i + 1 - first, logits, values)
    _flash_finish(o_ref, fs, B_GROUP_HEADS)


def _dilated(q, k, vt):
    bsz, s, w = q.shape
    nt = s // TILE
    return pl.pallas_call(
        _dilated_kernel,
        grid=(bsz, nt),
        in_specs=[pl.BlockSpec((1, TILE, w), lambda b, i: (b, i, 0)),
                  _resident((1, s, w), lambda b, i: (b, 0, 0)),
                  _resident((nt, w, TILE), lambda b, i: (b, 0, 0))],
        out_specs=pl.BlockSpec((1, TILE, w), lambda b, i: (b, i, 0)),
        out_shape=jax.ShapeDtypeStruct((bsz, s, w), BF16),
        scratch_shapes=[pltpu.VMEM((N_HEADS, TILE, LANES), BF16)] + _flash_scratch(B_GROUP_HEADS, B_V_DIM),
        compiler_params=_cparams(2), name="dilated",
    )(q, k, vt)


def _kmean_kernel(nblk, k_ref, o_ref):
    o_ref[0] = jnp.zeros(o_ref.shape[1:], o_ref.dtype)
    for n in range(nblk):
        kb = k_ref[0, n * MOBA_BLOCK:(n + 1) * MOBA_BLOCK, :].astype(F32)
        o_ref[0, n:n + 1, :] = jnp.mean(kb, axis=0, keepdims=True).astype(o_ref.dtype)


def _kmean(k):
    bsz, s, w = k.shape
    return pl.pallas_call(
        functools.partial(_kmean_kernel, s // MOBA_BLOCK),
        grid=(bsz,),
        in_specs=[pl.BlockSpec((1, s, w), lambda b: (b, 0, 0))],
        out_specs=pl.BlockSpec((1, LANES, w), lambda b: (b, 0, 0)),
        out_shape=jax.ShapeDtypeStruct((bsz, LANES, w), BF16),
        compiler_params=_cparams(1), name="kmean",
    )(k)


def _moba_kernel(nbp, q_ref, k_ref, vt_ref, km_ref, o_ref, sel_ref, qh_ref, *fs):
    i = pl.program_id(1)
    blk = lax.broadcasted_iota(I32, (nbp, TILE), 0)
    blk_f = blk.astype(F32)
    krow = lax.broadcasted_iota(I32, (TILE, TILE), 0)
    qcol = lax.broadcasted_iota(I32, (TILE, TILE), 1)
    causal_bias = jnp.where(krow <= qcol, 0.0, NEG)
    _store_heads(q_ref, qh_ref)

    for h in range(N_HEADS):
        g = jnp.where(blk < i, _dot_nt(km_ref[0, :, _head_cols(h)], qh_ref[h])[:nbp], -jnp.inf)
        sel = jnp.zeros((nbp, TILE), F32)
        for r in range(MOBA_TOPK):
            mx = jnp.max(g, axis=0, keepdims=True)
            idx = jnp.min(jnp.where(g == mx, blk_f, float(nbp)), axis=0, keepdims=True)
            hit = blk_f == idx
            sel = jnp.where(hit & (jnp.full((nbp, TILE), r, I32) < i), 1.0, sel)
            g = jnp.where(hit, -jnp.inf, g)
        sel_ref[h] = sel

    def raw_logits(j):
        ks = _key_tile(j)
        return [_dot_nt(k_ref[0, ks, _head_cols(h)], qh_ref[h]) for h in range(N_HEADS)]

    def past_logits(n, live):
        rows = [jnp.where(live, jnp.where(sel_ref[h, pl.ds(n, 1), :] > 0.5, 0.0, NEG), NEG)
                for h in range(N_HEADS)]
        return [s + row for s, row in zip(raw_logits(n), rows)]

    def values(j):
        return [vt_ref[j, h * HEAD_DIM:(h + 1) * HEAD_DIM, :] for h in range(N_HEADS)]

    _flash_reset(fs)
    _flash_run(fs, 0, i, past_logits, values)
    _flash_single(fs, [s + causal_bias for s in raw_logits(i)], values(i))
    _flash_finish(o_ref, fs, N_HEADS)


def _moba(q, k, vt, kmean):
    bsz, s, w = q.shape
    nt = s // TILE
    nbp = -(-nt // SUBLANES) * SUBLANES
    return pl.pallas_call(
        functools.partial(_moba_kernel, nbp),
        grid=(bsz, nt),
        in_specs=[pl.BlockSpec((1, TILE, w), lambda b, i: (b, i, 0)),
                  _resident((1, s, w), lambda b, i: (b, 0, 0)),
                  _resident((nt, w, TILE), lambda b, i: (b, 0, 0)),
                  _resident((1, LANES, w), lambda b, i: (b, 0, 0))],
        out_specs=pl.BlockSpec((1, TILE, w), lambda b, i: (b, i, 0)),
        out_shape=jax.ShapeDtypeStruct((bsz, s, w), BF16),
        scratch_shapes=[pltpu.VMEM((N_HEADS, nbp, TILE), F32),
                        pltpu.VMEM((N_HEADS, TILE, LANES), BF16)] + _flash_scratch(N_HEADS, HEAD_DIM),
        compiler_params=_cparams(2), name="moba",
    )(q, k, vt, kmean)


def _tail_kernel(alpha, tf, mix_ref, qm_ref, mkv_ref, wmix_ref, wmem_ref, x_ref, g1_ref, b1_ref,
                 wgu_ref, wd_ref, g2_ref, b2_ref, xo_ref, xb_ref, h_ref):
    rows = mix_ref.shape[0]
    lo, hi = _half_masks(rows)
    mo = []
    for p in range(MEM_WIDTH // LANES):
        qp = qm_ref[:, p * LANES:(p + 1) * LANES]
        mk = mkv_ref[0, :, p * LANES:(p + 1) * LANES]
        mv = mkv_ref[0, :, MEM_WIDTH + p * LANES:MEM_WIDTH + (p + 1) * LANES]
        outs = []
        for half in (lo, hi):
            s = _dot_nt(jnp.where(half, qp, jnp.zeros_like(qp)), mk)
            e = jnp.exp(s - jnp.max(s, axis=1, keepdims=True))
            pv = jnp.dot(e.astype(BF16), mv, preferred_element_type=F32)
            outs.append(pv / jnp.sum(e, axis=1, keepdims=True))
        mo.append(jnp.where(lo, outs[0], outs[1]).astype(BF16))
    mixed = jnp.dot(mix_ref[...], wmix_ref[...], preferred_element_type=F32)
    for p, mo_p in enumerate(mo):
        mixed = mixed + jnp.dot(mo_p, wmem_ref[p * LANES:(p + 1) * LANES, :],
                                preferred_element_type=F32)
    x1 = _layer_norm(alpha * x_ref[...] + mixed, g1_ref[...], b1_ref[...])

    x1b = x1.astype(BF16)
    dff = wd_ref.shape[0]
    for c in range(dff // tf):
        gate = jnp.dot(x1b, wgu_ref[:, c * tf:(c + 1) * tf], preferred_element_type=F32)
        up = jnp.dot(x1b, wgu_ref[:, dff + c * tf:dff + (c + 1) * tf], preferred_element_type=F32)
        h_ref[:, c * tf:(c + 1) * tf] = (gate * jax.nn.sigmoid(gate) * up).astype(BF16)
    y = jnp.dot(h_ref[...], wd_ref[...], preferred_element_type=F32)
    x2 = _layer_norm(alpha * x1 + y, g2_ref[...], b2_ref[...])
    xo_ref[...] = x2
    xb_ref[...] = x2.astype(BF16)


def _tail(alpha, mix, qm, mkv, wmix, wmem, xf, g1, b1, wgu, wd, g2, b2, seq, tm=512, tf=256):
    t, d = xf.shape
    wm = mix.shape[1]
    dff = wd.shape[0]
    per_batch = seq // tm
    row = lambda i: (i, 0)
    fixed = lambda i: (0, 0)
    resident = lambda shape: pl.BlockSpec(shape, fixed, pipeline_mode=pl.Buffered(1))
    vec = lambda a: a.reshape(1, d)
    return pl.pallas_call(
        functools.partial(_tail_kernel, alpha, tf),
        grid=(t // tm,),
        in_specs=[pl.BlockSpec((tm, wm), row),
                  pl.BlockSpec((tm, MEM_WIDTH), row),
                  pl.BlockSpec((1,) + mkv.shape[1:], lambda i: (i // per_batch, 0, 0)),
                  resident(wmix.shape), resident(wmem.shape),
                  pl.BlockSpec((tm, d), row),
                  resident((1, d)), resident((1, d)),
                  resident(wgu.shape), resident(wd.shape),
                  resident((1, d)), resident((1, d))],
        out_specs=[pl.BlockSpec((tm, d), row), pl.BlockSpec((tm, d), row)],
        out_shape=[jax.ShapeDtypeStruct((t, d), F32), jax.ShapeDtypeStruct((t, d), BF16)],
        scratch_shapes=[pltpu.VMEM((tm, dff), BF16)],
        compiler_params=_cparams(1), name="tail",
    )(mix, qm, mkv, wmix, wmem, xf, vec(g1), vec(b1), wgu, wd, vec(g2), vec(b2))


def _rope_tables(positions):
    inv = ROPE_THETA ** (-jnp.arange(ROPE_HALF, dtype=F32) / ROPE_HALF)
    ang = positions.astype(F32)[..., None] * inv
    cos, sin = jnp.cos(ang), jnp.sin(ang)
    rest = HEAD_DIM - ROPE_DIM
    pad = lambda a, before, after, val: jnp.pad(a, ((0, 0), (0, 0), (before, after)), constant_values=val)
    c = pad(jnp.concatenate([cos, cos], -1), 0, rest, 1.0)
    s1 = pad(sin, ROPE_HALF, rest, 0.0)
    s2 = pad(-sin, 0, ROPE_HALF + rest, 0.0)
    t = positions.shape[0] * positions.shape[1]
    return tuple(jnp.tile(a, (1, 1, LANES // HEAD_DIM)).reshape(t, LANES) for a in (c, s1, s2))


def kernel(x, mem, positions, mem_ln_g, mem_ln_b, w_in_a, idx_kn_g, idx_kn_b, w_in_b, w_in_c,
           w_mem_kv, w_out, ln1_g, ln1_b, w_gate_up, w_down, ln2_g, ln2_b):
    bsz, seq, d = x.shape
    t = bsz * seq
    depth = w_out.shape[0]
    alpha = (2 * depth) ** 0.25
    assert seq % TILE == 0 and TILE == MOBA_BLOCK

    tables = _rope_tables(positions)
    mkv_all = _memkv(mem, mem_ln_g, mem_ln_b, w_mem_kv.astype(BF16))
    xf = x.reshape(t, d)
    xb = xf
    dummy_ln = jnp.zeros((1, LANES), F32)
    m2, m3 = 2 * MIX_WIDTH, 3 * MIX_WIDTH
    b3 = lambda a: a.reshape(bsz, seq, a.shape[-1])
    qk_sections = [(0, MIX_WIDTH, "rope", Q_SCALE), (MIX_WIDTH, MIX_WIDTH, "rope", 1.0)]

    for i in range(depth):
        kind, j = i % 3, i // 3
        wo = w_out[i]
        if kind == 0:
            w = w_in_a[j]
            c_wi = m3 + IDX_HEADS * IDX_DIM
            c_ki, c_qm = c_wi + IDX_HEADS, c_wi + IDX_HEADS + IDX_DIM
            w_ki = w[:, c_ki:c_qm]
            w_new = jnp.concatenate([w[:, :m2], w[:, m3:c_wi], w[:, c_qm:], w_ki, w_ki], axis=1).astype(BF16)
            wt_new = jnp.concatenate([w[:, m2:m3], w[:, c_wi:c_ki]], axis=1).T.astype(BF16)
            o_qm = m2 + IDX_HEADS * IDX_DIM
            sections = qk_sections + [(m2, IDX_HEADS * IDX_DIM, "rope", 1.0),
                                      (o_qm, MEM_WIDTH, "plain", SCALE),
                                      (o_qm + MEM_WIDTH, LANES, "ln_rope", 1.0)]
            tsections = [(0, MIX_WIDTH, "tiles", 1.0),
                         (MIX_WIDTH, IDX_HEADS, "flat", float((IDX_HEADS * IDX_DIM) ** -0.5))]
            lng = jnp.tile(idx_kn_g[j], 2).reshape(1, LANES)
            lnb = jnp.tile(idx_kn_b[j], 2).reshape(1, LANES)
            q, k, qi, qm, ki, vt, wit = _inproj(xb, w_new, wt_new, tables, lng, lnb,
                                                sections, [BF16] * 5, tsections, [BF16, F32])
            mix = _dsa(b3(q), b3(k), vt, b3(qi), wit, b3(ki))
        else:
            w = w_in_b[j] if kind == 1 else w_in_c[j]
            w_new = jnp.concatenate([w[:, :m2], w[:, m3:]], axis=1).astype(BF16)
            wt_new = w[:, m2:m3].T.astype(BF16)
            sections = qk_sections + [(m2, MEM_WIDTH, "plain", SCALE)]
            q, k, qm, vt = _inproj(xb, w_new, wt_new, tables, dummy_ln, dummy_ln,
                                   sections, [BF16] * 3, [(0, MIX_WIDTH, "tiles", 1.0)], [BF16])
            if kind == 1:
                mix = _dilated(b3(q), b3(k), vt)
            else:
                k3 = b3(k)
                mix = _moba(b3(q), k3, vt, _kmean(k3))
        xf, xb = _tail(alpha, mix.reshape(t, MIX_WIDTH), qm, mkv_all[i], wo[:MIX_WIDTH].astype(BF16),
                       wo[MIX_WIDTH:].astype(BF16), xf, ln1_g[i], ln1_b[i],
                       w_gate_up[i].astype(BF16), w_down[i].astype(BF16), ln2_g[i], ln2_b[i], seq)
    return xf.reshape(bsz, seq, d)
```

```python
import functools

import jax
import jax.numpy as jnp
import numpy as np
from jax import lax
from jax.experimental import pallas as pl
from jax.experimental.pallas import tpu as pltpu

F32 = jnp.float32
BF16 = jnp.bfloat16
I32 = jnp.int32
I16 = jnp.int16

HEAD_DIM = 64
N_HEADS = 12
MIX_WIDTH = N_HEADS * HEAD_DIM
N_MEM_HEADS = 4
MEM_WIDTH = N_MEM_HEADS * HEAD_DIM
ROPE_DIM = HEAD_DIM // 4
ROPE_HALF = ROPE_DIM // 2
ROPE_THETA = 500000.0
IDX_HEADS = 8
IDX_DIM = 64
TOPK_MAX = 256
DIL_PAIRS = ((128, 1), (512, 4), (2048, 16))
B_GROUP_HEADS = 4
B_V_DIM = MIX_WIDTH // B_GROUP_HEADS
MOBA_BLOCK = 256
MOBA_TOPK = 3
LN_EPS = 1e-5
SCALE = HEAD_DIM ** -0.5
LOG2E = 1.4426950408889634
Q_SCALE = SCALE * LOG2E

LANES = 128
SUBLANES = 8
VMEM_LIMIT_BYTES = 56 * 1024 * 1024

TILE = 256
ONES_ROWS = 16
HEADS_PER_STEP = 12
NEG = -1e30
INT_MIN = np.int32(-2 ** 31)
I16_MIN = -2 ** 15
PACKED_ROWS = 2 * SUBLANES


def _cparams(n_axes):
    return pltpu.CompilerParams(dimension_semantics=("arbitrary",) * n_axes,
                                vmem_limit_bytes=VMEM_LIMIT_BYTES)


def _layer_norm(y, g, b):
    mu = jnp.mean(y, axis=-1, keepdims=True)
    yc = y - mu
    var = jnp.mean(yc * yc, axis=-1, keepdims=True)
    return yc * lax.rsqrt(var + LN_EPS) * g + b


def _dot_nt(a, b):
    return lax.dot_general(a, b, (((1,), (1,)), ((), ())), preferred_element_type=F32)


def _half_masks(rows):
    lane = lax.broadcasted_iota(I32, (rows, LANES), 1)
    return lane < HEAD_DIM, lane >= HEAD_DIM


def _store_heads(q_ref, qh_ref):
    lo, hi = _half_masks(q_ref.shape[1])
    for head in range(qh_ref.shape[0]):
        qp = q_ref[0, :, _head_cols(head)]
        qh_ref[head] = jnp.where(hi if head % 2 else lo, qp, jnp.zeros_like(qp))


def _head_cols(head):
    return slice((head // 2) * LANES, (head // 2 + 1) * LANES)


def _key_tile(j, ntiles=1):
    return pl.ds(pl.multiple_of(j * TILE, TILE), ntiles * TILE)


def _flash_over_tiles(states, lo, hi, step):
    npairs = (hi - lo) // 2
    states = lax.fori_loop(0, npairs, lambda jj, st: step(st, lo + 2 * jj, 2), states)
    return lax.cond((hi - lo) % 2 == 1, lambda st: step(st, hi - 1, 1), lambda st: st, states)


def _values_t(vt_ref, j, ntiles, rows):
    return jnp.concatenate([vt_ref[j + c, rows, :] for c in range(ntiles)], axis=1)


def _flash_init(dv):
    return (jnp.full((1, TILE), NEG, F32), jnp.zeros((dv + ONES_ROWS, TILE), F32))


def _flash_step(states, qk, biases, vts):
    n = len(states)
    ones = jnp.ones((ONES_ROWS, vts[0].shape[1]), BF16)
    logits = []
    for h in range(n):
        raw = _dot_nt(*qk[h])
        logits.append(jnp.concatenate([raw[c * TILE:(c + 1) * TILE] + b for c, b in enumerate(biases[h])],
                                      axis=0))
    mids = []
    for (m_old, acc), s in zip(states, logits):
        m_new = jnp.maximum(m_old, jnp.max(s, axis=0, keepdims=True))
        p = jnp.exp2(s - m_new).astype(BF16)
        mids.append((m_new, jnp.exp2(m_old - m_new), p))
    out = []
    for (m_new, alpha, p), (_, acc), vt in zip(mids, states, vts):
        pv = jnp.dot(jnp.concatenate([vt, ones], axis=0), p, preferred_element_type=F32)
        out.append((m_new, alpha * acc + pv))
    return tuple(out)


def _flash_finish(o_ref, states, col0):
    out_t = jnp.concatenate([acc[:-ONES_ROWS] * (1.0 / acc[-1:]) for _, acc in states], axis=0)
    for c in range(out_t.shape[0] // LANES):
        o_ref[0, :, col0 + c * LANES:col0 + (c + 1) * LANES] = (
            out_t[c * LANES:(c + 1) * LANES, :].T.astype(o_ref.dtype))


def _memkv_kernel(mem_ref, g_ref, b_ref, w_ref, o_ref):
    mn = _layer_norm(mem_ref[0], g_ref[...], b_ref[...])
    o_ref[0, 0] = jnp.dot(mn.astype(BF16), w_ref[0], preferred_element_type=F32).astype(BF16)


def _memkv(mem, g, b, w_bf16):
    depth = w_bf16.shape[0]
    bsz, n_mem, d = mem.shape
    wout = w_bf16.shape[2]
    return pl.pallas_call(
        _memkv_kernel,
        grid=(depth, bsz),
        in_specs=[pl.BlockSpec((1, n_mem, d), lambda i, b_: (b_, 0, 0)),
                  pl.BlockSpec((1, d), lambda i, b_: (0, 0)),
                  pl.BlockSpec((1, d), lambda i, b_: (0, 0)),
                  pl.BlockSpec((1, d, wout), lambda i, b_: (i, 0, 0))],
        out_specs=pl.BlockSpec((1, 1, n_mem, wout), lambda i, b_: (i, b_, 0, 0)),
        out_shape=jax.ShapeDtypeStruct((depth, bsz, n_mem, wout), BF16),
        compiler_params=_cparams(2), name="memkv",
    )(mem, g.reshape(1, d), b.reshape(1, d), w_bf16)


def _rope(h, c, s1, s2):
    return h * c + pltpu.roll(h, ROPE_HALF, 1) * s1 + pltpu.roll(h, LANES - ROPE_HALF, 1) * s2


def _inproj_kernel(sections, tsections, x_ref, w_ref, wt_ref, c_ref, s1_ref, s2_ref, lng_ref, lnb_ref,
                   *out_refs):
    x = x_ref[...].astype(BF16)
    c, s1, s2 = c_ref[...], s1_ref[...], s2_ref[...]
    for (start, width, kind, scale), o_ref in zip(sections, out_refs):
        h = jnp.dot(x, w_ref[:, start:start + width], preferred_element_type=F32)
        if kind == "ln_rope":
            h = _layer_norm(h, lng_ref[...], lnb_ref[...])
        for ch in range(width // LANES):
            hc = h[:, ch * LANES:(ch + 1) * LANES]
            if kind in ("rope", "ln_rope"):
                hc = _rope(hc, c, s1, s2)
            if scale != 1.0:
                hc = hc * scale
            o_ref[:, ch * LANES:(ch + 1) * LANES] = hc.astype(o_ref.dtype)
    for (start, rows, kind, scale), o_ref in zip(tsections, out_refs[len(sections):]):
        ht = _dot_nt(wt_ref[start:start + rows, :], x)
        if scale != 1.0:
            ht = ht * scale
        if kind == "tiles":
            for tl in range(o_ref.shape[0]):
                o_ref[tl] = ht[:, tl * TILE:(tl + 1) * TILE].astype(o_ref.dtype)
        else:
            o_ref[...] = ht.astype(o_ref.dtype)


def _inproj(xb, w_bf16, wt_bf16, tables, lng, lnb, sections, out_dtypes, tsections, tout_dtypes, tm=512):
    t, d = xb.shape
    row = lambda i: (i, 0)
    fixed = lambda i: (0, 0)
    out_shape = [jax.ShapeDtypeStruct((t, sec[1]), dt) for sec, dt in zip(sections, out_dtypes)]
    out_specs = [pl.BlockSpec((tm, sec[1]), row) for sec in sections]
    for (_, rows, kind, _), dt in zip(tsections, tout_dtypes):
        if kind == "tiles":
            out_shape.append(jax.ShapeDtypeStruct((t // TILE, rows, TILE), dt))
            out_specs.append(pl.BlockSpec((tm // TILE, rows, TILE), lambda i: (i, 0, 0)))
        else:
            out_shape.append(jax.ShapeDtypeStruct((rows, t), dt))
            out_specs.append(pl.BlockSpec((rows, tm), lambda i: (0, i)))
    return pl.pallas_call(
        functools.partial(_inproj_kernel, tuple(sections), tuple(tsections)),
        grid=(t // tm,),
        in_specs=[pl.BlockSpec((tm, d), row),
                  pl.BlockSpec(w_bf16.shape, fixed),
                  pl.BlockSpec(wt_bf16.shape, fixed),
                  pl.BlockSpec((tm, LANES), row),
                  pl.BlockSpec((tm, LANES), row),
                  pl.BlockSpec((tm, LANES), row),
                  pl.BlockSpec((1, LANES), fixed),
                  pl.BlockSpec((1, LANES), fixed)],
        out_specs=out_specs,
        out_shape=out_shape,
        compiler_params=_cparams(1), name="inproj",
    )(xb, w_bf16, wt_bf16, *tables, lng, lnb)


def _dsa_kernel(topk, q_ref, k_ref, vt_ref, qi_ref, wit_ref, ki_ref, o_ref,
                keys_ref, hi_ref, lo_ref, qh_ref, qih_ref):
    i = pl.program_id(1)
    nkt = i + 1
    krow = lax.broadcasted_iota(I32, (TILE, TILE), 0)
    qcol = lax.broadcasted_iota(I32, (TILE, TILE), 1)
    _store_heads(q_ref, qh_ref)
    _store_heads(qi_ref, qih_ref)

    wt = wit_ref[...]

    last_tile = ki_ref.shape[1] // TILE - 1

    def score_pair(jj, carry):
        tiles = [2 * jj + c for c in range(2)]
        dots = [[_dot_nt(ki_ref[0, _key_tile(jnp.minimum(j, last_tile)), :], qih_ref[h])
                 for h in range(IDX_HEADS)] for j in tiles]
        for j, dj in zip(tiles, dots):
            sc = jnp.zeros((TILE, TILE), F32)
            for h in range(IDX_HEADS):
                sc = sc + wt[h:h + 1, :] * jnp.maximum(dj[h], 0.0)
            bits = lax.bitcast_convert_type(sc, I32)
            key = bits ^ (lax.shift_right_arithmetic(bits, 31) & np.int32(0x7FFFFFFF))
            causal = (j * TILE + krow) <= (i * TILE + qcol)
            key = jnp.where(causal, key, INT_MIN)
            keys_ref[j] = key
            hi_ref[j] = lax.shift_right_arithmetic(key, 16).astype(I16)
            lo_ref[j] = ((key & 0xFFFF) + I16_MIN).astype(I16)
        return carry

    lax.fori_loop(0, (nkt + 1) // 2, score_pair, 0)

    def count16(ref, pred):
        def body(jj, acc):
            part = None
            for c in range(2):
                hits = jnp.where(pred(ref[2 * jj + c]), jnp.ones((), BF16), jnp.zeros((), BF16))
                hits = hits.reshape(TILE // PACKED_ROWS, PACKED_ROWS, TILE)
                for r in range(TILE // PACKED_ROWS):
                    part = hits[r] if part is None else part + hits[r]
            return acc + part.astype(F32)
        acc = lax.fori_loop(0, (nkt + 1) // 2, body, jnp.zeros((PACKED_ROWS, TILE), F32))
        return jnp.sum(acc, axis=0, keepdims=True)

    n_walked = (2 * ((nkt + 1) // 2) * TILE).astype(F32)

    def kth16(ref, k_q):
        zero = jnp.zeros((1, TILE), I32)
        c0 = count16(ref, lambda x: x >= zero.astype(I16))
        ok0 = c0 >= k_q
        init = (jnp.where(ok0, zero, zero + I16_MIN), jnp.where(ok0, c0, n_walked), jnp.where(ok0, 0.0, c0))

        def bisect(it, carry):
            t, c_ge, c_gt = carry
            cand = t | lax.shift_left(np.int32(1), 14 - it)
            c = count16(ref, lambda x: x >= cand.astype(I16))
            ok = c >= k_q
            return jnp.where(ok, cand, t), jnp.where(ok, c, c_ge), jnp.where(ok, c_gt, c)

        return lax.fori_loop(0, 15, bisect, init)

    t_q = i * TILE + lax.broadcasted_iota(I32, (1, TILE), 1)
    k_q = jnp.minimum(topk, t_q + 1).astype(F32)
    thr_hi, _, above_hi = kth16(hi_ref, k_q)
    thr_hi16 = thr_hi.astype(I16)
    k_lo = k_q - above_hi

    def park(j, carry):
        lo_ref[j] = jnp.where(hi_ref[j] == thr_hi16, lo_ref[j], jnp.full((), I16_MIN, I16))
        return carry

    lax.fori_loop(0, nkt, park, 0)
    thr_lo, at_or_above_lo, above_lo = kth16(lo_ref, k_lo)
    thr = lax.shift_left(thr_hi, 16) | ((thr_lo - I16_MIN) & 0xFFFF)

    need = k_lo - above_lo
    n_eq = at_or_above_lo - above_lo
    has_tie = jnp.max(jnp.where(n_eq > need, 1.0, 0.0))

    @pl.when(has_tie > 0.0)
    def _():
        tril = jnp.where(qcol <= krow, 1.0, 0.0).astype(BF16)

        def fix(j, seen):
            kt = keys_ref[j]
            eq = kt == thr
            eqf = jnp.where(eq, 1.0, 0.0)
            pref = jnp.dot(tril, eqf.astype(BF16), preferred_element_type=F32) + seen
            keys_ref[j] = jnp.where(eq & (pref > need), INT_MIN, kt)
            return seen + jnp.sum(eqf, axis=0, keepdims=True)

        lax.fori_loop(0, nkt, fix, jnp.zeros((1, TILE), F32))

    for h0 in range(0, N_HEADS, HEADS_PER_STEP):
        heads = range(h0, h0 + HEADS_PER_STEP)

        def step(states, j, ntiles, heads=heads):
            ks = _key_tile(j, ntiles)
            bias = tuple(jnp.where(keys_ref[j + c] >= thr, 0.0, NEG) for c in range(ntiles))
            return _flash_step(states, [(k_ref[0, ks, _head_cols(h)], qh_ref[h]) for h in heads],
                               [bias] * len(heads),
                               [_values_t(vt_ref, j, ntiles, slice(h * HEAD_DIM, (h + 1) * HEAD_DIM))
                                for h in heads])

        states = _flash_over_tiles(tuple(_flash_init(HEAD_DIM) for _ in heads), 0, nkt, step)
        _flash_finish(o_ref, states, h0 * HEAD_DIM)


def _dsa(q, k, vt, qi, wit, ki):
    bsz, s, _ = q.shape
    topk = min(TOPK_MAX, s // 4)
    nt = s // TILE
    qtile = lambda w: pl.BlockSpec((1, TILE, w), lambda b, i: (b, i, 0))
    full = lambda w: pl.BlockSpec((1, s, w), lambda b, i: (b, 0, 0))
    return pl.pallas_call(
        functools.partial(_dsa_kernel, topk),
        grid=(bsz, nt),
        in_specs=[qtile(MIX_WIDTH), full(MIX_WIDTH),
                  pl.BlockSpec((nt, MIX_WIDTH, TILE), lambda b, i: (b, 0, 0)),
                  qtile(IDX_HEADS * IDX_DIM),
                  pl.BlockSpec((IDX_HEADS, TILE), lambda b, i: (0, b * nt + i)),
                  full(LANES)],
        out_specs=qtile(MIX_WIDTH),
        out_shape=jax.ShapeDtypeStruct((bsz, s, MIX_WIDTH), BF16),
        scratch_shapes=[pltpu.VMEM((nt + nt % 2, TILE, TILE), I32),
                        pltpu.VMEM((nt + nt % 2, TILE, TILE), I16),
                        pltpu.VMEM((nt + nt % 2, TILE, TILE), I16),
                        pltpu.VMEM((N_HEADS, TILE, LANES), BF16),
                        pltpu.VMEM((IDX_HEADS, TILE, LANES), BF16)],
        compiler_params=_cparams(2), name="dsa",
    )(q, k, vt, qi, wit, ki)


def _dilated_kernel(q_ref, k_ref, vt_ref, o_ref, qh_ref):
    i = pl.program_id(1)
    rel = (lax.broadcasted_iota(I32, (TILE, TILE), 1)
           - lax.broadcasted_iota(I32, (TILE, TILE), 0))
    _store_heads(q_ref, qh_ref)
    states = tuple(_flash_init(B_V_DIM) for _ in range(B_GROUP_HEADS))
    for g, (window, dilation) in enumerate(DIL_PAIRS):
        heads = [g * B_GROUP_HEADS + hv for hv in range(B_GROUP_HEADS)]
        nback = -(-window // TILE)

        def step(states, j, ntiles, heads=heads, window=window, dilation=dilation):
            ks = _key_tile(j, ntiles)
            bias = []
            for c in range(ntiles):
                dist = rel + (i - j - c) * TILE
                ok = (dist >= 0) & (dist <= window) & ((dist & (dilation - 1)) == 0)
                bias.append(jnp.where(ok, 0.0, NEG))
            return _flash_step(states, [(k_ref[0, ks, _head_cols(h)], qh_ref[h]) for h in heads],
                               [tuple(bias)] * len(heads),
                               [_values_t(vt_ref, j, ntiles, slice(hv * B_V_DIM, (hv + 1) * B_V_DIM))
                                for hv in range(B_GROUP_HEADS)])

        states = _flash_over_tiles(states, jnp.maximum(i - nback, 0), i + 1, step)
    _flash_finish(o_ref, states, 0)


def _dilated(q, k, vt):
    bsz, s, w = q.shape
    nt = s // TILE
    return pl.pallas_call(
        _dilated_kernel,
        grid=(bsz, nt),
        in_specs=[pl.BlockSpec((1, TILE, w), lambda b, i: (b, i, 0)),
                  pl.BlockSpec((1, s, w), lambda b, i: (b, 0, 0)),
                  pl.BlockSpec((nt, w, TILE), lambda b, i: (b, 0, 0))],
        out_specs=pl.BlockSpec((1, TILE, w), lambda b, i: (b, i, 0)),
        out_shape=jax.ShapeDtypeStruct((bsz, s, w), BF16),
        scratch_shapes=[pltpu.VMEM((N_HEADS, TILE, LANES), BF16)],
        compiler_params=_cparams(2), name="dilated",
    )(q, k, vt)


def _kmean_kernel(nblk, k_ref, o_ref):
    o_ref[0] = jnp.zeros(o_ref.shape[1:], o_ref.dtype)
    for n in range(nblk):
        kb = k_ref[0, n * MOBA_BLOCK:(n + 1) * MOBA_BLOCK, :].astype(F32)
        o_ref[0, n:n + 1, :] = jnp.mean(kb, axis=0, keepdims=True).astype(o_ref.dtype)


def _kmean(k):
    bsz, s, w = k.shape
    return pl.pallas_call(
        functools.partial(_kmean_kernel, s // MOBA_BLOCK),
        grid=(bsz,),
        in_specs=[pl.BlockSpec((1, s, w), lambda b: (b, 0, 0))],
        out_specs=pl.BlockSpec((1, LANES, w), lambda b: (b, 0, 0)),
        out_shape=jax.ShapeDtypeStruct((bsz, LANES, w), BF16),
        compiler_params=_cparams(1), name="kmean",
    )(k)


def _moba_kernel(nbp, q_ref, k_ref, vt_ref, km_ref, o_ref, sel_ref, qh_ref):
    i = pl.program_id(1)
    blk = lax.broadcasted_iota(I32, (nbp, TILE), 0)
    blk_f = blk.astype(F32)
    krow = lax.broadcasted_iota(I32, (TILE, TILE), 0)
    qcol = lax.broadcasted_iota(I32, (TILE, TILE), 1)
    causal_bias = jnp.where(krow <= qcol, 0.0, NEG)
    _store_heads(q_ref, qh_ref)

    for h in range(N_HEADS):
        g = jnp.where(blk < i, _dot_nt(km_ref[0, :, _head_cols(h)], qh_ref[h])[:nbp], -jnp.inf)
        sel = jnp.zeros((nbp, TILE), F32)
        for r in range(MOBA_TOPK):
            mx = jnp.max(g, axis=0, keepdims=True)
            idx = jnp.min(jnp.where(g == mx, blk_f, float(nbp)), axis=0, keepdims=True)
            hit = blk_f == idx
            sel = jnp.where(hit & (jnp.full((nbp, TILE), r, I32) < i), 1.0, sel)
            g = jnp.where(hit, -jnp.inf, g)
        sel_ref[h] = sel

    for h0 in range(0, N_HEADS, HEADS_PER_STEP):
        heads = range(h0, h0 + HEADS_PER_STEP)

        def step(states, j, ntiles, bias_of, heads=heads):
            ks = _key_tile(j, ntiles)
            return _flash_step(states, [(k_ref[0, ks, _head_cols(h)], qh_ref[h]) for h in heads],
                               [tuple(bias_of(h, j + c) for c in range(ntiles)) for h in heads],
                               [_values_t(vt_ref, j, ntiles, slice(h * HEAD_DIM, (h + 1) * HEAD_DIM))
                                for h in heads])

        def past(states, n, ntiles, step=step):
            picked = lambda h, blk_n: jnp.where(sel_ref[h, pl.ds(blk_n, 1), :] > 0.5, 0.0, NEG)
            return step(states, n, ntiles, picked)

        states = _flash_over_tiles(tuple(_flash_init(HEAD_DIM) for _ in heads), 0, i, past)
        states = step(states, i, 1, lambda h, blk_n: causal_bias)
        _flash_finish(o_ref, states, h0 * HEAD_DIM)


def _moba(q, k, vt, kmean):
    bsz, s, w = q.shape
    nt = s // TILE
    nbp = -(-nt // SUBLANES) * SUBLANES
    return pl.pallas_call(
        functools.partial(_moba_kernel, nbp),
        grid=(bsz, nt),
        in_specs=[pl.BlockSpec((1, TILE, w), lambda b, i: (b, i, 0)),
                  pl.BlockSpec((1, s, w), lambda b, i: (b, 0, 0)),
                  pl.BlockSpec((nt, w, TILE), lambda b, i: (b, 0, 0)),
                  pl.BlockSpec((1, LANES, w), lambda b, i: (b, 0, 0))],
        out_specs=pl.BlockSpec((1, TILE, w), lambda b, i: (b, i, 0)),
        out_shape=jax.ShapeDtypeStruct((bsz, s, w), BF16),
        scratch_shapes=[pltpu.VMEM((N_HEADS, nbp, TILE), F32),
                        pltpu.VMEM((N_HEADS, TILE, LANES), BF16)],
        compiler_params=_cparams(2), name="moba",
    )(q, k, vt, kmean)


def _tail_kernel(alpha, tf, mix_ref, qm_ref, mkv_ref, wmix_ref, wmem_ref, x_ref, g1_ref, b1_ref,
                 wgu_ref, wd_ref, g2_ref, b2_ref, xo_ref, xb_ref, h_ref):
    rows = mix_ref.shape[0]
    lo, hi = _half_masks(rows)
    mo = []
    for p in range(MEM_WIDTH // LANES):
        qp = qm_ref[:, p * LANES:(p + 1) * LANES]
        mk = mkv_ref[0, :, p * LANES:(p + 1) * LANES]
        mv = mkv_ref[0, :, MEM_WIDTH + p * LANES:MEM_WIDTH + (p + 1) * LANES]
        outs = []
        for half in (lo, hi):
            s = _dot_nt(jnp.where(half, qp, jnp.zeros_like(qp)), mk)
            e = jnp.exp(s - jnp.max(s, axis=1, keepdims=True))
            pv = jnp.dot(e.astype(BF16), mv, preferred_element_type=F32)
            outs.append(pv / jnp.sum(e, axis=1, keepdims=True))
        mo.append(jnp.where(lo, outs[0], outs[1]).astype(BF16))
    mixed = jnp.dot(mix_ref[...], wmix_ref[...], preferred_element_type=F32)
    for p, mo_p in enumerate(mo):
        mixed = mixed + jnp.dot(mo_p, wmem_ref[p * LANES:(p + 1) * LANES, :],
                                preferred_element_type=F32)
    x1 = _layer_norm(alpha * x_ref[...] + mixed, g1_ref[...], b1_ref[...])

    x1b = x1.astype(BF16)
    dff = wd_ref.shape[0]
    for c in range(dff // tf):
        gate = jnp.dot(x1b, wgu_ref[:, c * tf:(c + 1) * tf], preferred_element_type=F32)
        up = jnp.dot(x1b, wgu_ref[:, dff + c * tf:dff + (c + 1) * tf], preferred_element_type=F32)
        h_ref[:, c * tf:(c + 1) * tf] = (gate * jax.nn.sigmoid(gate) * up).astype(BF16)
    y = jnp.dot(h_ref[...], wd_ref[...], preferred_element_type=F32)
    x2 = _layer_norm(alpha * x1 + y, g2_ref[...], b2_ref[...])
    xo_ref[...] = x2
    xb_ref[...] = x2.astype(BF16)


def _tail(alpha, mix, qm, mkv, wmix, wmem, xf, g1, b1, wgu, wd, g2, b2, seq, tm=512, tf=256):
    t, d = xf.shape
    wm = mix.shape[1]
    dff = wd.shape[0]
    per_batch = seq // tm
    row = lambda i: (i, 0)
    fixed = lambda i: (0, 0)
    resident = lambda shape: pl.BlockSpec(shape, fixed, pipeline_mode=pl.Buffered(1))
    vec = lambda a: a.reshape(1, d)
    return pl.pallas_call(
        functools.partial(_tail_kernel, alpha, tf),
        grid=(t // tm,),
        in_specs=[pl.BlockSpec((tm, wm), row),
                  pl.BlockSpec((tm, MEM_WIDTH), row),
                  pl.BlockSpec((1,) + mkv.shape[1:], lambda i: (i // per_batch, 0, 0)),
                  resident(wmix.shape), resident(wmem.shape),
                  pl.BlockSpec((tm, d), row),
                  resident((1, d)), resident((1, d)),
                  resident(wgu.shape), resident(wd.shape),
                  resident((1, d)), resident((1, d))],
        out_specs=[pl.BlockSpec((tm, d), row), pl.BlockSpec((tm, d), row)],
        out_shape=[jax.ShapeDtypeStruct((t, d), F32), jax.ShapeDtypeStruct((t, d), BF16)],
        scratch_shapes=[pltpu.VMEM((tm, dff), BF16)],
        compiler_params=_cparams(1), name="tail",
    )(mix, qm, mkv, wmix, wmem, xf, vec(g1), vec(b1), wgu, wd, vec(g2), vec(b2))


def _rope_tables(positions):
    inv = ROPE_THETA ** (-jnp.arange(ROPE_HALF, dtype=F32) / ROPE_HALF)
    ang = positions.astype(F32)[..., None] * inv
    cos, sin = jnp.cos(ang), jnp.sin(ang)
    rest = HEAD_DIM - ROPE_DIM
    pad = lambda a, before, after, val: jnp.pad(a, ((0, 0), (0, 0), (before, after)), constant_values=val)
    c = pad(jnp.concatenate([cos, cos], -1), 0, rest, 1.0)
    s1 = pad(sin, ROPE_HALF, rest, 0.0)
    s2 = pad(-sin, 0, ROPE_HALF + rest, 0.0)
    t = positions.shape[0] * positions.shape[1]
    return tuple(jnp.tile(a, (1, 1, LANES // HEAD_DIM)).reshape(t, LANES) for a in (c, s1, s2))


def kernel(x, mem, positions, mem_ln_g, mem_ln_b, w_in_a, idx_kn_g, idx_kn_b, w_in_b, w_in_c,
           w_mem_kv, w_out, ln1_g, ln1_b, w_gate_up, w_down, ln2_g, ln2_b):
    bsz, seq, d = x.shape
    t = bsz * seq
    depth = w_out.shape[0]
    alpha = (2 * depth) ** 0.25
    assert seq % TILE == 0 and TILE == MOBA_BLOCK

    tables = _rope_tables(positions)
    mkv_all = _memkv(mem, mem_ln_g, mem_ln_b, w_mem_kv.astype(BF16))
    xf = x.reshape(t, d)
    xb = xf
    dummy_ln = jnp.zeros((1, LANES), F32)
    m2, m3 = 2 * MIX_WIDTH, 3 * MIX_WIDTH
    b3 = lambda a: a.reshape(bsz, seq, a.shape[-1])
    qk_sections = [(0, MIX_WIDTH, "rope", Q_SCALE), (MIX_WIDTH, MIX_WIDTH, "rope", 1.0)]

    for i in range(depth):
        kind, j = i % 3, i // 3
        wo = w_out[i]
        if kind == 0:
            w = w_in_a[j]
            c_wi = m3 + IDX_HEADS * IDX_DIM
            c_ki, c_qm = c_wi + IDX_HEADS, c_wi + IDX_HEADS + IDX_DIM
            w_ki = w[:, c_ki:c_qm]
            w_new = jnp.concatenate([w[:, :m2], w[:, m3:c_wi], w[:, c_qm:], w_ki, w_ki], axis=1).astype(BF16)
            wt_new = jnp.concatenate([w[:, m2:m3], w[:, c_wi:c_ki]], axis=1).T.astype(BF16)
            o_qm = m2 + IDX_HEADS * IDX_DIM
            sections = qk_sections + [(m2, IDX_HEADS * IDX_DIM, "rope", 1.0),
                                      (o_qm, MEM_WIDTH, "plain", SCALE),
                                      (o_qm + MEM_WIDTH, LANES, "ln_rope", 1.0)]
            tsections = [(0, MIX_WIDTH, "tiles", 1.0),
                         (MIX_WIDTH, IDX_HEADS, "flat", float((IDX_HEADS * IDX_DIM) ** -0.5))]
            lng = jnp.tile(idx_kn_g[j], 2).reshape(1, LANES)
            lnb = jnp.tile(idx_kn_b[j], 2).reshape(1, LANES)
            q, k, qi, qm, ki, vt, wit = _inproj(xb, w_new, wt_new, tables, lng, lnb,
                                                sections, [BF16] * 5, tsections, [BF16, F32])
            mix = _dsa(b3(q), b3(k), vt, b3(qi), wit, b3(ki))
        else:
            w = w_in_b[j] if kind == 1 else w_in_c[j]
            w_new = jnp.concatenate([w[:, :m2], w[:, m3:]], axis=1).astype(BF16)
            wt_new = w[:, m2:m3].T.astype(BF16)
            sections = qk_sections + [(m2, MEM_WIDTH, "plain", SCALE)]
            q, k, qm, vt = _inproj(xb, w_new, wt_new, tables, dummy_ln, dummy_ln,
                                   sections, [BF16] * 3, [(0, MIX_WIDTH, "tiles", 1.0)], [BF16])
            if kind == 1:
                mix = _dilated(b3(q), b3(k), vt)
            else:
                k3 = b3(k)
                mix = _moba(b3(q), k3, vt, _kmean(k3))
        xf, xb = _tail(alpha, mix.reshape(t, MIX_WIDTH), qm, mkv_all[i], wo[:MIX_WIDTH].astype(BF16),
                       wo[MIX_WIDTH:].astype(BF16), xf, ln1_g[i], ln1_b[i],
                       w_gate_up[i].astype(BF16), w_down[i].astype(BF16), ln2_g[i], ln2_b[i], seq)
    return xf.reshape(bsz, seq, d)
```

```python
import functools

import jax
import jax.numpy as jnp
import numpy as np
from jax import lax
from jax.experimental import pallas as pl
from jax.experimental.pallas import tpu as pltpu

F32 = jnp.float32
BF16 = jnp.bfloat16
I32 = jnp.int32
I16 = jnp.int16

HEAD_DIM = 64
N_HEADS = 12
MIX_WIDTH = N_HEADS * HEAD_DIM
N_MEM_HEADS = 4
MEM_WIDTH = N_MEM_HEADS * HEAD_DIM
ROPE_DIM = HEAD_DIM // 4
ROPE_HALF = ROPE_DIM // 2
ROPE_THETA = 500000.0
IDX_HEADS = 8
IDX_DIM = 64
TOPK_MAX = 256
DIL_PAIRS = ((128, 1), (512, 4), (2048, 16))
B_GROUP_HEADS = 4
B_V_DIM = MIX_WIDTH // B_GROUP_HEADS
MOBA_BLOCK = 256
MOBA_TOPK = 3
LN_EPS = 1e-5
SCALE = HEAD_DIM ** -0.5
LOG2E = 1.4426950408889634
Q_SCALE = SCALE * LOG2E

LANES = 128
SUBLANES = 8
VMEM_LIMIT_BYTES = 56 * 1024 * 1024

TILE = 256
ROW_TILE = 512
FF_CHUNK = 256
HALF_BITS = 16
LOW_MASK = (1 << HALF_BITS) - 1
SIGN_FLIP = np.int32(0x7FFFFFFF)
ONES_ROWS = 16
HEADS_PER_STEP = 12
NEG = -1e30
INT_MIN = np.int32(-2 ** 31)
I16_MIN = -2 ** 15
PACKED_ROWS = 2 * SUBLANES


def _cparams(n_axes):
    return pltpu.CompilerParams(dimension_semantics=("arbitrary",) * n_axes,
                                vmem_limit_bytes=VMEM_LIMIT_BYTES)


def _layer_norm(y, g, b):
    mu = jnp.mean(y, axis=-1, keepdims=True)
    yc = y - mu
    var = jnp.mean(yc * yc, axis=-1, keepdims=True)
    return yc * lax.rsqrt(var + LN_EPS) * g + b


def _dot_nt(a, b):
    return lax.dot_general(a, b, (((1,), (1,)), ((), ())), preferred_element_type=F32)


def _half_masks(rows):
    lane = lax.broadcasted_iota(I32, (rows, LANES), 1)
    return lane < HEAD_DIM, lane >= HEAD_DIM


def _store_heads(q_ref, qh_ref):
    lo, hi = _half_masks(q_ref.shape[1])
    for head in range(qh_ref.shape[0]):
        qp = q_ref[0, :, _head_cols(head)]
        qh_ref[head] = jnp.where(hi if head % 2 else lo, qp, jnp.zeros_like(qp))


def _head_cols(head):
    return slice((head // 2) * LANES, (head // 2 + 1) * LANES)


def _key_tile(j, ntiles=1):
    return pl.ds(pl.multiple_of(j * TILE, TILE), ntiles * TILE)


def _flash_over_tiles(states, lo, hi, step):
    npairs = (hi - lo) // 2
    states = lax.fori_loop(0, npairs, lambda jj, st: step(st, lo + 2 * jj, 2), states)
    return lax.cond((hi - lo) % 2 == 1, lambda st: step(st, hi - 1, 1), lambda st: st, states)


def _values_t(vt_ref, j, ntiles, rows):
    return jnp.concatenate([vt_ref[j + c, rows, :] for c in range(ntiles)], axis=1)


def _flash_init(dv):
    return (jnp.full((1, TILE), NEG, F32), jnp.zeros((dv + ONES_ROWS, TILE), F32))


def _flash_step(states, qk, biases, vts):
    n = len(states)
    ones = jnp.ones((ONES_ROWS, vts[0].shape[1]), BF16)
    logits = []
    for h in range(n):
        raw = _dot_nt(*qk[h])
        logits.append(jnp.concatenate([raw[c * TILE:(c + 1) * TILE] + b for c, b in enumerate(biases[h])],
                                      axis=0))
    mids = []
    for (m_old, acc), s in zip(states, logits):
        m_new = jnp.maximum(m_old, jnp.max(s, axis=0, keepdims=True))
        p = jnp.exp2(s - m_new).astype(BF16)
        mids.append((m_new, jnp.exp2(m_old - m_new), p))
    out = []
    for (m_new, alpha, p), (_, acc), vt in zip(mids, states, vts):
        pv = jnp.dot(jnp.concatenate([vt, ones], axis=0), p, preferred_element_type=F32)
        out.append((m_new, alpha * acc + pv))
    return tuple(out)


def _flash_finish(o_ref, states, col0):
    out_t = jnp.concatenate([acc[:-ONES_ROWS] * (1.0 / acc[-1:]) for _, acc in states], axis=0)
    for c in range(out_t.shape[0] // LANES):
        o_ref[0, :, col0 + c * LANES:col0 + (c + 1) * LANES] = (
            out_t[c * LANES:(c + 1) * LANES, :].T.astype(o_ref.dtype))


def _memkv_kernel(mem_ref, g_ref, b_ref, w_ref, o_ref):
    mn = _layer_norm(mem_ref[0], g_ref[...], b_ref[...])
    o_ref[0, 0] = jnp.dot(mn.astype(BF16), w_ref[0], preferred_element_type=F32).astype(BF16)


def _memkv(mem, g, b, w_bf16):
    depth = w_bf16.shape[0]
    bsz, n_mem, d = mem.shape
    wout = w_bf16.shape[2]
    return pl.pallas_call(
        _memkv_kernel,
        grid=(depth, bsz),
        in_specs=[pl.BlockSpec((1, n_mem, d), lambda i, b_: (b_, 0, 0)),
                  pl.BlockSpec((1, d), lambda i, b_: (0, 0)),
                  pl.BlockSpec((1, d), lambda i, b_: (0, 0)),
                  pl.BlockSpec((1, d, wout), lambda i, b_: (i, 0, 0))],
        out_specs=pl.BlockSpec((1, 1, n_mem, wout), lambda i, b_: (i, b_, 0, 0)),
        out_shape=jax.ShapeDtypeStruct((depth, bsz, n_mem, wout), BF16),
        compiler_params=_cparams(2), name="memkv",
    )(mem, g.reshape(1, d), b.reshape(1, d), w_bf16)


def _rope(h, c, s1, s2):
    return h * c + pltpu.roll(h, ROPE_HALF, 1) * s1 + pltpu.roll(h, LANES - ROPE_HALF, 1) * s2


def _inproj_kernel(sections, tsections, x_ref, w_ref, wt_ref, c_ref, s1_ref, s2_ref, lng_ref, lnb_ref,
                   *out_refs):
    x = x_ref[...].astype(BF16)
    c, s1, s2 = c_ref[...], s1_ref[...], s2_ref[...]
    for (start, width, kind, scale), o_ref in zip(sections, out_refs):
        h = jnp.dot(x, w_ref[:, start:start + width], preferred_element_type=F32)
        if kind == "ln_rope":
            h = _layer_norm(h, lng_ref[...], lnb_ref[...])
        for ch in range(width // LANES):
            hc = h[:, ch * LANES:(ch + 1) * LANES]
            if kind in ("rope", "ln_rope"):
                hc = _rope(hc, c, s1, s2)
            if scale != 1.0:
                hc = hc * scale
            o_ref[:, ch * LANES:(ch + 1) * LANES] = hc.astype(o_ref.dtype)
    for (start, rows, kind, scale), o_ref in zip(tsections, out_refs[len(sections):]):
        ht = _dot_nt(wt_ref[start:start + rows, :], x)
        if scale != 1.0:
            ht = ht * scale
        if kind == "tiles":
            for tl in range(o_ref.shape[0]):
                o_ref[tl] = ht[:, tl * TILE:(tl + 1) * TILE].astype(o_ref.dtype)
        else:
            o_ref[...] = ht.astype(o_ref.dtype)


def _inproj(xb, w_bf16, wt_bf16, tables, lng, lnb, sections, out_dtypes, tsections, tout_dtypes, tm=ROW_TILE):
    t, d = xb.shape
    row = lambda i: (i, 0)
    fixed = lambda i: (0, 0)
    out_shape = [jax.ShapeDtypeStruct((t, sec[1]), dt) for sec, dt in zip(sections, out_dtypes)]
    out_specs = [pl.BlockSpec((tm, sec[1]), row) for sec in sections]
    for (_, rows, kind, _), dt in zip(tsections, tout_dtypes):
        if kind == "tiles":
            out_shape.append(jax.ShapeDtypeStruct((t // TILE, rows, TILE), dt))
            out_specs.append(pl.BlockSpec((tm // TILE, rows, TILE), lambda i: (i, 0, 0)))
        else:
            out_shape.append(jax.ShapeDtypeStruct((rows, t), dt))
            out_specs.append(pl.BlockSpec((rows, tm), lambda i: (0, i)))
    return pl.pallas_call(
        functools.partial(_inproj_kernel, tuple(sections), tuple(tsections)),
        grid=(t // tm,),
        in_specs=[pl.BlockSpec((tm, d), row),
                  pl.BlockSpec(w_bf16.shape, fixed),
                  pl.BlockSpec(wt_bf16.shape, fixed),
                  pl.BlockSpec((tm, LANES), row),
                  pl.BlockSpec((tm, LANES), row),
                  pl.BlockSpec((tm, LANES), row),
                  pl.BlockSpec((1, LANES), fixed),
                  pl.BlockSpec((1, LANES), fixed)],
        out_specs=out_specs,
        out_shape=out_shape,
        compiler_params=_cparams(1), name="inproj",
    )(xb, w_bf16, wt_bf16, *tables, lng, lnb)


def _dsa_kernel(topk, q_ref, k_ref, vt_ref, qi_ref, wit_ref, ki_ref, o_ref,
                keys_ref, hi_ref, lo_ref, qh_ref, qih_ref):
    i = pl.program_id(1)
    nkt = i + 1
    krow = lax.broadcasted_iota(I32, (TILE, TILE), 0)
    qcol = lax.broadcasted_iota(I32, (TILE, TILE), 1)
    _store_heads(q_ref, qh_ref)
    _store_heads(qi_ref, qih_ref)

    wt = wit_ref[...]

    last_tile = ki_ref.shape[1] // TILE - 1

    def score_pair(jj, carry):
        tiles = [2 * jj + c for c in range(2)]
        dots = [[_dot_nt(ki_ref[0, _key_tile(jnp.minimum(j, last_tile)), :], qih_ref[h])
                 for h in range(IDX_HEADS)] for j in tiles]
        for j, dj in zip(tiles, dots):
            sc = jnp.zeros((TILE, TILE), F32)
            for h in range(IDX_HEADS):
                sc = sc + wt[h:h + 1, :] * jnp.maximum(dj[h], 0.0)
            bits = lax.bitcast_convert_type(sc, I32)
            key = bits ^ (lax.shift_right_arithmetic(bits, 31) & SIGN_FLIP)
            causal = (j * TILE + krow) <= (i * TILE + qcol)
            key = jnp.where(causal, key, INT_MIN)
            keys_ref[j] = key
            hi_ref[j] = lax.shift_right_arithmetic(key, HALF_BITS).astype(I16)
            lo_ref[j] = ((key & LOW_MASK) + I16_MIN).astype(I16)
        return carry

    lax.fori_loop(0, (nkt + 1) // 2, score_pair, 0)

    def count16(ref, pred):
        def body(jj, acc):
            part = None
            for c in range(2):
                hits = jnp.where(pred(ref[2 * jj + c]), jnp.ones((), BF16), jnp.zeros((), BF16))
                hits = hits.reshape(TILE // PACKED_ROWS, PACKED_ROWS, TILE)
                for r in range(TILE // PACKED_ROWS):
                    part = hits[r] if part is None else part + hits[r]
            return acc + part.astype(F32)
        acc = lax.fori_loop(0, (nkt + 1) // 2, body, jnp.zeros((PACKED_ROWS, TILE), F32))
        return jnp.sum(acc, axis=0, keepdims=True)

    n_walked = (2 * ((nkt + 1) // 2) * TILE).astype(F32)

    def kth16(ref, k_q):
        zero = jnp.zeros((1, TILE), I32)
        c0 = count16(ref, lambda x: x >= zero.astype(I16))
        ok0 = c0 >= k_q
        init = (jnp.where(ok0, zero, zero + I16_MIN), jnp.where(ok0, c0, n_walked), jnp.where(ok0, 0.0, c0))

        def bisect(it, carry):
            t, c_ge, c_gt = carry
            cand = t | lax.shift_left(np.int32(1), HALF_BITS - 2 - it)
            c = count16(ref, lambda x: x >= cand.astype(I16))
            ok = c >= k_q
            return jnp.where(ok, cand, t), jnp.where(ok, c, c_ge), jnp.where(ok, c_gt, c)

        return lax.fori_loop(0, HALF_BITS - 1, bisect, init)

    t_q = i * TILE + lax.broadcasted_iota(I32, (1, TILE), 1)
    k_q = jnp.minimum(topk, t_q + 1).astype(F32)
    thr_hi, _, above_hi = kth16(hi_ref, k_q)
    thr_hi16 = thr_hi.astype(I16)
    k_lo = k_q - above_hi

    def park(j, carry):
        lo_ref[j] = jnp.where(hi_ref[j] == thr_hi16, lo_ref[j], jnp.full((), I16_MIN, I16))
        return carry

    lax.fori_loop(0, nkt, park, 0)
    thr_lo, at_or_above_lo, above_lo = kth16(lo_ref, k_lo)
    thr = lax.shift_left(thr_hi, HALF_BITS) | ((thr_lo - I16_MIN) & LOW_MASK)

    need = k_lo - above_lo
    n_eq = at_or_above_lo - above_lo
    has_tie = jnp.max(jnp.where(n_eq > need, 1.0, 0.0))

    @pl.when(has_tie > 0.0)
    def _():
        tril = jnp.where(qcol <= krow, 1.0, 0.0).astype(BF16)

        def fix(j, seen):
            kt = keys_ref[j]
            eq = kt == thr
            eqf = jnp.where(eq, 1.0, 0.0)
            pref = jnp.dot(tril, eqf.astype(BF16), preferred_element_type=F32) + seen
            keys_ref[j] = jnp.where(eq & (pref > need), INT_MIN, kt)
            return seen + jnp.sum(eqf, axis=0, keepdims=True)

        lax.fori_loop(0, nkt, fix, jnp.zeros((1, TILE), F32))

    for h0 in range(0, N_HEADS, HEADS_PER_STEP):
        heads = range(h0, h0 + HEADS_PER_STEP)

        def step(states, j, ntiles, heads=heads):
            ks = _key_tile(j, ntiles)
            bias = tuple(jnp.where(keys_ref[j + c] >= thr, 0.0, NEG) for c in range(ntiles))
            return _flash_step(states, [(k_ref[0, ks, _head_cols(h)], qh_ref[h]) for h in heads],
                               [bias] * len(heads),
                               [_values_t(vt_ref, j, ntiles, slice(h * HEAD_DIM, (h + 1) * HEAD_DIM))
                                for h in heads])

        states = _flash_over_tiles(tuple(_flash_init(HEAD_DIM) for _ in heads), 0, nkt, step)
        _flash_finish(o_ref, states, h0 * HEAD_DIM)


def _dsa(q, k, vt, qi, wit, ki):
    bsz, s, _ = q.shape
    topk = min(TOPK_MAX, s // 4)
    nt = s // TILE
    qtile = lambda w: pl.BlockSpec((1, TILE, w), lambda b, i: (b, i, 0))
    full = lambda w: pl.BlockSpec((1, s, w), lambda b, i: (b, 0, 0))
    return pl.pallas_call(
        functools.partial(_dsa_kernel, topk),
        grid=(bsz, nt),
        in_specs=[qtile(MIX_WIDTH), full(MIX_WIDTH),
                  pl.BlockSpec((nt, MIX_WIDTH, TILE), lambda b, i: (b, 0, 0)),
                  qtile(IDX_HEADS * IDX_DIM),
                  pl.BlockSpec((IDX_HEADS, TILE), lambda b, i: (0, b * nt + i)),
                  full(LANES)],
        out_specs=qtile(MIX_WIDTH),
        out_shape=jax.ShapeDtypeStruct((bsz, s, MIX_WIDTH), BF16),
        scratch_shapes=[pltpu.VMEM((nt + nt % 2, TILE, TILE), I32),
                        pltpu.VMEM((nt + nt % 2, TILE, TILE), I16),
                        pltpu.VMEM((nt + nt % 2, TILE, TILE), I16),
                        pltpu.VMEM((N_HEADS, TILE, LANES), BF16),
                        pltpu.VMEM((IDX_HEADS, TILE, LANES), BF16)],
        compiler_params=_cparams(2), name="dsa",
    )(q, k, vt, qi, wit, ki)


def _dilated_kernel(q_ref, k_ref, vt_ref, o_ref, qh_ref):
    i = pl.program_id(1)
    rel = (lax.broadcasted_iota(I32, (TILE, TILE), 1)
           - lax.broadcasted_iota(I32, (TILE, TILE), 0))
    _store_heads(q_ref, qh_ref)
    states = tuple(_flash_init(B_V_DIM) for _ in range(B_GROUP_HEADS))
    for g, (window, dilation) in enumerate(DIL_PAIRS):
        heads = [g * B_GROUP_HEADS + hv for hv in range(B_GROUP_HEADS)]
        nback = -(-window // TILE)

        def step(states, j, ntiles, heads=heads, window=window, dilation=dilation):
            ks = _key_tile(j, ntiles)
            bias = []
            for c in range(ntiles):
                dist = rel + (i - j - c) * TILE
                ok = (dist >= 0) & (dist <= window) & ((dist & (dilation - 1)) == 0)
                bias.append(jnp.where(ok, 0.0, NEG))
            return _flash_step(states, [(k_ref[0, ks, _head_cols(h)], qh_ref[h]) for h in heads],
                               [tuple(bias)] * len(heads),
                               [_values_t(vt_ref, j, ntiles, slice(hv * B_V_DIM, (hv + 1) * B_V_DIM))
                                for hv in range(B_GROUP_HEADS)])

        states = _flash_over_tiles(states, jnp.maximum(i - nback, 0), i + 1, step)
    _flash_finish(o_ref, states, 0)


def _dilated(q, k, vt):
    bsz, s, w = q.shape
    nt = s // TILE
    return pl.pallas_call(
        _dilated_kernel,
        grid=(bsz, nt),
        in_specs=[pl.BlockSpec((1, TILE, w), lambda b, i: (b, i, 0)),
                  pl.BlockSpec((1, s, w), lambda b, i: (b, 0, 0)),
                  pl.BlockSpec((nt, w, TILE), lambda b, i: (b, 0, 0))],
        out_specs=pl.BlockSpec((1, TILE, w), lambda b, i: (b, i, 0)),
        out_shape=jax.ShapeDtypeStruct((bsz, s, w), BF16),
        scratch_shapes=[pltpu.VMEM((N_HEADS, TILE, LANES), BF16)],
        compiler_params=_cparams(2), name="dilated",
    )(q, k, vt)


def _kmean_kernel(nblk, k_ref, o_ref):
    o_ref[0] = jnp.zeros(o_ref.shape[1:], o_ref.dtype)
    for n in range(nblk):
        kb = k_ref[0, n * MOBA_BLOCK:(n + 1) * MOBA_BLOCK, :].astype(F32)
        o_ref[0, n:n + 1, :] = jnp.mean(kb, axis=0, keepdims=True).astype(o_ref.dtype)


def _kmean(k):
    bsz, s, w = k.shape
    return pl.pallas_call(
        functools.partial(_kmean_kernel, s // MOBA_BLOCK),
        grid=(bsz,),
        in_specs=[pl.BlockSpec((1, s, w), lambda b: (b, 0, 0))],
        out_specs=pl.BlockSpec((1, LANES, w), lambda b: (b, 0, 0)),
        out_shape=jax.ShapeDtypeStruct((bsz, LANES, w), BF16),
        compiler_params=_cparams(1), name="kmean",
    )(k)


def _moba_kernel(nbp, q_ref, k_ref, vt_ref, km_ref, o_ref, sel_ref, qh_ref):
    i = pl.program_id(1)
    blk = lax.broadcasted_iota(I32, (nbp, TILE), 0)
    blk_f = blk.astype(F32)
    krow = lax.broadcasted_iota(I32, (TILE, TILE), 0)
    qcol = lax.broadcasted_iota(I32, (TILE, TILE), 1)
    causal_bias = jnp.where(krow <= qcol, 0.0, NEG)
    _store_heads(q_ref, qh_ref)

    for h in range(N_HEADS):
        g = jnp.where(blk < i, _dot_nt(km_ref[0, :, _head_cols(h)], qh_ref[h])[:nbp], -jnp.inf)
        sel = jnp.zeros((nbp, TILE), F32)
        for r in range(MOBA_TOPK):
            mx = jnp.max(g, axis=0, keepdims=True)
            idx = jnp.min(jnp.where(g == mx, blk_f, float(nbp)), axis=0, keepdims=True)
            hit = blk_f == idx
            sel = jnp.where(hit & (jnp.full((nbp, TILE), r, I32) < i), 1.0, sel)
            g = jnp.where(hit, -jnp.inf, g)
        sel_ref[h] = sel

    for h0 in range(0, N_HEADS, HEADS_PER_STEP):
        heads = range(h0, h0 + HEADS_PER_STEP)

        def step(states, j, ntiles, bias_of, heads=heads):
            ks = _key_tile(j, ntiles)
            return _flash_step(states, [(k_ref[0, ks, _head_cols(h)], qh_ref[h]) for h in heads],
                               [tuple(bias_of(h, j + c) for c in range(ntiles)) for h in heads],
                               [_values_t(vt_ref, j, ntiles, slice(h * HEAD_DIM, (h + 1) * HEAD_DIM))
                                for h in heads])

        def past(states, n, ntiles, step=step):
            picked = lambda h, blk_n: jnp.where(sel_ref[h, pl.ds(blk_n, 1), :] > 0.5, 0.0, NEG)
            return step(states, n, ntiles, picked)

        states = _flash_over_tiles(tuple(_flash_init(HEAD_DIM) for _ in heads), 0, i, past)
        states = step(states, i, 1, lambda h, blk_n: causal_bias)
        _flash_finish(o_ref, states, h0 * HEAD_DIM)


def _moba(q, k, vt, kmean):
    bsz, s, w = q.shape
    nt = s // TILE
    nbp = -(-nt // SUBLANES) * SUBLANES
    return pl.pallas_call(
        functools.partial(_moba_kernel, nbp),
        grid=(bsz, nt),
        in_specs=[pl.BlockSpec((1, TILE, w), lambda b, i: (b, i, 0)),
                  pl.BlockSpec((1, s, w), lambda b, i: (b, 0, 0)),
                  pl.BlockSpec((nt, w, TILE), lambda b, i: (b, 0, 0)),
                  pl.BlockSpec((1, LANES, w), lambda b, i: (b, 0, 0))],
        out_specs=pl.BlockSpec((1, TILE, w), lambda b, i: (b, i, 0)),
        out_shape=jax.ShapeDtypeStruct((bsz, s, w), BF16),
        scratch_shapes=[pltpu.VMEM((N_HEADS, nbp, TILE), F32),
                        pltpu.VMEM((N_HEADS, TILE, LANES), BF16)],
        compiler_params=_cparams(2), name="moba",
    )(q, k, vt, kmean)


def _tail_kernel(alpha, tf, mix_ref, qm_ref, mkv_ref, wmix_ref, wmem_ref, x_ref, g1_ref, b1_ref,
                 wgu_ref, wd_ref, g2_ref, b2_ref, xo_ref, xb_ref, h_ref):
    rows = mix_ref.shape[0]
    lo, hi = _half_masks(rows)
    mo = []
    for p in range(MEM_WIDTH // LANES):
        qp = qm_ref[:, p * LANES:(p + 1) * LANES]
        mk = mkv_ref[0, :, p * LANES:(p + 1) * LANES]
        mv = mkv_ref[0, :, MEM_WIDTH + p * LANES:MEM_WIDTH + (p + 1) * LANES]
        outs = []
        for half in (lo, hi):
            s = _dot_nt(jnp.where(half, qp, jnp.zeros_like(qp)), mk)
            e = jnp.exp(s - jnp.max(s, axis=1, keepdims=True))
            pv = jnp.dot(e.astype(BF16), mv, preferred_element_type=F32)
            outs.append(pv / jnp.sum(e, axis=1, keepdims=True))
        mo.append(jnp.where(lo, outs[0], outs[1]).astype(BF16))
    mixed = jnp.dot(mix_ref[...], wmix_ref[...], preferred_element_type=F32)
    for p, mo_p in enumerate(mo):
        mixed = mixed + jnp.dot(mo_p, wmem_ref[p * LANES:(p + 1) * LANES, :],
                                preferred_element_type=F32)
    x1 = _layer_norm(alpha * x_ref[...] + mixed, g1_ref[...], b1_ref[...])

    x1b = x1.astype(BF16)
    dff = wd_ref.shape[0]
    for c in range(dff // tf):
        gate = jnp.dot(x1b, wgu_ref[:, c * tf:(c + 1) * tf], preferred_element_type=F32)
        up = jnp.dot(x1b, wgu_ref[:, dff + c * tf:dff + (c + 1) * tf], preferred_element_type=F32)
        h_ref[:, c * tf:(c + 1) * tf] = (gate * jax.nn.sigmoid(gate) * up).astype(BF16)
    y = jnp.dot(h_ref[...], wd_ref[...], preferred_element_type=F32)
    x2 = _layer_norm(alpha * x1 + y, g2_ref[...], b2_ref[...])
    xo_ref[...] = x2
    xb_ref[...] = x2.astype(BF16)


def _tail(alpha, mix, qm, mkv, wmix, wmem, xf, g1, b1, wgu, wd, g2, b2, seq, tm=ROW_TILE, tf=FF_CHUNK):
    t, d = xf.shape
    wm = mix.shape[1]
    dff = wd.shape[0]
    per_batch = seq // tm
    row = lambda i: (i, 0)
    fixed = lambda i: (0, 0)
    resident = lambda shape: pl.BlockSpec(shape, fixed, pipeline_mode=pl.Buffered(1))
    vec = lambda a: a.reshape(1, d)
    return pl.pallas_call(
        functools.partial(_tail_kernel, alpha, tf),
        grid=(t // tm,),
        in_specs=[pl.BlockSpec((tm, wm), row),
                  pl.BlockSpec((tm, MEM_WIDTH), row),
                  pl.BlockSpec((1,) + mkv.shape[1:], lambda i: (i // per_batch, 0, 0)),
                  resident(wmix.shape), resident(wmem.shape),
                  pl.BlockSpec((tm, d), row),
                  resident((1, d)), resident((1, d)),
                  resident(wgu.shape), resident(wd.shape),
                  resident((1, d)), resident((1, d))],
        out_specs=[pl.BlockSpec((tm, d), row), pl.BlockSpec((tm, d), row)],
        out_shape=[jax.ShapeDtypeStruct((t, d), F32), jax.ShapeDtypeStruct((t, d), BF16)],
        scratch_shapes=[pltpu.VMEM((tm, dff), BF16)],
        compiler_params=_cparams(1), name="tail",
    )(mix, qm, mkv, wmix, wmem, xf, vec(g1), vec(b1), wgu, wd, vec(g2), vec(b2))


def _rope_tables(positions):
    inv = ROPE_THETA ** (-jnp.arange(ROPE_HALF, dtype=F32) / ROPE_HALF)
    ang = positions.astype(F32)[..., None] * inv
    cos, sin = jnp.cos(ang), jnp.sin(ang)
    rest = HEAD_DIM - ROPE_DIM
    pad = lambda a, before, after, val: jnp.pad(a, ((0, 0), (0, 0), (before, after)), constant_values=val)
    c = pad(jnp.concatenate([cos, cos], -1), 0, rest, 1.0)
    s1 = pad(sin, ROPE_HALF, rest, 0.0)
    s2 = pad(-sin, 0, ROPE_HALF + rest, 0.0)
    t = positions.shape[0] * positions.shape[1]
    return tuple(jnp.tile(a, (1, 1, LANES // HEAD_DIM)).reshape(t, LANES) for a in (c, s1, s2))


def kernel(x, mem, positions, mem_ln_g, mem_ln_b, w_in_a, idx_kn_g, idx_kn_b, w_in_b, w_in_c,
           w_mem_kv, w_out, ln1_g, ln1_b, w_gate_up, w_down, ln2_g, ln2_b):
    bsz, seq, d = x.shape
    t = bsz * seq
    depth = w_out.shape[0]
    alpha = (2 * depth) ** 0.25
    assert seq % TILE == 0 and TILE == MOBA_BLOCK

    tables = _rope_tables(positions)
    mkv_all = _memkv(mem, mem_ln_g, mem_ln_b, w_mem_kv.astype(BF16))
    xf = x.reshape(t, d)
    xb = xf
    dummy_ln = jnp.zeros((1, LANES), F32)
    m2, m3 = 2 * MIX_WIDTH, 3 * MIX_WIDTH
    b3 = lambda a: a.reshape(bsz, seq, a.shape[-1])
    qk_sections = [(0, MIX_WIDTH, "rope", Q_SCALE), (MIX_WIDTH, MIX_WIDTH, "rope", 1.0)]

    for i in range(depth):
        kind, j = i % 3, i // 3
        wo = w_out[i]
        if kind == 0:
            w = w_in_a[j]
            c_wi = m3 + IDX_HEADS * IDX_DIM
            c_ki, c_qm = c_wi + IDX_HEADS, c_wi + IDX_HEADS + IDX_DIM
            w_ki = w[:, c_ki:c_qm]
            w_new = jnp.concatenate([w[:, :m2], w[:, m3:c_wi], w[:, c_qm:], w_ki, w_ki], axis=1).astype(BF16)
            wt_new = jnp.concatenate([w[:, m2:m3], w[:, c_wi:c_ki]], axis=1).T.astype(BF16)
            o_qm = m2 + IDX_HEADS * IDX_DIM
            sections = qk_sections + [(m2, IDX_HEADS * IDX_DIM, "rope", 1.0),
                                      (o_qm, MEM_WIDTH, "plain", SCALE),
                                      (o_qm + MEM_WIDTH, LANES, "ln_rope", 1.0)]
            tsections = [(0, MIX_WIDTH, "tiles", 1.0),
                         (MIX_WIDTH, IDX_HEADS, "flat", float((IDX_HEADS * IDX_DIM) ** -0.5))]
            lng = jnp.tile(idx_kn_g[j], 2).reshape(1, LANES)
            lnb = jnp.tile(idx_kn_b[j], 2).reshape(1, LANES)
            q, k, qi, qm, ki, vt, wit = _inproj(xb, w_new, wt_new, tables, lng, lnb,
                                                sections, [BF16] * 5, tsections, [BF16, F32])
            mix = _dsa(b3(q), b3(k), vt, b3(qi), wit, b3(ki))
        else:
            w = w_in_b[j] if kind == 1 else w_in_c[j]
            w_new = jnp.concatenate([w[:, :m2], w[:, m3:]], axis=1).astype(BF16)
            wt_new = w[:, m2:m3].T.astype(BF16)
            sections = qk_sections + [(m2, MEM_WIDTH, "plain", SCALE)]
            q, k, qm, vt = _inproj(xb, w_new, wt_new, tables, dummy_ln, dummy_ln,
                                   sections, [BF16] * 3, [(0, MIX_WIDTH, "tiles", 1.0)], [BF16])
            if kind == 1:
                mix = _dilated(b3(q), b3(k), vt)
            else:
                k3 = b3(k)
                mix = _moba(b3(q), k3, vt, _kmean(k3))
        xf, xb = _tail(alpha, mix.reshape(t, MIX_WIDTH), qm, mkv_all[i], wo[:MIX_WIDTH].astype(BF16),
                       wo[MIX_WIDTH:].astype(BF16), xf, ln1_g[i], ln1_b[i],
                       w_gate_up[i].astype(BF16), w_down[i].astype(BF16), ln2_g[i], ln2_b[i], seq)
    return xf.reshape(bsz, seq, d)
```

```python
import functools

import jax
import jax.numpy as jnp
import numpy as np
from jax import lax
from jax.experimental import pallas as pl
from jax.experimental.pallas import tpu as pltpu

F32 = jnp.float32
BF16 = jnp.bfloat16
I32 = jnp.int32
I16 = jnp.int16

HEAD_DIM = 64
N_HEADS = 12
MIX_WIDTH = N_HEADS * HEAD_DIM
N_MEM_HEADS = 4
MEM_WIDTH = N_MEM_HEADS * HEAD_DIM
ROPE_DIM = HEAD_DIM // 4
ROPE_HALF = ROPE_DIM // 2
ROPE_THETA = 500000.0
IDX_HEADS = 8
IDX_DIM = 64
TOPK_MAX = 256
DIL_PAIRS = ((128, 1), (512, 4), (2048, 16))
B_GROUP_HEADS = 4
B_V_DIM = MIX_WIDTH // B_GROUP_HEADS
MOBA_BLOCK = 256
MOBA_TOPK = 3
LN_EPS = 1e-5
SCALE = HEAD_DIM ** -0.5
LOG2E = 1.4426950408889634
Q_SCALE = SCALE * LOG2E

LANES = 128
SUBLANES = 8
VMEM_LIMIT_BYTES = 56 * 1024 * 1024

TILE = 256
ROW_TILE = 512
FF_CHUNK = 256
HALF_BITS = 16
LOW_MASK = (1 << HALF_BITS) - 1
SIGN_FLIP = np.int32(0x7FFFFFFF)
ONES_ROWS = 16
HEADS_PER_STEP = 6
NEG = -1e30
INT_MIN = np.int32(-2 ** 31)
I16_MIN = -2 ** 15
PACKED_ROWS = 2 * SUBLANES


def _cparams(n_axes):
    return pltpu.CompilerParams(dimension_semantics=("arbitrary",) * n_axes,
                                vmem_limit_bytes=VMEM_LIMIT_BYTES)


def _layer_norm(y, g, b):
    mu = jnp.mean(y, axis=-1, keepdims=True)
    yc = y - mu
    var = jnp.mean(yc * yc, axis=-1, keepdims=True)
    return yc * lax.rsqrt(var + LN_EPS) * g + b


def _dot_nt(a, b):
    return lax.dot_general(a, b, (((1,), (1,)), ((), ())), preferred_element_type=F32)


def _half_masks(rows):
    lane = lax.broadcasted_iota(I32, (rows, LANES), 1)
    return lane < HEAD_DIM, lane >= HEAD_DIM


def _store_heads(q_ref, qh_ref):
    lo, hi = _half_masks(q_ref.shape[1])
    for head in range(qh_ref.shape[0]):
        qp = q_ref[0, :, _head_cols(head)]
        qh_ref[head] = jnp.where(hi if head % 2 else lo, qp, jnp.zeros_like(qp))


def _head_cols(head):
    return slice((head // 2) * LANES, (head // 2 + 1) * LANES)


def _key_tile(j, ntiles=1):
    return pl.ds(pl.multiple_of(j * TILE, TILE), ntiles * TILE)


def _flash_over_tiles(states, lo, hi, step):
    npairs = (hi - lo) // 2
    states = lax.fori_loop(0, npairs, lambda jj, st: step(st, lo + 2 * jj, 2), states)
    return lax.cond((hi - lo) % 2 == 1, lambda st: step(st, hi - 1, 1), lambda st: st, states)


def _values_t(vt_ref, j, ntiles, rows):
    return jnp.concatenate([vt_ref[j + c, rows, :] for c in range(ntiles)], axis=1)


def _flash_init(dv):
    return (jnp.full((1, TILE), NEG, F32), jnp.zeros((dv + ONES_ROWS, TILE), F32))


def _flash_step(states, qk, biases, vts):
    n = len(states)
    ones = jnp.ones((ONES_ROWS, vts[0].shape[1]), BF16)
    logits = []
    for h in range(n):
        raw = _dot_nt(*qk[h])
        logits.append(jnp.concatenate([raw[c * TILE:(c + 1) * TILE] + b for c, b in enumerate(biases[h])],
                                      axis=0))
    mids = []
    for (m_old, acc), s in zip(states, logits):
        m_new = jnp.maximum(m_old, jnp.max(s, axis=0, keepdims=True))
        p = jnp.exp2(s - m_new).astype(BF16)
        mids.append((m_new, jnp.exp2(m_old - m_new), p))
    out = []
    for (m_new, alpha, p), (_, acc), vt in zip(mids, states, vts):
        pv = jnp.dot(jnp.concatenate([vt, ones], axis=0), p, preferred_element_type=F32)
        out.append((m_new, alpha * acc + pv))
    return tuple(out)


def _flash_finish(o_ref, states, col0):
    out_t = jnp.concatenate([acc[:-ONES_ROWS] * (1.0 / acc[-1:]) for _, acc in states], axis=0)
    for c in range(out_t.shape[0] // LANES):
        o_ref[0, :, col0 + c * LANES:col0 + (c + 1) * LANES] = (
            out_t[c * LANES:(c + 1) * LANES, :].T.astype(o_ref.dtype))


def _memkv_kernel(mem_ref, g_ref, b_ref, w_ref, o_ref):
    mn = _layer_norm(mem_ref[0], g_ref[...], b_ref[...])
    o_ref[0, 0] = jnp.dot(mn.astype(BF16), w_ref[0], preferred_element_type=F32).astype(BF16)


def _memkv(mem, g, b, w_bf16):
    depth = w_bf16.shape[0]
    bsz, n_mem, d = mem.shape
    wout = w_bf16.shape[2]
    return pl.pallas_call(
        _memkv_kernel,
        grid=(depth, bsz),
        in_specs=[pl.BlockSpec((1, n_mem, d), lambda i, b_: (b_, 0, 0)),
                  pl.BlockSpec((1, d), lambda i, b_: (0, 0)),
                  pl.BlockSpec((1, d), lambda i, b_: (0, 0)),
                  pl.BlockSpec((1, d, wout), lambda i, b_: (i, 0, 0))],
        out_specs=pl.BlockSpec((1, 1, n_mem, wout), lambda i, b_: (i, b_, 0, 0)),
        out_shape=jax.ShapeDtypeStruct((depth, bsz, n_mem, wout), BF16),
        compiler_params=_cparams(2), name="memkv",
    )(mem, g.reshape(1, d), b.reshape(1, d), w_bf16)


def _rope(h, c, s1, s2):
    return h * c + pltpu.roll(h, ROPE_HALF, 1) * s1 + pltpu.roll(h, LANES - ROPE_HALF, 1) * s2


def _inproj_kernel(sections, tsections, x_ref, w_ref, wt_ref, c_ref, s1_ref, s2_ref, lng_ref, lnb_ref,
                   *out_refs):
    x = x_ref[...].astype(BF16)
    c, s1, s2 = c_ref[...], s1_ref[...], s2_ref[...]
    for (start, width, kind, scale), o_ref in zip(sections, out_refs):
        h = jnp.dot(x, w_ref[:, start:start + width], preferred_element_type=F32)
        if kind == "ln_rope":
            h = _layer_norm(h, lng_ref[...], lnb_ref[...])
        for ch in range(width // LANES):
            hc = h[:, ch * LANES:(ch + 1) * LANES]
            if kind in ("rope", "ln_rope"):
                hc = _rope(hc, c, s1, s2)
            if scale != 1.0:
                hc = hc * scale
            o_ref[:, ch * LANES:(ch + 1) * LANES] = hc.astype(o_ref.dtype)
    for (start, rows, kind, scale), o_ref in zip(tsections, out_refs[len(sections):]):
        ht = _dot_nt(wt_ref[start:start + rows, :], x)
        if scale != 1.0:
            ht = ht * scale
        if kind == "tiles":
            for tl in range(o_ref.shape[0]):
                o_ref[tl] = ht[:, tl * TILE:(tl + 1) * TILE].astype(o_ref.dtype)
        else:
            o_ref[...] = ht.astype(o_ref.dtype)


def _inproj(xb, w_bf16, wt_bf16, tables, lng, lnb, sections, out_dtypes, tsections, tout_dtypes, tm=ROW_TILE):
    t, d = xb.shape
    row = lambda i: (i, 0)
    fixed = lambda i: (0, 0)
    out_shape = [jax.ShapeDtypeStruct((t, sec[1]), dt) for sec, dt in zip(sections, out_dtypes)]
    out_specs = [pl.BlockSpec((tm, sec[1]), row) for sec in sections]
    for (_, rows, kind, _), dt in zip(tsections, tout_dtypes):
        if kind == "tiles":
            out_shape.append(jax.ShapeDtypeStruct((t // TILE, rows, TILE), dt))
            out_specs.append(pl.BlockSpec((tm // TILE, rows, TILE), lambda i: (i, 0, 0)))
        else:
            out_shape.append(jax.ShapeDtypeStruct((rows, t), dt))
            out_specs.append(pl.BlockSpec((rows, tm), lambda i: (0, i)))
    return pl.pallas_call(
        functools.partial(_inproj_kernel, tuple(sections), tuple(tsections)),
        grid=(t // tm,),
        in_specs=[pl.BlockSpec((tm, d), row),
                  pl.BlockSpec(w_bf16.shape, fixed),
                  pl.BlockSpec(wt_bf16.shape, fixed),
                  pl.BlockSpec((tm, LANES), row),
                  pl.BlockSpec((tm, LANES), row),
                  pl.BlockSpec((tm, LANES), row),
                  pl.BlockSpec((1, LANES), fixed),
                  pl.BlockSpec((1, LANES), fixed)],
        out_specs=out_specs,
        out_shape=out_shape,
        compiler_params=_cparams(1), name="inproj",
    )(xb, w_bf16, wt_bf16, *tables, lng, lnb)


def _dsa_kernel(topk, q_ref, k_ref, vt_ref, qi_ref, wit_ref, ki_ref, o_ref,
                keys_ref, hi_ref, lo_ref, qh_ref, qih_ref):
    i = pl.program_id(1)
    nkt = i + 1
    krow = lax.broadcasted_iota(I32, (TILE, TILE), 0)
    qcol = lax.broadcasted_iota(I32, (TILE, TILE), 1)
    _store_heads(q_ref, qh_ref)
    _store_heads(qi_ref, qih_ref)

    wt = wit_ref[...]

    last_tile = ki_ref.shape[1] // TILE - 1

    def score_pair(jj, carry):
        tiles = [2 * jj + c for c in range(2)]
        dots = [[_dot_nt(ki_ref[0, _key_tile(jnp.minimum(j, last_tile)), :], qih_ref[h])
                 for h in range(IDX_HEADS)] for j in tiles]
        for j, dj in zip(tiles, dots):
            sc = jnp.zeros((TILE, TILE), F32)
            for h in range(IDX_HEADS):
                sc = sc + wt[h:h + 1, :] * jnp.maximum(dj[h], 0.0)
            bits = lax.bitcast_convert_type(sc, I32)
            key = bits ^ (lax.shift_right_arithmetic(bits, 31) & SIGN_FLIP)
            causal = (j * TILE + krow) <= (i * TILE + qcol)
            key = jnp.where(causal, key, INT_MIN)
            keys_ref[j] = key
            hi_ref[j] = lax.shift_right_arithmetic(key, HALF_BITS).astype(I16)
            lo_ref[j] = ((key & LOW_MASK) + I16_MIN).astype(I16)
        return carry

    lax.fori_loop(0, (nkt + 1) // 2, score_pair, 0)

    def count16(ref, pred):
        def body(jj, acc):
            part = None
            for c in range(2):
                hits = jnp.where(pred(ref[2 * jj + c]), jnp.ones((), BF16), jnp.zeros((), BF16))
                hits = hits.reshape(TILE // PACKED_ROWS, PACKED_ROWS, TILE)
                for r in range(TILE // PACKED_ROWS):
                    part = hits[r] if part is None else part + hits[r]
            return acc + part.astype(F32)
        acc = lax.fori_loop(0, (nkt + 1) // 2, body, jnp.zeros((PACKED_ROWS, TILE), F32))
        return jnp.sum(acc, axis=0, keepdims=True)

    n_walked = (2 * ((nkt + 1) // 2) * TILE).astype(F32)

    def kth16(ref, k_q):
        zero = jnp.zeros((1, TILE), I32)
        c0 = count16(ref, lambda x: x >= zero.astype(I16))
        ok0 = c0 >= k_q
        init = (jnp.where(ok0, zero, zero + I16_MIN), jnp.where(ok0, c0, n_walked), jnp.where(ok0, 0.0, c0))

        def bisect(it, carry):
            t, c_ge, c_gt = carry
            cand = t | lax.shift_left(np.int32(1), HALF_BITS - 2 - it)
            c = count16(ref, lambda x: x >= cand.astype(I16))
            ok = c >= k_q
            return jnp.where(ok, cand, t), jnp.where(ok, c, c_ge), jnp.where(ok, c_gt, c)

        return lax.fori_loop(0, HALF_BITS - 1, bisect, init)

    t_q = i * TILE + lax.broadcasted_iota(I32, (1, TILE), 1)
    k_q = jnp.minimum(topk, t_q + 1).astype(F32)
    thr_hi, _, above_hi = kth16(hi_ref, k_q)
    thr_hi16 = thr_hi.astype(I16)
    k_lo = k_q - above_hi

    def park(j, carry):
        lo_ref[j] = jnp.where(hi_ref[j] == thr_hi16, lo_ref[j], jnp.full((), I16_MIN, I16))
        return carry

    lax.fori_loop(0, nkt, park, 0)
    thr_lo, at_or_above_lo, above_lo = kth16(lo_ref, k_lo)
    thr = lax.shift_left(thr_hi, HALF_BITS) | ((thr_lo - I16_MIN) & LOW_MASK)

    need = k_lo - above_lo
    n_eq = at_or_above_lo - above_lo
    has_tie = jnp.max(jnp.where(n_eq > need, 1.0, 0.0))

    @pl.when(has_tie > 0.0)
    def _():
        tril = jnp.where(qcol <= krow, 1.0, 0.0).astype(BF16)

        def fix(j, seen):
            kt = keys_ref[j]
            eq = kt == thr
            eqf = jnp.where(eq, 1.0, 0.0)
            pref = jnp.dot(tril, eqf.astype(BF16), preferred_element_type=F32) + seen
            keys_ref[j] = jnp.where(eq & (pref > need), INT_MIN, kt)
            return seen + jnp.sum(eqf, axis=0, keepdims=True)

        lax.fori_loop(0, nkt, fix, jnp.zeros((1, TILE), F32))

    for h0 in range(0, N_HEADS, HEADS_PER_STEP):
        heads = range(h0, h0 + HEADS_PER_STEP)

        def step(states, j, ntiles, heads=heads):
            ks = _key_tile(j, ntiles)
            bias = tuple(jnp.where(keys_ref[j + c] >= thr, 0.0, NEG) for c in range(ntiles))
            return _flash_step(states, [(k_ref[0, ks, _head_cols(h)], qh_ref[h]) for h in heads],
                               [bias] * len(heads),
                               [_values_t(vt_ref, j, ntiles, slice(h * HEAD_DIM, (h + 1) * HEAD_DIM))
                                for h in heads])

        states = _flash_over_tiles(tuple(_flash_init(HEAD_DIM) for _ in heads), 0, nkt, step)
        _flash_finish(o_ref, states, h0 * HEAD_DIM)


def _dsa(q, k, vt, qi, wit, ki):
    bsz, s, _ = q.shape
    topk = min(TOPK_MAX, s // 4)
    nt = s // TILE
    qtile = lambda w: pl.BlockSpec((1, TILE, w), lambda b, i: (b, i, 0))
    full = lambda w: pl.BlockSpec((1, s, w), lambda b, i: (b, 0, 0))
    return pl.pallas_call(
        functools.partial(_dsa_kernel, topk),
        grid=(bsz, nt),
        in_specs=[qtile(MIX_WIDTH), full(MIX_WIDTH),
                  pl.BlockSpec((nt, MIX_WIDTH, TILE), lambda b, i: (b, 0, 0)),
                  qtile(IDX_HEADS * IDX_DIM),
                  pl.BlockSpec((IDX_HEADS, TILE), lambda b, i: (0, b * nt + i)),
                  full(LANES)],
        out_specs=qtile(MIX_WIDTH),
        out_shape=jax.ShapeDtypeStruct((bsz, s, MIX_WIDTH), BF16),
        scratch_shapes=[pltpu.VMEM((nt + nt % 2, TILE, TILE), I32),
                        pltpu.VMEM((nt + nt % 2, TILE, TILE), I16),
                        pltpu.VMEM((nt + nt % 2, TILE, TILE), I16),
                        pltpu.VMEM((N_HEADS, TILE, LANES), BF16),
                        pltpu.VMEM((IDX_HEADS, TILE, LANES), BF16)],
        compiler_params=_cparams(2), name="dsa",
    )(q, k, vt, qi, wit, ki)


def _dilated_kernel(q_ref, k_ref, vt_ref, o_ref, qh_ref):
    i = pl.program_id(1)
    rel = (lax.broadcasted_iota(I32, (TILE, TILE), 1)
           - lax.broadcasted_iota(I32, (TILE, TILE), 0))
    _store_heads(q_ref, qh_ref)
    states = tuple(_flash_init(B_V_DIM) for _ in range(B_GROUP_HEADS))
    for g, (window, dilation) in enumerate(DIL_PAIRS):
        heads = [g * B_GROUP_HEADS + hv for hv in range(B_GROUP_HEADS)]
        nback = -(-window // TILE)

        def step(states, j, ntiles, heads=heads, window=window, dilation=dilation):
            ks = _key_tile(j, ntiles)
            bias = []
            for c in range(ntiles):
                dist = rel + (i - j - c) * TILE
                ok = (dist >= 0) & (dist <= window) & ((dist & (dilation - 1)) == 0)
                bias.append(jnp.where(ok, 0.0, NEG))
            return _flash_step(states, [(k_ref[0, ks, _head_cols(h)], qh_ref[h]) for h in heads],
                               [tuple(bias)] * len(heads),
                               [_values_t(vt_ref, j, ntiles, slice(hv * B_V_DIM, (hv + 1) * B_V_DIM))
                                for hv in range(B_GROUP_HEADS)])

        states = _flash_over_tiles(states, jnp.maximum(i - nback, 0), i + 1, step)
    _flash_finish(o_ref, states, 0)


def _dilated(q, k, vt):
    bsz, s, w = q.shape
    nt = s // TILE
    return pl.pallas_call(
        _dilated_kernel,
        grid=(bsz, nt),
        in_specs=[pl.BlockSpec((1, TILE, w), lambda b, i: (b, i, 0)),
                  pl.BlockSpec((1, s, w), lambda b, i: (b, 0, 0)),
                  pl.BlockSpec((nt, w, TILE), lambda b, i: (b, 0, 0))],
        out_specs=pl.BlockSpec((1, TILE, w), lambda b, i: (b, i, 0)),
        out_shape=jax.ShapeDtypeStruct((bsz, s, w), BF16),
        scratch_shapes=[pltpu.VMEM((N_HEADS, TILE, LANES), BF16)],
        compiler_params=_cparams(2), name="dilated",
    )(q, k, vt)


def _kmean_kernel(nblk, k_ref, o_ref):
    o_ref[0] = jnp.zeros(o_ref.shape[1:], o_ref.dtype)
    for n in range(nblk):
        kb = k_ref[0, n * MOBA_BLOCK:(n + 1) * MOBA_BLOCK, :].astype(F32)
        o_ref[0, n:n + 1, :] = jnp.mean(kb, axis=0, keepdims=True).astype(o_ref.dtype)


def _kmean(k):
    bsz, s, w = k.shape
    return pl.pallas_call(
        functools.partial(_kmean_kernel, s // MOBA_BLOCK),
        grid=(bsz,),
        in_specs=[pl.BlockSpec((1, s, w), lambda b: (b, 0, 0))],
        out_specs=pl.BlockSpec((1, LANES, w), lambda b: (b, 0, 0)),
        out_shape=jax.ShapeDtypeStruct((bsz, LANES, w), BF16),
        compiler_params=_cparams(1), name="kmean",
    )(k)


def _moba_kernel(nbp, q_ref, k_ref, vt_ref, km_ref, o_ref, sel_ref, qh_ref):
    i = pl.program_id(1)
    blk = lax.broadcasted_iota(I32, (nbp, TILE), 0)
    blk_f = blk.astype(F32)
    krow = lax.broadcasted_iota(I32, (TILE, TILE), 0)
    qcol = lax.broadcasted_iota(I32, (TILE, TILE), 1)
    causal_bias = jnp.where(krow <= qcol, 0.0, NEG)
    _store_heads(q_ref, qh_ref)

    for h in range(N_HEADS):
        g = jnp.where(blk < i, _dot_nt(km_ref[0, :, _head_cols(h)], qh_ref[h])[:nbp], -jnp.inf)
        sel = jnp.zeros((nbp, TILE), F32)
        for r in range(MOBA_TOPK):
            mx = jnp.max(g, axis=0, keepdims=True)
            idx = jnp.min(jnp.where(g == mx, blk_f, float(nbp)), axis=0, keepdims=True)
            hit = blk_f == idx
            sel = jnp.where(hit & (jnp.full((nbp, TILE), r, I32) < i), 1.0, sel)
            g = jnp.where(hit, -jnp.inf, g)
        sel_ref[h] = sel

    for h0 in range(0, N_HEADS, HEADS_PER_STEP):
        heads = range(h0, h0 + HEADS_PER_STEP)

        def step(states, j, ntiles, bias_of, heads=heads):
            ks = _key_tile(j, ntiles)
            return _flash_step(states, [(k_ref[0, ks, _head_cols(h)], qh_ref[h]) for h in heads],
                               [tuple(bias_of(h, j + c) for c in range(ntiles)) for h in heads],
                               [_values_t(vt_ref, j, ntiles, slice(h * HEAD_DIM, (h + 1) * HEAD_DIM))
                                for h in heads])

        def past(states, n, ntiles, step=step):
            picked = lambda h, blk_n: jnp.where(sel_ref[h, pl.ds(blk_n, 1), :] > 0.5, 0.0, NEG)
            return step(states, n, ntiles, picked)

        states = _flash_over_tiles(tuple(_flash_init(HEAD_DIM) for _ in heads), 0, i, past)
        states = step(states, i, 1, lambda h, blk_n: causal_bias)
        _flash_finish(o_ref, states, h0 * HEAD_DIM)


def _moba(q, k, vt, kmean):
    bsz, s, w = q.shape
    nt = s // TILE
    nbp = -(-nt // SUBLANES) * SUBLANES
    return pl.pallas_call(
        functools.partial(_moba_kernel, nbp),
        grid=(bsz, nt),
        in_specs=[pl.BlockSpec((1, TILE, w), lambda b, i: (b, i, 0)),
                  pl.BlockSpec((1, s, w), lambda b, i: (b, 0, 0)),
                  pl.BlockSpec((nt, w, TILE), lambda b, i: (b, 0, 0)),
                  pl.BlockSpec((1, LANES, w), lambda b, i: (b, 0, 0))],
        out_specs=pl.BlockSpec((1, TILE, w), lambda b, i: (b, i, 0)),
        out_shape=jax.ShapeDtypeStruct((bsz, s, w), BF16),
        scratch_shapes=[pltpu.VMEM((N_HEADS, nbp, TILE), F32),
                        pltpu.VMEM((N_HEADS, TILE, LANES), BF16)],
        compiler_params=_cparams(2), name="moba",
    )(q, k, vt, kmean)


def _tail_kernel(alpha, tf, mix_ref, qm_ref, mkv_ref, wmix_ref, wmem_ref, x_ref, g1_ref, b1_ref,
                 wgu_ref, wd_ref, g2_ref, b2_ref, xo_ref, xb_ref, h_ref):
    rows = mix_ref.shape[0]
    lo, hi = _half_masks(rows)
    mo = []
    for p in range(MEM_WIDTH // LANES):
        qp = qm_ref[:, p * LANES:(p + 1) * LANES]
        mk = mkv_ref[0, :, p * LANES:(p + 1) * LANES]
        mv = mkv_ref[0, :, MEM_WIDTH + p * LANES:MEM_WIDTH + (p + 1) * LANES]
        outs = []
        for half in (lo, hi):
            s = _dot_nt(jnp.where(half, qp, jnp.zeros_like(qp)), mk)
            e = jnp.exp(s - jnp.max(s, axis=1, keepdims=True))
            pv = jnp.dot(e.astype(BF16), mv, preferred_element_type=F32)
            outs.append(pv / jnp.sum(e, axis=1, keepdims=True))
        mo.append(jnp.where(lo, outs[0], outs[1]).astype(BF16))
    mixed = jnp.dot(mix_ref[...], wmix_ref[...], preferred_element_type=F32)
    for p, mo_p in enumerate(mo):
        mixed = mixed + jnp.dot(mo_p, wmem_ref[p * LANES:(p + 1) * LANES, :],
                                preferred_element_type=F32)
    x1 = _layer_norm(alpha * x_ref[...] + mixed, g1_ref[...], b1_ref[...])

    x1b = x1.astype(BF16)
    dff = wd_ref.shape[0]
    for c in range(dff // tf):
        gate = jnp.dot(x1b, wgu_ref[:, c * tf:(c + 1) * tf], preferred_element_type=F32)
        up = jnp.dot(x1b, wgu_ref[:, dff + c * tf:dff + (c + 1) * tf], preferred_element_type=F32)
        h_ref[:, c * tf:(c + 1) * tf] = (gate * jax.nn.sigmoid(gate) * up).astype(BF16)
    y = jnp.dot(h_ref[...], wd_ref[...], preferred_element_type=F32)
    x2 = _layer_norm(alpha * x1 + y, g2_ref[...], b2_ref[...])
    xo_ref[...] = x2
    xb_ref[...] = x2.astype(BF16)


def _tail(alpha, mix, qm, mkv, wmix, wmem, xf, g1, b1, wgu, wd, g2, b2, seq, tm=ROW_TILE, tf=FF_CHUNK):
    t, d = xf.shape
    wm = mix.shape[1]
    dff = wd.shape[0]
    per_batch = seq // tm
    row = lambda i: (i, 0)
    fixed = lambda i: (0, 0)
    resident = lambda shape: pl.BlockSpec(shape, fixed, pipeline_mode=pl.Buffered(1))
    vec = lambda a: a.reshape(1, d)
    return pl.pallas_call(
        functools.partial(_tail_kernel, alpha, tf),
        grid=(t // tm,),
        in_specs=[pl.BlockSpec((tm, wm), row),
                  pl.BlockSpec((tm, MEM_WIDTH), row),
                  pl.BlockSpec((1,) + mkv.shape[1:], lambda i: (i // per_batch, 0, 0)),
                  resident(wmix.shape), resident(wmem.shape),
                  pl.BlockSpec((tm, d), row),
                  resident((1, d)), resident((1, d)),
                  resident(wgu.shape), resident(wd.shape),
                  resident((1, d)), resident((1, d))],
        out_specs=[pl.BlockSpec((tm, d), row), pl.BlockSpec((tm, d), row)],
        out_shape=[jax.ShapeDtypeStruct((t, d), F32), jax.ShapeDtypeStruct((t, d), BF16)],
        scratch_shapes=[pltpu.VMEM((tm, dff), BF16)],
        compiler_params=_cparams(1), name="tail",
    )(mix, qm, mkv, wmix, wmem, xf, vec(g1), vec(b1), wgu, wd, vec(g2), vec(b2))


def _rope_tables(positions):
    inv = ROPE_THETA ** (-jnp.arange(ROPE_HALF, dtype=F32) / ROPE_HALF)
    ang = positions.astype(F32)[..., None] * inv
    cos, sin = jnp.cos(ang), jnp.sin(ang)
    rest = HEAD_DIM - ROPE_DIM
    pad = lambda a, before, after, val: jnp.pad(a, ((0, 0), (0, 0), (before, after)), constant_values=val)
    c = pad(jnp.concatenate([cos, cos], -1), 0, rest, 1.0)
    s1 = pad(sin, ROPE_HALF, rest, 0.0)
    s2 = pad(-sin, 0, ROPE_HALF + rest, 0.0)
    t = positions.shape[0] * positions.shape[1]
    return tuple(jnp.tile(a, (1, 1, LANES // HEAD_DIM)).reshape(t, LANES) for a in (c, s1, s2))


def kernel(x, mem, positions, mem_ln_g, mem_ln_b, w_in_a, idx_kn_g, idx_kn_b, w_in_b, w_in_c,
           w_mem_kv, w_out, ln1_g, ln1_b, w_gate_up, w_down, ln2_g, ln2_b):
    bsz, seq, d = x.shape
    t = bsz * seq
    depth = w_out.shape[0]
    alpha = (2 * depth) ** 0.25
    assert seq % TILE == 0 and TILE == MOBA_BLOCK

    tables = _rope_tables(positions)
    mkv_all = _memkv(mem, mem_ln_g, mem_ln_b, w_mem_kv.astype(BF16))
    xf = x.reshape(t, d)
    xb = xf
    dummy_ln = jnp.zeros((1, LANES), F32)
    m2, m3 = 2 * MIX_WIDTH, 3 * MIX_WIDTH
    b3 = lambda a: a.reshape(bsz, seq, a.shape[-1])
    qk_sections = [(0, MIX_WIDTH, "rope", Q_SCALE), (MIX_WIDTH, MIX_WIDTH, "rope", 1.0)]

    for i in range(depth):
        kind, j = i % 3, i // 3
        wo = w_out[i]
        if kind == 0:
            w = w_in_a[j]
            c_wi = m3 + IDX_HEADS * IDX_DIM
            c_ki, c_qm = c_wi + IDX_HEADS, c_wi + IDX_HEADS + IDX_DIM
            w_ki = w[:, c_ki:c_qm]
            w_new = jnp.concatenate([w[:, :m2], w[:, m3:c_wi], w[:, c_qm:], w_ki, w_ki], axis=1).astype(BF16)
            wt_new = jnp.concatenate([w[:, m2:m3], w[:, c_wi:c_ki]], axis=1).T.astype(BF16)
            o_qm = m2 + IDX_HEADS * IDX_DIM
            sections = qk_sections + [(m2, IDX_HEADS * IDX_DIM, "rope", 1.0),
                                      (o_qm, MEM_WIDTH, "plain", SCALE),
                                      (o_qm + MEM_WIDTH, LANES, "ln_rope", 1.0)]
            tsections = [(0, MIX_WIDTH, "tiles", 1.0),
                         (MIX_WIDTH, IDX_HEADS, "flat", float((IDX_HEADS * IDX_DIM) ** -0.5))]
            lng = jnp.tile(idx_kn_g[j], 2).reshape(1, LANES)
            lnb = jnp.tile(idx_kn_b[j], 2).reshape(1, LANES)
            q, k, qi, qm, ki, vt, wit = _inproj(xb, w_new, wt_new, tables, lng, lnb,
                                                sections, [BF16] * 5, tsections, [BF16, F32])
            mix = _dsa(b3(q), b3(k), vt, b3(qi), wit, b3(ki))
        else:
            w = w_in_b[j] if kind == 1 else w_in_c[j]
            w_new = jnp.concatenate([w[:, :m2], w[:, m3:]], axis=1).astype(BF16)
            wt_new = w[:, m2:m3].T.astype(BF16)
            sections = qk_sections + [(m2, MEM_WIDTH, "plain", SCALE)]
            q, k, qm, vt = _inproj(xb, w_new, wt_new, tables, dummy_ln, dummy_ln,
                                   sections, [BF16] * 3, [(0, MIX_WIDTH, "tiles", 1.0)], [BF16])
            if kind == 1:
                mix = _dilated(b3(q), b3(k), vt)
            else:
                k3 = b3(k)
                mix = _moba(b3(q), k3, vt, _kmean(k3))
        xf, xb = _tail(alpha, mix.reshape(t, MIX_WIDTH), qm, mkv_all[i], wo[:MIX_WIDTH].astype(BF16),
                       wo[MIX_WIDTH:].astype(BF16), xf, ln1_g[i], ln1_b[i],
                       w_gate_up[i].astype(BF16), w_down[i].astype(BF16), ln2_g[i], ln2_b[i], seq)
    return xf.reshape(bsz, seq, d)
```

```python
import functools

import jax
import jax.numpy as jnp
import numpy as np
from jax import lax
from jax.experimental import pallas as pl
from jax.experimental.pallas import tpu as pltpu

F32 = jnp.float32
BF16 = jnp.bfloat16
I32 = jnp.int32
I16 = jnp.int16

HEAD_DIM = 64
N_HEADS = 12
MIX_WIDTH = N_HEADS * HEAD_DIM
N_MEM_HEADS = 4
MEM_WIDTH = N_MEM_HEADS * HEAD_DIM
ROPE_DIM = HEAD_DIM // 4
ROPE_HALF = ROPE_DIM // 2
ROPE_THETA = 500000.0
IDX_HEADS = 8
IDX_DIM = 64
TOPK_MAX = 256
DIL_PAIRS = ((128, 1), (512, 4), (2048, 16))
B_GROUP_HEADS = 4
B_V_DIM = MIX_WIDTH // B_GROUP_HEADS
MOBA_BLOCK = 256
MOBA_TOPK = 3
LN_EPS = 1e-5
SCALE = HEAD_DIM ** -0.5
LOG2E = 1.4426950408889634
Q_SCALE = SCALE * LOG2E

LANES = 128
SUBLANES = 8
VMEM_LIMIT_BYTES = 56 * 1024 * 1024

TILE = 256
ROW_TILE = 512
FF_CHUNK = 256
HALF_BITS = 16
LOW_MASK = (1 << HALF_BITS) - 1
SIGN_FLIP = np.int32(0x7FFFFFFF)
ONES_ROWS = 16
HEADS_PER_STEP = 12
NEG = -1e30
INT_MIN = np.int32(-2 ** 31)
I16_MIN = -2 ** 15
PACKED_ROWS = 2 * SUBLANES


def _cparams(n_axes):
    return pltpu.CompilerParams(dimension_semantics=("arbitrary",) * n_axes,
                                vmem_limit_bytes=VMEM_LIMIT_BYTES)


def _layer_norm(y, g, b):
    mu = jnp.mean(y, axis=-1, keepdims=True)
    yc = y - mu
    var = jnp.mean(yc * yc, axis=-1, keepdims=True)
    return yc * lax.rsqrt(var + LN_EPS) * g + b


def _dot_nt(a, b):
    return lax.dot_general(a, b, (((1,), (1,)), ((), ())), preferred_element_type=F32)


def _half_masks(rows):
    lane = lax.broadcasted_iota(I32, (rows, LANES), 1)
    return lane < HEAD_DIM, lane >= HEAD_DIM


def _store_heads(q_ref, qh_ref):
    lo, hi = _half_masks(q_ref.shape[1])
    for head in range(qh_ref.shape[0]):
        qp = q_ref[0, :, _head_cols(head)]
        qh_ref[head] = jnp.where(hi if head % 2 else lo, qp, jnp.zeros_like(qp))


def _store_heads_t(qt_ref, qh_ref):
    row = lax.broadcasted_iota(I32, (LANES, qt_ref.shape[2]), 0)
    for head in range(qh_ref.shape[0]):
        qp = qt_ref[0, (head // 2) * LANES:(head // 2 + 1) * LANES, :]
        keep = (row >= HEAD_DIM) if head % 2 else (row < HEAD_DIM)
        qh_ref[head] = jnp.where(keep, qp, jnp.zeros_like(qp))


def _head_cols(head):
    return slice((head // 2) * LANES, (head // 2 + 1) * LANES)


def _key_tile(j, ntiles=1):
    return pl.ds(pl.multiple_of(j * TILE, TILE), ntiles * TILE)


def _flash_over_tiles(states, lo, hi, step):
    npairs = (hi - lo) // 2
    states = lax.fori_loop(0, npairs, lambda jj, st: step(st, lo + 2 * jj, 2), states)
    return lax.cond((hi - lo) % 2 == 1, lambda st: step(st, hi - 1, 1), lambda st: st, states)


def _values_t(vt_ref, j, ntiles, rows):
    return jnp.concatenate([vt_ref[j + c, rows, :] for c in range(ntiles)], axis=1)


def _flash_init(dv):
    return (jnp.full((1, TILE), NEG, F32), jnp.zeros((dv + ONES_ROWS, TILE), F32))


def _flash_step(states, qk, biases, vts):
    n = len(states)
    ones = jnp.ones((ONES_ROWS, vts[0].shape[1]), BF16)
    logits = []
    for h in range(n):
        raw = jnp.dot(qk[h][0], qk[h][1], preferred_element_type=F32)
        logits.append(jnp.concatenate([raw[c * TILE:(c + 1) * TILE] + b for c, b in enumerate(biases[h])],
                                      axis=0))
    mids = []
    for (m_old, acc), s in zip(states, logits):
        m_new = jnp.maximum(m_old, jnp.max(s, axis=0, keepdims=True))
        p = jnp.exp2(s - m_new).astype(BF16)
        mids.append((m_new, jnp.exp2(m_old - m_new), p))
    out = []
    for (m_new, alpha, p), (_, acc), vt in zip(mids, states, vts):
        pv = jnp.dot(jnp.concatenate([vt, ones], axis=0), p, preferred_element_type=F32)
        out.append((m_new, alpha * acc + pv))
    return tuple(out)


def _flash_finish(o_ref, states, col0):
    out_t = jnp.concatenate([acc[:-ONES_ROWS] * (1.0 / acc[-1:]) for _, acc in states], axis=0)
    for c in range(out_t.shape[0] // LANES):
        o_ref[0, :, col0 + c * LANES:col0 + (c + 1) * LANES] = (
            out_t[c * LANES:(c + 1) * LANES, :].T.astype(o_ref.dtype))


def _memkv_kernel(mem_ref, g_ref, b_ref, w_ref, o_ref):
    mn = _layer_norm(mem_ref[0], g_ref[...], b_ref[...])
    o_ref[0, 0] = jnp.dot(mn.astype(BF16), w_ref[0], preferred_element_type=F32).astype(BF16)


def _memkv(mem, g, b, w_bf16):
    depth = w_bf16.shape[0]
    bsz, n_mem, d = mem.shape
    wout = w_bf16.shape[2]
    return pl.pallas_call(
        _memkv_kernel,
        grid=(depth, bsz),
        in_specs=[pl.BlockSpec((1, n_mem, d), lambda i, b_: (b_, 0, 0)),
                  pl.BlockSpec((1, d), lambda i, b_: (0, 0)),
                  pl.BlockSpec((1, d), lambda i, b_: (0, 0)),
                  pl.BlockSpec((1, d, wout), lambda i, b_: (i, 0, 0))],
        out_specs=pl.BlockSpec((1, 1, n_mem, wout), lambda i, b_: (i, b_, 0, 0)),
        out_shape=jax.ShapeDtypeStruct((depth, bsz, n_mem, wout), BF16),
        compiler_params=_cparams(2), name="memkv",
    )(mem, g.reshape(1, d), b.reshape(1, d), w_bf16)


def _rope(h, c, s1, s2):
    return h * c + pltpu.roll(h, ROPE_HALF, 1) * s1 + pltpu.roll(h, LANES - ROPE_HALF, 1) * s2


def _rope_t(ht, cos_t, sin_t):
    parts = []
    for base in range(0, ht.shape[0], HEAD_DIM):
        x1, x2 = ht[base:base + ROPE_HALF], ht[base + ROPE_HALF:base + ROPE_DIM]
        parts += [x1 * cos_t - x2 * sin_t, x2 * cos_t + x1 * sin_t, ht[base + ROPE_DIM:base + HEAD_DIM]]
    return jnp.concatenate(parts, axis=0)


def _inproj_kernel(sections, tsections, x_ref, w_ref, wt_ref, c_ref, s1_ref, s2_ref, ct_ref, st_ref,
                   lng_ref, lnb_ref, *out_refs):
    x = x_ref[...].astype(BF16)
    c, s1, s2 = c_ref[...], s1_ref[...], s2_ref[...]
    for (start, width, kind, scale), o_ref in zip(sections, out_refs):
        h = jnp.dot(x, w_ref[:, start:start + width], preferred_element_type=F32)
        if kind == "ln_rope":
            h = _layer_norm(h, lng_ref[...], lnb_ref[...])
        for ch in range(width // LANES):
            hc = h[:, ch * LANES:(ch + 1) * LANES]
            if kind in ("rope", "ln_rope"):
                hc = _rope(hc, c, s1, s2)
            if scale != 1.0:
                hc = hc * scale
            o_ref[:, ch * LANES:(ch + 1) * LANES] = hc.astype(o_ref.dtype)
    for (start, rows, kind, scale), o_ref in zip(tsections, out_refs[len(sections):]):
        ht = _dot_nt(wt_ref[start:start + rows, :], x)
        if kind == "rope_tiles":
            ht = _rope_t(ht, ct_ref[...], st_ref[...])
        if scale != 1.0:
            ht = ht * scale
        if kind in ("tiles", "rope_tiles"):
            for tl in range(o_ref.shape[0]):
                o_ref[tl] = ht[:, tl * TILE:(tl + 1) * TILE].astype(o_ref.dtype)
        else:
            o_ref[...] = ht.astype(o_ref.dtype)


def _inproj(xb, w_bf16, wt_bf16, tables, lng, lnb, sections, out_dtypes, tsections, tout_dtypes, tm=ROW_TILE):
    t, d = xb.shape
    row = lambda i: (i, 0)
    fixed = lambda i: (0, 0)
    out_shape = [jax.ShapeDtypeStruct((t, sec[1]), dt) for sec, dt in zip(sections, out_dtypes)]
    out_specs = [pl.BlockSpec((tm, sec[1]), row) for sec in sections]
    for (_, rows, kind, _), dt in zip(tsections, tout_dtypes):
        if kind in ("tiles", "rope_tiles"):
            out_shape.append(jax.ShapeDtypeStruct((t // TILE, rows, TILE), dt))
            out_specs.append(pl.BlockSpec((tm // TILE, rows, TILE), lambda i: (i, 0, 0)))
        else:
            out_shape.append(jax.ShapeDtypeStruct((rows, t), dt))
            out_specs.append(pl.BlockSpec((rows, tm), lambda i: (0, i)))
    return pl.pallas_call(
        functools.partial(_inproj_kernel, tuple(sections), tuple(tsections)),
        grid=(t // tm,),
        in_specs=[pl.BlockSpec((tm, d), row),
                  pl.BlockSpec(w_bf16.shape, fixed),
                  pl.BlockSpec(wt_bf16.shape, fixed),
                  pl.BlockSpec((tm, LANES), row),
                  pl.BlockSpec((tm, LANES), row),
                  pl.BlockSpec((tm, LANES), row),
                  pl.BlockSpec((ROPE_HALF, tm), lambda i: (0, i)),
                  pl.BlockSpec((ROPE_HALF, tm), lambda i: (0, i)),
                  pl.BlockSpec((1, LANES), fixed),
                  pl.BlockSpec((1, LANES), fixed)],
        out_specs=out_specs,
        out_shape=out_shape,
        compiler_params=_cparams(1), name="inproj",
    )(xb, w_bf16, wt_bf16, *tables, lng, lnb)


def _dsa_kernel(topk, q_ref, k_ref, vt_ref, qi_ref, wit_ref, ki_ref, o_ref,
                keys_ref, hi_ref, lo_ref, qh_ref, qih_ref):
    i = pl.program_id(1)
    nkt = i + 1
    krow = lax.broadcasted_iota(I32, (TILE, TILE), 0)
    qcol = lax.broadcasted_iota(I32, (TILE, TILE), 1)
    _store_heads_t(q_ref, qh_ref)
    _store_heads(qi_ref, qih_ref)

    wt = wit_ref[...]

    last_tile = ki_ref.shape[1] // TILE - 1

    def score_pair(jj, carry):
        tiles = [2 * jj + c for c in range(2)]
        dots = [[_dot_nt(ki_ref[0, _key_tile(jnp.minimum(j, last_tile)), :], qih_ref[h])
                 for h in range(IDX_HEADS)] for j in tiles]
        for j, dj in zip(tiles, dots):
            sc = jnp.zeros((TILE, TILE), F32)
            for h in range(IDX_HEADS):
                sc = sc + wt[h:h + 1, :] * jnp.maximum(dj[h], 0.0)
            bits = lax.bitcast_convert_type(sc, I32)
            key = bits ^ (lax.shift_right_arithmetic(bits, 31) & SIGN_FLIP)
            causal = (j * TILE + krow) <= (i * TILE + qcol)
            key = jnp.where(causal, key, INT_MIN)
            keys_ref[j] = key
            hi_ref[j] = lax.shift_right_arithmetic(key, HALF_BITS).astype(I16)
            lo_ref[j] = ((key & LOW_MASK) + I16_MIN).astype(I16)
        return carry

    lax.fori_loop(0, (nkt + 1) // 2, score_pair, 0)

    def count16(ref, pred):
        def body(jj, acc):
            part = None
            for c in range(2):
                hits = jnp.where(pred(ref[2 * jj + c]), jnp.ones((), BF16), jnp.zeros((), BF16))
                hits = hits.reshape(TILE // PACKED_ROWS, PACKED_ROWS, TILE)
                for r in range(TILE // PACKED_ROWS):
                    part = hits[r] if part is None else part + hits[r]
            return acc + part.astype(F32)
        acc = lax.fori_loop(0, (nkt + 1) // 2, body, jnp.zeros((PACKED_ROWS, TILE), F32))
        return jnp.sum(acc, axis=0, keepdims=True)

    n_walked = (2 * ((nkt + 1) // 2) * TILE).astype(F32)

    def kth16(ref, k_q):
        zero = jnp.zeros((1, TILE), I32)
        c0 = count16(ref, lambda x: x >= zero.astype(I16))
        ok0 = c0 >= k_q
        init = (jnp.where(ok0, zero, zero + I16_MIN), jnp.where(ok0, c0, n_walked), jnp.where(ok0, 0.0, c0))

        def bisect(it, carry):
            t, c_ge, c_gt = carry
            cand = t | lax.shift_left(np.int32(1), HALF_BITS - 2 - it)
            c = count16(ref, lambda x: x >= cand.astype(I16))
            ok = c >= k_q
            return jnp.where(ok, cand, t), jnp.where(ok, c, c_ge), jnp.where(ok, c_gt, c)

        return lax.fori_loop(0, HALF_BITS - 1, bisect, init)

    t_q = i * TILE + lax.broadcasted_iota(I32, (1, TILE), 1)
    k_q = jnp.minimum(topk, t_q + 1).astype(F32)
    thr_hi, _, above_hi = kth16(hi_ref, k_q)
    thr_hi16 = thr_hi.astype(I16)
    k_lo = k_q - above_hi

    def park(j, carry):
        lo_ref[j] = jnp.where(hi_ref[j] == thr_hi16, lo_ref[j], jnp.full((), I16_MIN, I16))
        return carry

    lax.fori_loop(0, nkt, park, 0)
    thr_lo, at_or_above_lo, above_lo = kth16(lo_ref, k_lo)
    thr = lax.shift_left(thr_hi, HALF_BITS) | ((thr_lo - I16_MIN) & LOW_MASK)

    need = k_lo - above_lo
    n_eq = at_or_above_lo - above_lo
    has_tie = jnp.max(jnp.where(n_eq > need, 1.0, 0.0))

    @pl.when(has_tie > 0.0)
    def _():
        tril = jnp.where(qcol <= krow, 1.0, 0.0).astype(BF16)

        def fix(j, seen):
            kt = keys_ref[j]
            eq = kt == thr
            eqf = jnp.where(eq, 1.0, 0.0)
            pref = jnp.dot(tril, eqf.astype(BF16), preferred_element_type=F32) + seen
            keys_ref[j] = jnp.where(eq & (pref > need), INT_MIN, kt)
            return seen + jnp.sum(eqf, axis=0, keepdims=True)

        lax.fori_loop(0, nkt, fix, jnp.zeros((1, TILE), F32))

    for h0 in range(0, N_HEADS, HEADS_PER_STEP):
        heads = range(h0, h0 + HEADS_PER_STEP)

        def step(states, j, ntiles, heads=heads):
            ks = _key_tile(j, ntiles)
            bias = tuple(jnp.where(keys_ref[j + c] >= thr, 0.0, NEG) for c in range(ntiles))
            return _flash_step(states, [(k_ref[0, ks, _head_cols(h)], qh_ref[h]) for h in heads],
                               [bias] * len(heads),
                               [_values_t(vt_ref, j, ntiles, slice(h * HEAD_DIM, (h + 1) * HEAD_DIM))
                                for h in heads])

        states = _flash_over_tiles(tuple(_flash_init(HEAD_DIM) for _ in heads), 0, nkt, step)
        _flash_finish(o_ref, states, h0 * HEAD_DIM)


def _dsa(q, k, vt, qi, wit, ki):
    bsz, s, _ = k.shape
    topk = min(TOPK_MAX, s // 4)
    nt = s // TILE
    qtile = lambda w: pl.BlockSpec((1, TILE, w), lambda b, i: (b, i, 0))
    full = lambda w: pl.BlockSpec((1, s, w), lambda b, i: (b, 0, 0))
    return pl.pallas_call(
        functools.partial(_dsa_kernel, topk),
        grid=(bsz, nt),
        in_specs=[pl.BlockSpec((1, MIX_WIDTH, TILE), lambda b, i: (b * nt + i, 0, 0)), full(MIX_WIDTH),
                  pl.BlockSpec((nt, MIX_WIDTH, TILE), lambda b, i: (b, 0, 0)),
                  qtile(IDX_HEADS * IDX_DIM),
                  pl.BlockSpec((IDX_HEADS, TILE), lambda b, i: (0, b * nt + i)),
                  full(LANES)],
        out_specs=qtile(MIX_WIDTH),
        out_shape=jax.ShapeDtypeStruct((bsz, s, MIX_WIDTH), BF16),
        scratch_shapes=[pltpu.VMEM((nt + nt % 2, TILE, TILE), I32),
                        pltpu.VMEM((nt + nt % 2, TILE, TILE), I16),
                        pltpu.VMEM((nt + nt % 2, TILE, TILE), I16),
                        pltpu.VMEM((N_HEADS, LANES, TILE), BF16),
                        pltpu.VMEM((IDX_HEADS, TILE, LANES), BF16)],
        compiler_params=_cparams(2), name="dsa",
    )(q, k, vt, qi, wit, ki)


def _dilated_kernel(q_ref, k_ref, vt_ref, o_ref, qh_ref):
    i = pl.program_id(1)
    rel = (lax.broadcasted_iota(I32, (TILE, TILE), 1)
           - lax.broadcasted_iota(I32, (TILE, TILE), 0))
    _store_heads_t(q_ref, qh_ref)
    states = tuple(_flash_init(B_V_DIM) for _ in range(B_GROUP_HEADS))
    for g, (window, dilation) in enumerate(DIL_PAIRS):
        heads = [g * B_GROUP_HEADS + hv for hv in range(B_GROUP_HEADS)]
        nback = -(-window // TILE)

        def step(states, j, ntiles, heads=heads, window=window, dilation=dilation):
            ks = _key_tile(j, ntiles)
            bias = []
            for c in range(ntiles):
                dist = rel + (i - j - c) * TILE
                ok = (dist >= 0) & (dist <= window) & ((dist & (dilation - 1)) == 0)
                bias.append(jnp.where(ok, 0.0, NEG))
            return _flash_step(states, [(k_ref[0, ks, _head_cols(h)], qh_ref[h]) for h in heads],
                               [tuple(bias)] * len(heads),
                               [_values_t(vt_ref, j, ntiles, slice(hv * B_V_DIM, (hv + 1) * B_V_DIM))
                                for hv in range(B_GROUP_HEADS)])

        states = _flash_over_tiles(states, jnp.maximum(i - nback, 0), i + 1, step)
    _flash_finish(o_ref, states, 0)


def _dilated(q, k, vt):
    bsz, s, w = k.shape
    nt = s // TILE
    return pl.pallas_call(
        _dilated_kernel,
        grid=(bsz, nt),
        in_specs=[pl.BlockSpec((1, w, TILE), lambda b, i: (b * nt + i, 0, 0)),
                  pl.BlockSpec((1, s, w), lambda b, i: (b, 0, 0)),
                  pl.BlockSpec((nt, w, TILE), lambda b, i: (b, 0, 0))],
        out_specs=pl.BlockSpec((1, TILE, w), lambda b, i: (b, i, 0)),
        out_shape=jax.ShapeDtypeStruct((bsz, s, w), BF16),
        scratch_shapes=[pltpu.VMEM((N_HEADS, LANES, TILE), BF16)],
        compiler_params=_cparams(2), name="dilated",
    )(q, k, vt)


def _kmean_kernel(nblk, k_ref, o_ref):
    o_ref[0] = jnp.zeros(o_ref.shape[1:], o_ref.dtype)
    for n in range(nblk):
        kb = k_ref[0, n * MOBA_BLOCK:(n + 1) * MOBA_BLOCK, :].astype(F32)
        o_ref[0, n:n + 1, :] = jnp.mean(kb, axis=0, keepdims=True).astype(o_ref.dtype)


def _kmean(k):
    bsz, s, w = k.shape
    return pl.pallas_call(
        functools.partial(_kmean_kernel, s // MOBA_BLOCK),
        grid=(bsz,),
        in_specs=[pl.BlockSpec((1, s, w), lambda b: (b, 0, 0))],
        out_specs=pl.BlockSpec((1, LANES, w), lambda b: (b, 0, 0)),
        out_shape=jax.ShapeDtypeStruct((bsz, LANES, w), BF16),
        compiler_params=_cparams(1), name="kmean",
    )(k)


def _moba_kernel(nbp, q_ref, k_ref, vt_ref, km_ref, o_ref, sel_ref, qh_ref):
    i = pl.program_id(1)
    blk = lax.broadcasted_iota(I32, (nbp, TILE), 0)
    blk_f = blk.astype(F32)
    krow = lax.broadcasted_iota(I32, (TILE, TILE), 0)
    qcol = lax.broadcasted_iota(I32, (TILE, TILE), 1)
    causal_bias = jnp.where(krow <= qcol, 0.0, NEG)
    _store_heads_t(q_ref, qh_ref)

    for h in range(N_HEADS):
        gate = jnp.dot(km_ref[0, :, _head_cols(h)], qh_ref[h], preferred_element_type=F32)
        g = jnp.where(blk < i, gate[:nbp], -jnp.inf)
        sel = jnp.zeros((nbp, TILE), F32)
        for r in range(MOBA_TOPK):
            mx = jnp.max(g, axis=0, keepdims=True)
            idx = jnp.min(jnp.where(g == mx, blk_f, float(nbp)), axis=0, keepdims=True)
            hit = blk_f == idx
            sel = jnp.where(hit & (jnp.full((nbp, TILE), r, I32) < i), 1.0, sel)
            g = jnp.where(hit, -jnp.inf, g)
        sel_ref[h] = sel

    for h0 in range(0, N_HEADS, HEADS_PER_STEP):
        heads = range(h0, h0 + HEADS_PER_STEP)

        def step(states, j, ntiles, bias_of, heads=heads):
            ks = _key_tile(j, ntiles)
            return _flash_step(states, [(k_ref[0, ks, _head_cols(h)], qh_ref[h]) for h in heads],
                               [tuple(bias_of(h, j + c) for c in range(ntiles)) for h in heads],
                               [_values_t(vt_ref, j, ntiles, slice(h * HEAD_DIM, (h + 1) * HEAD_DIM))
                                for h in heads])

        def past(states, n, ntiles, step=step):
            picked = lambda h, blk_n: jnp.where(sel_ref[h, pl.ds(blk_n, 1), :] > 0.5, 0.0, NEG)
            return step(states, n, ntiles, picked)

        states = _flash_over_tiles(tuple(_flash_init(HEAD_DIM) for _ in heads), 0, i, past)
        states = step(states, i, 1, lambda h, blk_n: causal_bias)
        _flash_finish(o_ref, states, h0 * HEAD_DIM)


def _moba(q, k, vt, kmean):
    bsz, s, w = k.shape
    nt = s // TILE
    nbp = -(-nt // SUBLANES) * SUBLANES
    return pl.pallas_call(
        functools.partial(_moba_kernel, nbp),
        grid=(bsz, nt),
        in_specs=[pl.BlockSpec((1, w, TILE), lambda b, i: (b * nt + i, 0, 0)),
                  pl.BlockSpec((1, s, w), lambda b, i: (b, 0, 0)),
                  pl.BlockSpec((nt, w, TILE), lambda b, i: (b, 0, 0)),
                  pl.BlockSpec((1, LANES, w), lambda b, i: (b, 0, 0))],
        out_specs=pl.BlockSpec((1, TILE, w), lambda b, i: (b, i, 0)),
        out_shape=jax.ShapeDtypeStruct((bsz, s, w), BF16),
        scratch_shapes=[pltpu.VMEM((N_HEADS, nbp, TILE), F32),
                        pltpu.VMEM((N_HEADS, LANES, TILE), BF16)],
        compiler_params=_cparams(2), name="moba",
    )(q, k, vt, kmean)


def _tail_kernel(alpha, tf, mix_ref, qm_ref, mkv_ref, wmix_ref, wmem_ref, x_ref, g1_ref, b1_ref,
                 wgu_ref, wd_ref, g2_ref, b2_ref, xo_ref, xb_ref, h_ref):
    rows = mix_ref.shape[0]
    lo, hi = _half_masks(rows)
    mo = []
    for p in range(MEM_WIDTH // LANES):
        qp = qm_ref[:, p * LANES:(p + 1) * LANES]
        mk = mkv_ref[0, :, p * LANES:(p + 1) * LANES]
        mv = mkv_ref[0, :, MEM_WIDTH + p * LANES:MEM_WIDTH + (p + 1) * LANES]
        outs = []
        for half in (lo, hi):
            s = _dot_nt(jnp.where(half, qp, jnp.zeros_like(qp)), mk)
            e = jnp.exp(s - jnp.max(s, axis=1, keepdims=True))
            pv = jnp.dot(e.astype(BF16), mv, preferred_element_type=F32)
            outs.append(pv / jnp.sum(e, axis=1, keepdims=True))
        mo.append(jnp.where(lo, outs[0], outs[1]).astype(BF16))
    mixed = jnp.dot(mix_ref[...], wmix_ref[...], preferred_element_type=F32)
    for p, mo_p in enumerate(mo):
        mixed = mixed + jnp.dot(mo_p, wmem_ref[p * LANES:(p + 1) * LANES, :],
                                preferred_element_type=F32)
    x1 = _layer_norm(alpha * x_ref[...] + mixed, g1_ref[...], b1_ref[...])

    x1b = x1.astype(BF16)
    dff = wd_ref.shape[0]
    for c in range(dff // tf):
        gate = jnp.dot(x1b, wgu_ref[:, c * tf:(c + 1) * tf], preferred_element_type=F32)
        up = jnp.dot(x1b, wgu_ref[:, dff + c * tf:dff + (c + 1) * tf], preferred_element_type=F32)
        h_ref[:, c * tf:(c + 1) * tf] = (gate * jax.nn.sigmoid(gate) * up).astype(BF16)
    y = jnp.dot(h_ref[...], wd_ref[...], preferred_element_type=F32)
    x2 = _layer_norm(alpha * x1 + y, g2_ref[...], b2_ref[...])
    xo_ref[...] = x2
    xb_ref[...] = x2.astype(BF16)


def _tail(alpha, mix, qm, mkv, wmix, wmem, xf, g1, b1, wgu, wd, g2, b2, seq, tm=ROW_TILE, tf=FF_CHUNK):
    t, d = xf.shape
    wm = mix.shape[1]
    dff = wd.shape[0]
    per_batch = seq // tm
    row = lambda i: (i, 0)
    fixed = lambda i: (0, 0)
    resident = lambda shape: pl.BlockSpec(shape, fixed, pipeline_mode=pl.Buffered(1))
    vec = lambda a: a.reshape(1, d)
    return pl.pallas_call(
        functools.partial(_tail_kernel, alpha, tf),
        grid=(t // tm,),
        in_specs=[pl.BlockSpec((tm, wm), row),
                  pl.BlockSpec((tm, MEM_WIDTH), row),
                  pl.BlockSpec((1,) + mkv.shape[1:], lambda i: (i // per_batch, 0, 0)),
                  resident(wmix.shape), resident(wmem.shape),
                  pl.BlockSpec((tm, d), row),
                  resident((1, d)), resident((1, d)),
                  resident(wgu.shape), resident(wd.shape),
                  resident((1, d)), resident((1, d))],
        out_specs=[pl.BlockSpec((tm, d), row), pl.BlockSpec((tm, d), row)],
        out_shape=[jax.ShapeDtypeStruct((t, d), F32), jax.ShapeDtypeStruct((t, d), BF16)],
        scratch_shapes=[pltpu.VMEM((tm, dff), BF16)],
        compiler_params=_cparams(1), name="tail",
    )(mix, qm, mkv, wmix, wmem, xf, vec(g1), vec(b1), wgu, wd, vec(g2), vec(b2))


def _rope_tables(positions):
    inv = ROPE_THETA ** (-jnp.arange(ROPE_HALF, dtype=F32) / ROPE_HALF)
    ang = positions.astype(F32)[..., None] * inv
    cos, sin = jnp.cos(ang), jnp.sin(ang)
    rest = HEAD_DIM - ROPE_DIM
    pad = lambda a, before, after, val: jnp.pad(a, ((0, 0), (0, 0), (before, after)), constant_values=val)
    c = pad(jnp.concatenate([cos, cos], -1), 0, rest, 1.0)
    s1 = pad(sin, ROPE_HALF, rest, 0.0)
    s2 = pad(-sin, 0, ROPE_HALF + rest, 0.0)
    t = positions.shape[0] * positions.shape[1]
    token_major = tuple(jnp.tile(a, (1, 1, LANES // HEAD_DIM)).reshape(t, LANES) for a in (c, s1, s2))
    return token_major + (cos.reshape(t, ROPE_HALF).T, sin.reshape(t, ROPE_HALF).T)


def kernel(x, mem, positions, mem_ln_g, mem_ln_b, w_in_a, idx_kn_g, idx_kn_b, w_in_b, w_in_c,
           w_mem_kv, w_out, ln1_g, ln1_b, w_gate_up, w_down, ln2_g, ln2_b):
    bsz, seq, d = x.shape
    t = bsz * seq
    depth = w_out.shape[0]
    alpha = (2 * depth) ** 0.25
    assert seq % TILE == 0 and TILE == MOBA_BLOCK

    tables = _rope_tables(positions)
    mkv_all = _memkv(mem, mem_ln_g, mem_ln_b, w_mem_kv.astype(BF16))
    xf = x.reshape(t, d)
    xb = xf
    dummy_ln = jnp.zeros((1, LANES), F32)
    m2, m3 = 2 * MIX_WIDTH, 3 * MIX_WIDTH
    b3 = lambda a: a.reshape(bsz, seq, a.shape[-1])
    k_section = [(0, MIX_WIDTH, "rope", 1.0)]
    qv_tsections = [(0, MIX_WIDTH, "rope_tiles", Q_SCALE), (MIX_WIDTH, MIX_WIDTH, "tiles", 1.0)]
    m1 = MIX_WIDTH

    for i in range(depth):
        kind, j = i % 3, i // 3
        wo = w_out[i]
        if kind == 0:
            w = w_in_a[j]
            c_wi = m3 + IDX_HEADS * IDX_DIM
            c_ki, c_qm = c_wi + IDX_HEADS, c_wi + IDX_HEADS + IDX_DIM
            w_ki = w[:, c_ki:c_qm]
            w_new = jnp.concatenate([w[:, m1:m2], w[:, m3:c_wi], w[:, c_qm:], w_ki, w_ki], axis=1).astype(BF16)
            wt_new = jnp.concatenate([w[:, :m1], w[:, m2:m3], w[:, c_wi:c_ki]], axis=1).T.astype(BF16)
            o_qm = m1 + IDX_HEADS * IDX_DIM
            sections = k_section + [(m1, IDX_HEADS * IDX_DIM, "rope", 1.0),
                                    (o_qm, MEM_WIDTH, "plain", SCALE),
                                    (o_qm + MEM_WIDTH, LANES, "ln_rope", 1.0)]
            tsections = qv_tsections + [(m2, IDX_HEADS, "flat", float((IDX_HEADS * IDX_DIM) ** -0.5))]
            lng = jnp.tile(idx_kn_g[j], 2).reshape(1, LANES)
            lnb = jnp.tile(idx_kn_b[j], 2).reshape(1, LANES)
            k, qi, qm, ki, qt, vt, wit = _inproj(xb, w_new, wt_new, tables, lng, lnb,
                                                 sections, [BF16] * 4, tsections, [BF16, BF16, F32])
            mix = _dsa(qt, b3(k), vt, b3(qi), wit, b3(ki))
        else:
            w = w_in_b[j] if kind == 1 else w_in_c[j]
            w_new = jnp.concatenate([w[:, m1:m2], w[:, m3:]], axis=1).astype(BF16)
            wt_new = jnp.concatenate([w[:, :m1], w[:, m2:m3]], axis=1).T.astype(BF16)
            sections = k_section + [(m1, MEM_WIDTH, "plain", SCALE)]
            k, qm, qt, vt = _inproj(xb, w_new, wt_new, tables, dummy_ln, dummy_ln,
                                    sections, [BF16] * 2, qv_tsections, [BF16, BF16])
            if kind == 1:
                mix = _dilated(qt, b3(k), vt)
            else:
                k3 = b3(k)
                mix = _moba(qt, k3, vt, _kmean(k3))
        xf, xb = _tail(alpha, mix.reshape(t, MIX_WIDTH), qm, mkv_all[i], wo[:MIX_WIDTH].astype(BF16),
                       wo[MIX_WIDTH:].astype(BF16), xf, ln1_g[i], ln1_b[i],
                       w_gate_up[i].astype(BF16), w_down[i].astype(BF16), ln2_g[i], ln2_b[i], seq)
    return xf.reshape(bsz, seq, d)
```

```python
import functools

import jax
import jax.numpy as jnp
import numpy as np
from jax import lax
from jax.experimental import pallas as pl
from jax.experimental.pallas import tpu as pltpu

F32 = jnp.float32
BF16 = jnp.bfloat16
I32 = jnp.int32
I16 = jnp.int16

HEAD_DIM = 64
N_HEADS = 12
MIX_WIDTH = N_HEADS * HEAD_DIM
N_MEM_HEADS = 4
MEM_WIDTH = N_MEM_HEADS * HEAD_DIM
ROPE_DIM = HEAD_DIM // 4
ROPE_HALF = ROPE_DIM // 2
ROPE_THETA = 500000.0
IDX_HEADS = 8
IDX_DIM = 64
TOPK_MAX = 256
DIL_PAIRS = ((128, 1), (512, 4), (2048, 16))
B_GROUP_HEADS = 4
B_V_DIM = MIX_WIDTH // B_GROUP_HEADS
MOBA_BLOCK = 256
MOBA_TOPK = 3
LN_EPS = 1e-5
SCALE = HEAD_DIM ** -0.5
LOG2E = 1.4426950408889634
Q_SCALE = SCALE * LOG2E

LANES = 128
SUBLANES = 8
VMEM_LIMIT_BYTES = 56 * 1024 * 1024

TILE = 256
ROW_TILE = 512
FF_CHUNK = 256
HALF_BITS = 16
LOW_MASK = (1 << HALF_BITS) - 1
SIGN_FLIP = np.int32(0x7FFFFFFF)
ONES_ROWS = 16
HEADS_PER_STEP = 12
NEG = -1e30
INT_MIN = np.int32(-2 ** 31)
I16_MIN = -2 ** 15
PACKED_ROWS = 2 * SUBLANES


def _cparams(n_axes):
    return pltpu.CompilerParams(dimension_semantics=("arbitrary",) * n_axes,
                                vmem_limit_bytes=VMEM_LIMIT_BYTES)


def _layer_norm(y, g, b):
    mu = jnp.mean(y, axis=-1, keepdims=True)
    yc = y - mu
    var = jnp.mean(yc * yc, axis=-1, keepdims=True)
    return yc * lax.rsqrt(var + LN_EPS) * g + b


def _dot_nt(a, b):
    return lax.dot_general(a, b, (((1,), (1,)), ((), ())), preferred_element_type=F32)


def _half_masks(rows):
    lane = lax.broadcasted_iota(I32, (rows, LANES), 1)
    return lane < HEAD_DIM, lane >= HEAD_DIM


def _store_heads(q_ref, qh_ref):
    lo, hi = _half_masks(q_ref.shape[1])
    for head in range(qh_ref.shape[0]):
        qp = q_ref[0, :, _head_cols(head)]
        qh_ref[head] = jnp.where(hi if head % 2 else lo, qp, jnp.zeros_like(qp))


def _store_heads_t(qt_ref, qh_ref):
    row = lax.broadcasted_iota(I32, (LANES, qt_ref.shape[2]), 0)
    for head in range(qh_ref.shape[0]):
        qp = qt_ref[0, (head // 2) * LANES:(head // 2 + 1) * LANES, :]
        keep = (row >= HEAD_DIM) if head % 2 else (row < HEAD_DIM)
        qh_ref[head] = jnp.where(keep, qp, jnp.zeros_like(qp))


def _head_cols(head):
    return slice((head // 2) * LANES, (head // 2 + 1) * LANES)


def _key_tile(j, ntiles=1):
    return pl.ds(pl.multiple_of(j * TILE, TILE), ntiles * TILE)


def _flash_over_tiles(states, lo, hi, step):
    npairs = (hi - lo) // 2
    states = lax.fori_loop(0, npairs, lambda jj, st: step(st, lo + 2 * jj, 2), states)
    return lax.cond((hi - lo) % 2 == 1, lambda st: step(st, hi - 1, 1), lambda st: st, states)


def _values_t(vt_ref, j, ntiles, rows):
    return jnp.concatenate([vt_ref[j + c, rows, :] for c in range(ntiles)], axis=1)


def _flash_init(dv):
    return (jnp.full((1, TILE), NEG, F32), jnp.zeros((dv + ONES_ROWS, TILE), F32))


def _flash_step(states, qk, biases, vts):
    n = len(states)
    ones = jnp.ones((ONES_ROWS, vts[0].shape[1]), BF16)
    logits = []
    for h in range(n):
        raw = jnp.dot(qk[h][0], qk[h][1], preferred_element_type=F32)
        logits.append(jnp.concatenate([raw[c * TILE:(c + 1) * TILE] + b for c, b in enumerate(biases[h])],
                                      axis=0))
    mids = []
    for (m_old, acc), s in zip(states, logits):
        m_new = jnp.maximum(m_old, jnp.max(s, axis=0, keepdims=True))
        p = jnp.exp2(s - m_new).astype(BF16)
        mids.append((m_new, jnp.exp2(m_old - m_new), p))
    out = []
    for (m_new, alpha, p), (_, acc), vt in zip(mids, states, vts):
        pv = jnp.dot(jnp.concatenate([vt, ones], axis=0), p, preferred_element_type=F32)
        out.append((m_new, alpha * acc + pv))
    return tuple(out)


def _flash_finish(o_ref, states, col0):
    out_t = jnp.concatenate([acc[:-ONES_ROWS] * (1.0 / acc[-1:]) for _, acc in states], axis=0)
    for c in range(out_t.shape[0] // LANES):
        o_ref[0, :, col0 + c * LANES:col0 + (c + 1) * LANES] = (
            out_t[c * LANES:(c + 1) * LANES, :].T.astype(o_ref.dtype))


def _memkv_kernel(mem_ref, g_ref, b_ref, w_ref, o_ref):
    mn = _layer_norm(mem_ref[0], g_ref[...], b_ref[...])
    o_ref[0, 0] = jnp.dot(mn.astype(BF16), w_ref[0], preferred_element_type=F32).astype(BF16)


def _memkv(mem, g, b, w_bf16):
    depth = w_bf16.shape[0]
    bsz, n_mem, d = mem.shape
    wout = w_bf16.shape[2]
    return pl.pallas_call(
        _memkv_kernel,
        grid=(depth, bsz),
        in_specs=[pl.BlockSpec((1, n_mem, d), lambda i, b_: (b_, 0, 0)),
                  pl.BlockSpec((1, d), lambda i, b_: (0, 0)),
                  pl.BlockSpec((1, d), lambda i, b_: (0, 0)),
                  pl.BlockSpec((1, d, wout), lambda i, b_: (i, 0, 0))],
        out_specs=pl.BlockSpec((1, 1, n_mem, wout), lambda i, b_: (i, b_, 0, 0)),
        out_shape=jax.ShapeDtypeStruct((depth, bsz, n_mem, wout), BF16),
        compiler_params=_cparams(2), name="memkv",
    )(mem, g.reshape(1, d), b.reshape(1, d), w_bf16)


def _rope(h, c, s1, s2):
    return h * c + pltpu.roll(h, ROPE_HALF, 1) * s1 + pltpu.roll(h, LANES - ROPE_HALF, 1) * s2


def _rope_t(ht, cos_t, sin_t):
    parts = []
    for base in range(0, ht.shape[0], HEAD_DIM):
        x1, x2 = ht[base:base + ROPE_HALF], ht[base + ROPE_HALF:base + ROPE_DIM]
        parts += [x1 * cos_t - x2 * sin_t, x2 * cos_t + x1 * sin_t, ht[base + ROPE_DIM:base + HEAD_DIM]]
    return jnp.concatenate(parts, axis=0)


def _inproj_kernel(sections, tsections, x_ref, w_ref, wt_ref, c_ref, s1_ref, s2_ref, ct_ref, st_ref,
                   lng_ref, lnb_ref, *out_refs):
    x = x_ref[...].astype(BF16)
    c, s1, s2 = c_ref[...], s1_ref[...], s2_ref[...]
    for (start, width, kind, scale), o_ref in zip(sections, out_refs):
        h = jnp.dot(x, w_ref[:, start:start + width], preferred_element_type=F32)
        if kind == "ln_rope":
            h = _layer_norm(h, lng_ref[...], lnb_ref[...])
        for ch in range(width // LANES):
            hc = h[:, ch * LANES:(ch + 1) * LANES]
            if kind in ("rope", "ln_rope"):
                hc = _rope(hc, c, s1, s2)
            if scale != 1.0:
                hc = hc * scale
            o_ref[:, ch * LANES:(ch + 1) * LANES] = hc.astype(o_ref.dtype)
    for (start, rows, kind, scale), o_ref in zip(tsections, out_refs[len(sections):]):
        ht = _dot_nt(wt_ref[start:start + rows, :], x)
        if kind == "rope_tiles":
            ht = _rope_t(ht, ct_ref[...], st_ref[...])
        if scale != 1.0:
            ht = ht * scale
        if kind in ("tiles", "rope_tiles"):
            for tl in range(o_ref.shape[0]):
                o_ref[tl] = ht[:, tl * TILE:(tl + 1) * TILE].astype(o_ref.dtype)
        else:
            o_ref[...] = ht.astype(o_ref.dtype)


def _inproj(xb, w_bf16, wt_bf16, tables, lng, lnb, sections, out_dtypes, tsections, tout_dtypes, tm=ROW_TILE):
    t, d = xb.shape
    row = lambda i: (i, 0)
    fixed = lambda i: (0, 0)
    out_shape = [jax.ShapeDtypeStruct((t, sec[1]), dt) for sec, dt in zip(sections, out_dtypes)]
    out_specs = [pl.BlockSpec((tm, sec[1]), row) for sec in sections]
    for (_, rows, kind, _), dt in zip(tsections, tout_dtypes):
        if kind in ("tiles", "rope_tiles"):
            out_shape.append(jax.ShapeDtypeStruct((t // TILE, rows, TILE), dt))
            out_specs.append(pl.BlockSpec((tm // TILE, rows, TILE), lambda i: (i, 0, 0)))
        else:
            out_shape.append(jax.ShapeDtypeStruct((rows, t), dt))
            out_specs.append(pl.BlockSpec((rows, tm), lambda i: (0, i)))
    return pl.pallas_call(
        functools.partial(_inproj_kernel, tuple(sections), tuple(tsections)),
        grid=(t // tm,),
        in_specs=[pl.BlockSpec((tm, d), row),
                  pl.BlockSpec(w_bf16.shape, fixed),
                  pl.BlockSpec(wt_bf16.shape, fixed),
                  pl.BlockSpec((tm, LANES), row),
                  pl.BlockSpec((tm, LANES), row),
                  pl.BlockSpec((tm, LANES), row),
                  pl.BlockSpec((ROPE_HALF, tm), lambda i: (0, i)),
                  pl.BlockSpec((ROPE_HALF, tm), lambda i: (0, i)),
                  pl.BlockSpec((1, LANES), fixed),
                  pl.BlockSpec((1, LANES), fixed)],
        out_specs=out_specs,
        out_shape=out_shape,
        compiler_params=_cparams(1), name="inproj",
    )(xb, w_bf16, wt_bf16, *tables, lng, lnb)


def _dsa_kernel(topk, q_ref, k_ref, vt_ref, qi_ref, wit_ref, ki_ref, o_ref,
                keys_ref, hi_ref, lo_ref, qh_ref, qih_ref):
    i = pl.program_id(1)
    nkt = i + 1
    krow = lax.broadcasted_iota(I32, (TILE, TILE), 0)
    qcol = lax.broadcasted_iota(I32, (TILE, TILE), 1)
    _store_heads_t(q_ref, qh_ref)
    _store_heads_t(qi_ref, qih_ref)

    wt = wit_ref[...]

    last_tile = ki_ref.shape[1] // TILE - 1

    def score_pair(jj, carry):
        tiles = [2 * jj + c for c in range(2)]
        dots = [[jnp.dot(ki_ref[0, _key_tile(jnp.minimum(j, last_tile)), :], qih_ref[h], preferred_element_type=F32)
                 for h in range(IDX_HEADS)] for j in tiles]
        for j, dj in zip(tiles, dots):
            sc = jnp.zeros((TILE, TILE), F32)
            for h in range(IDX_HEADS):
                sc = sc + wt[h:h + 1, :] * jnp.maximum(dj[h], 0.0)
            bits = lax.bitcast_convert_type(sc, I32)
            key = bits ^ (lax.shift_right_arithmetic(bits, 31) & SIGN_FLIP)
            causal = (j * TILE + krow) <= (i * TILE + qcol)
            key = jnp.where(causal, key, INT_MIN)
            keys_ref[j] = key
            hi_ref[j] = lax.shift_right_arithmetic(key, HALF_BITS).astype(I16)
            lo_ref[j] = ((key & LOW_MASK) + I16_MIN).astype(I16)
        return carry

    lax.fori_loop(0, (nkt + 1) // 2, score_pair, 0)

    def count16(ref, pred):
        def body(jj, acc):
            part = None
            for c in range(2):
                hits = jnp.where(pred(ref[2 * jj + c]), jnp.ones((), BF16), jnp.zeros((), BF16))
                hits = hits.reshape(TILE // PACKED_ROWS, PACKED_ROWS, TILE)
                for r in range(TILE // PACKED_ROWS):
                    part = hits[r] if part is None else part + hits[r]
            return acc + part.astype(F32)
        acc = lax.fori_loop(0, (nkt + 1) // 2, body, jnp.zeros((PACKED_ROWS, TILE), F32))
        return jnp.sum(acc, axis=0, keepdims=True)

    n_walked = (2 * ((nkt + 1) // 2) * TILE).astype(F32)

    def kth16(ref, k_q):
        zero = jnp.zeros((1, TILE), I32)
        c0 = count16(ref, lambda x: x >= zero.astype(I16))
        ok0 = c0 >= k_q
        init = (jnp.where(ok0, zero, zero + I16_MIN), jnp.where(ok0, c0, n_walked), jnp.where(ok0, 0.0, c0))

        def bisect(it, carry):
            t, c_ge, c_gt = carry
            cand = t | lax.shift_left(np.int32(1), HALF_BITS - 2 - it)
            c = count16(ref, lambda x: x >= cand.astype(I16))
            ok = c >= k_q
            return jnp.where(ok, cand, t), jnp.where(ok, c, c_ge), jnp.where(ok, c_gt, c)

        return lax.fori_loop(0, HALF_BITS - 1, bisect, init)

    t_q = i * TILE + lax.broadcasted_iota(I32, (1, TILE), 1)
    k_q = jnp.minimum(topk, t_q + 1).astype(F32)
    thr_hi, _, above_hi = kth16(hi_ref, k_q)
    thr_hi16 = thr_hi.astype(I16)
    k_lo = k_q - above_hi

    def park(j, carry):
        lo_ref[j] = jnp.where(hi_ref[j] == thr_hi16, lo_ref[j], jnp.full((), I16_MIN, I16))
        return carry

    lax.fori_loop(0, nkt, park, 0)
    thr_lo, at_or_above_lo, above_lo = kth16(lo_ref, k_lo)
    thr = lax.shift_left(thr_hi, HALF_BITS) | ((thr_lo - I16_MIN) & LOW_MASK)

    need = k_lo - above_lo
    n_eq = at_or_above_lo - above_lo
    has_tie = jnp.max(jnp.where(n_eq > need, 1.0, 0.0))

    @pl.when(has_tie > 0.0)
    def _():
        tril = jnp.where(qcol <= krow, 1.0, 0.0).astype(BF16)

        def fix(j, seen):
            kt = keys_ref[j]
            eq = kt == thr
            eqf = jnp.where(eq, 1.0, 0.0)
            pref = jnp.dot(tril, eqf.astype(BF16), preferred_element_type=F32) + seen
            keys_ref[j] = jnp.where(eq & (pref > need), INT_MIN, kt)
            return seen + jnp.sum(eqf, axis=0, keepdims=True)

        lax.fori_loop(0, nkt, fix, jnp.zeros((1, TILE), F32))

    for h0 in range(0, N_HEADS, HEADS_PER_STEP):
        heads = range(h0, h0 + HEADS_PER_STEP)

        def step(states, j, ntiles, heads=heads):
            ks = _key_tile(j, ntiles)
            bias = tuple(jnp.where(keys_ref[j + c] >= thr, 0.0, NEG) for c in range(ntiles))
            return _flash_step(states, [(k_ref[0, ks, _head_cols(h)], qh_ref[h]) for h in heads],
                               [bias] * len(heads),
                               [_values_t(vt_ref, j, ntiles, slice(h * HEAD_DIM, (h + 1) * HEAD_DIM))
                                for h in heads])

        states = _flash_over_tiles(tuple(_flash_init(HEAD_DIM) for _ in heads), 0, nkt, step)
        _flash_finish(o_ref, states, h0 * HEAD_DIM)


def _dsa(q, k, vt, qi, wit, ki):
    bsz, s, _ = k.shape
    topk = min(TOPK_MAX, s // 4)
    nt = s // TILE
    qtile = lambda w: pl.BlockSpec((1, TILE, w), lambda b, i: (b, i, 0))
    full = lambda w: pl.BlockSpec((1, s, w), lambda b, i: (b, 0, 0))
    return pl.pallas_call(
        functools.partial(_dsa_kernel, topk),
        grid=(bsz, nt),
        in_specs=[pl.BlockSpec((1, MIX_WIDTH, TILE), lambda b, i: (b * nt + i, 0, 0)), full(MIX_WIDTH),
                  pl.BlockSpec((nt, MIX_WIDTH, TILE), lambda b, i: (b, 0, 0)),
                  pl.BlockSpec((1, IDX_HEADS * IDX_DIM, TILE), lambda b, i: (b * nt + i, 0, 0)),
                  pl.BlockSpec((IDX_HEADS, TILE), lambda b, i: (0, b * nt + i)),
                  full(LANES)],
        out_specs=qtile(MIX_WIDTH),
        out_shape=jax.ShapeDtypeStruct((bsz, s, MIX_WIDTH), BF16),
        scratch_shapes=[pltpu.VMEM((nt + nt % 2, TILE, TILE), I32),
                        pltpu.VMEM((nt + nt % 2, TILE, TILE), I16),
                        pltpu.VMEM((nt + nt % 2, TILE, TILE), I16),
                        pltpu.VMEM((N_HEADS, LANES, TILE), BF16),
                        pltpu.VMEM((IDX_HEADS, LANES, TILE), BF16)],
        compiler_params=_cparams(2), name="dsa",
    )(q, k, vt, qi, wit, ki)


def _dilated_kernel(q_ref, k_ref, vt_ref, o_ref, qh_ref):
    i = pl.program_id(1)
    rel = (lax.broadcasted_iota(I32, (TILE, TILE), 1)
           - lax.broadcasted_iota(I32, (TILE, TILE), 0))
    _store_heads_t(q_ref, qh_ref)
    states = tuple(_flash_init(B_V_DIM) for _ in range(B_GROUP_HEADS))
    for g, (window, dilation) in enumerate(DIL_PAIRS):
        heads = [g * B_GROUP_HEADS + hv for hv in range(B_GROUP_HEADS)]
        nback = -(-window // TILE)

        def step(states, j, ntiles, heads=heads, window=window, dilation=dilation):
            ks = _key_tile(j, ntiles)
            bias = []
            for c in range(ntiles):
                dist = rel + (i - j - c) * TILE
                ok = (dist >= 0) & (dist <= window) & ((dist & (dilation - 1)) == 0)
                bias.append(jnp.where(ok, 0.0, NEG))
            return _flash_step(states, [(k_ref[0, ks, _head_cols(h)], qh_ref[h]) for h in heads],
                               [tuple(bias)] * len(heads),
                               [_values_t(vt_ref, j, ntiles, slice(hv * B_V_DIM, (hv + 1) * B_V_DIM))
                                for hv in range(B_GROUP_HEADS)])

        states = _flash_over_tiles(states, jnp.maximum(i - nback, 0), i + 1, step)
    _flash_finish(o_ref, states, 0)


def _dilated(q, k, vt):
    bsz, s, w = k.shape
    nt = s // TILE
    return pl.pallas_call(
        _dilated_kernel,
        grid=(bsz, nt),
        in_specs=[pl.BlockSpec((1, w, TILE), lambda b, i: (b * nt + i, 0, 0)),
                  pl.BlockSpec((1, s, w), lambda b, i: (b, 0, 0)),
                  pl.BlockSpec((nt, w, TILE), lambda b, i: (b, 0, 0))],
        out_specs=pl.BlockSpec((1, TILE, w), lambda b, i: (b, i, 0)),
        out_shape=jax.ShapeDtypeStruct((bsz, s, w), BF16),
        scratch_shapes=[pltpu.VMEM((N_HEADS, LANES, TILE), BF16)],
        compiler_params=_cparams(2), name="dilated",
    )(q, k, vt)


def _kmean_kernel(nblk, k_ref, o_ref):
    o_ref[0] = jnp.zeros(o_ref.shape[1:], o_ref.dtype)
    for n in range(nblk):
        kb = k_ref[0, n * MOBA_BLOCK:(n + 1) * MOBA_BLOCK, :].astype(F32)
        o_ref[0, n:n + 1, :] = jnp.mean(kb, axis=0, keepdims=True).astype(o_ref.dtype)


def _kmean(k):
    bsz, s, w = k.shape
    return pl.pallas_call(
        functools.partial(_kmean_kernel, s // MOBA_BLOCK),
        grid=(bsz,),
        in_specs=[pl.BlockSpec((1, s, w), lambda b: (b, 0, 0))],
        out_specs=pl.BlockSpec((1, LANES, w), lambda b: (b, 0, 0)),
        out_shape=jax.ShapeDtypeStruct((bsz, LANES, w), BF16),
        compiler_params=_cparams(1), name="kmean",
    )(k)


def _moba_kernel(nbp, q_ref, k_ref, vt_ref, km_ref, o_ref, sel_ref, qh_ref):
    i = pl.program_id(1)
    blk = lax.broadcasted_iota(I32, (nbp, TILE), 0)
    blk_f = blk.astype(F32)
    krow = lax.broadcasted_iota(I32, (TILE, TILE), 0)
    qcol = lax.broadcasted_iota(I32, (TILE, TILE), 1)
    causal_bias = jnp.where(krow <= qcol, 0.0, NEG)
    _store_heads_t(q_ref, qh_ref)

    for h in range(N_HEADS):
        gate = jnp.dot(km_ref[0, :, _head_cols(h)], qh_ref[h], preferred_element_type=F32)
        g = jnp.where(blk < i, gate[:nbp], -jnp.inf)
        sel = jnp.zeros((nbp, TILE), F32)
        for r in range(MOBA_TOPK):
            mx = jnp.max(g, axis=0, keepdims=True)
            idx = jnp.min(jnp.where(g == mx, blk_f, float(nbp)), axis=0, keepdims=True)
            hit = blk_f == idx
            sel = jnp.where(hit & (jnp.full((nbp, TILE), r, I32) < i), 1.0, sel)
            g = jnp.where(hit, -jnp.inf, g)
        sel_ref[h] = sel

    for h0 in range(0, N_HEADS, HEADS_PER_STEP):
        heads = range(h0, h0 + HEADS_PER_STEP)

        def step(states, j, ntiles, bias_of, heads=heads):
            ks = _key_tile(j, ntiles)
            return _flash_step(states, [(k_ref[0, ks, _head_cols(h)], qh_ref[h]) for h in heads],
                               [tuple(bias_of(h, j + c) for c in range(ntiles)) for h in heads],
                               [_values_t(vt_ref, j, ntiles, slice(h * HEAD_DIM, (h + 1) * HEAD_DIM))
                                for h in heads])

        def past(states, n, ntiles, step=step):
            picked = lambda h, blk_n: jnp.where(sel_ref[h, pl.ds(blk_n, 1), :] > 0.5, 0.0, NEG)
            return step(states, n, ntiles, picked)

        states = _flash_over_tiles(tuple(_flash_init(HEAD_DIM) for _ in heads), 0, i, past)
        states = step(states, i, 1, lambda h, blk_n: causal_bias)
        _flash_finish(o_ref, states, h0 * HEAD_DIM)


def _moba(q, k, vt, kmean):
    bsz, s, w = k.shape
    nt = s // TILE
    nbp = -(-nt // SUBLANES) * SUBLANES
    return pl.pallas_call(
        functools.partial(_moba_kernel, nbp),
        grid=(bsz, nt),
        in_specs=[pl.BlockSpec((1, w, TILE), lambda b, i: (b * nt + i, 0, 0)),
                  pl.BlockSpec((1, s, w), lambda b, i: (b, 0, 0)),
                  pl.BlockSpec((nt, w, TILE), lambda b, i: (b, 0, 0)),
                  pl.BlockSpec((1, LANES, w), lambda b, i: (b, 0, 0))],
        out_specs=pl.BlockSpec((1, TILE, w), lambda b, i: (b, i, 0)),
        out_shape=jax.ShapeDtypeStruct((bsz, s, w), BF16),
        scratch_shapes=[pltpu.VMEM((N_HEADS, nbp, TILE), F32),
                        pltpu.VMEM((N_HEADS, LANES, TILE), BF16)],
        compiler_params=_cparams(2), name="moba",
    )(q, k, vt, kmean)


def _tail_kernel(alpha, tf, mix_ref, qm_ref, mkv_ref, wmix_ref, wmem_ref, x_ref, g1_ref, b1_ref,
                 wgu_ref, wd_ref, g2_ref, b2_ref, xo_ref, xb_ref, h_ref):
    rows = mix_ref.shape[0]
    lo, hi = _half_masks(rows)
    mo = []
    for p in range(MEM_WIDTH // LANES):
        qp = qm_ref[:, p * LANES:(p + 1) * LANES]
        mk = mkv_ref[0, :, p * LANES:(p + 1) * LANES]
        mv = mkv_ref[0, :, MEM_WIDTH + p * LANES:MEM_WIDTH + (p + 1) * LANES]
        outs = []
        for half in (lo, hi):
            s = _dot_nt(jnp.where(half, qp, jnp.zeros_like(qp)), mk)
            e = jnp.exp(s - jnp.max(s, axis=1, keepdims=True))
            pv = jnp.dot(e.astype(BF16), mv, preferred_element_type=F32)
            outs.append(pv / jnp.sum(e, axis=1, keepdims=True))
        mo.append(jnp.where(lo, outs[0], outs[1]).astype(BF16))
    mixed = jnp.dot(mix_ref[...], wmix_ref[...], preferred_element_type=F32)
    for p, mo_p in enumerate(mo):
        mixed = mixed + jnp.dot(mo_p, wmem_ref[p * LANES:(p + 1) * LANES, :],
                                preferred_element_type=F32)
    x1 = _layer_norm(alpha * x_ref[...] + mixed, g1_ref[...], b1_ref[...])

    x1b = x1.astype(BF16)
    dff = wd_ref.shape[0]
    for c in range(dff // tf):
        gate = jnp.dot(x1b, wgu_ref[:, c * tf:(c + 1) * tf], preferred_element_type=F32)
        up = jnp.dot(x1b, wgu_ref[:, dff + c * tf:dff + (c + 1) * tf], preferred_element_type=F32)
        h_ref[:, c * tf:(c + 1) * tf] = (gate * jax.nn.sigmoid(gate) * up).astype(BF16)
    y = jnp.dot(h_ref[...], wd_ref[...], preferred_element_type=F32)
    x2 = _layer_norm(alpha * x1 + y, g2_ref[...], b2_ref[...])
    xo_ref[...] = x2
    xb_ref[...] = x2.astype(BF16)


def _tail(alpha, mix, qm, mkv, wmix, wmem, xf, g1, b1, wgu, wd, g2, b2, seq, tm=ROW_TILE, tf=FF_CHUNK):
    t, d = xf.shape
    wm = mix.shape[1]
    dff = wd.shape[0]
    per_batch = seq // tm
    row = lambda i: (i, 0)
    fixed = lambda i: (0, 0)
    resident = lambda shape: pl.BlockSpec(shape, fixed, pipeline_mode=pl.Buffered(1))
    vec = lambda a: a.reshape(1, d)
    return pl.pallas_call(
        functools.partial(_tail_kernel, alpha, tf),
        grid=(t // tm,),
        in_specs=[pl.BlockSpec((tm, wm), row),
                  pl.BlockSpec((tm, MEM_WIDTH), row),
                  pl.BlockSpec((1,) + mkv.shape[1:], lambda i: (i // per_batch, 0, 0)),
                  resident(wmix.shape), resident(wmem.shape),
                  pl.BlockSpec((tm, d), row),
                  resident((1, d)), resident((1, d)),
                  resident(wgu.shape), resident(wd.shape),
                  resident((1, d)), resident((1, d))],
        out_specs=[pl.BlockSpec((tm, d), row), pl.BlockSpec((tm, d), row)],
        out_shape=[jax.ShapeDtypeStruct((t, d), F32), jax.ShapeDtypeStruct((t, d), BF16)],
        scratch_shapes=[pltpu.VMEM((tm, dff), BF16)],
        compiler_params=_cparams(1), name="tail",
    )(mix, qm, mkv, wmix, wmem, xf, vec(g1), vec(b1), wgu, wd, vec(g2), vec(b2))


def _rope_tables(positions):
    inv = ROPE_THETA ** (-jnp.arange(ROPE_HALF, dtype=F32) / ROPE_HALF)
    ang = positions.astype(F32)[..., None] * inv
    cos, sin = jnp.cos(ang), jnp.sin(ang)
    rest = HEAD_DIM - ROPE_DIM
    pad = lambda a, before, after, val: jnp.pad(a, ((0, 0), (0, 0), (before, after)), constant_values=val)
    c = pad(jnp.concatenate([cos, cos], -1), 0, rest, 1.0)
    s1 = pad(sin, ROPE_HALF, rest, 0.0)
    s2 = pad(-sin, 0, ROPE_HALF + rest, 0.0)
    t = positions.shape[0] * positions.shape[1]
    token_major = tuple(jnp.tile(a, (1, 1, LANES // HEAD_DIM)).reshape(t, LANES) for a in (c, s1, s2))
    return token_major + (cos.reshape(t, ROPE_HALF).T, sin.reshape(t, ROPE_HALF).T)


def kernel(x, mem, positions, mem_ln_g, mem_ln_b, w_in_a, idx_kn_g, idx_kn_b, w_in_b, w_in_c,
           w_mem_kv, w_out, ln1_g, ln1_b, w_gate_up, w_down, ln2_g, ln2_b):
    bsz, seq, d = x.shape
    t = bsz * seq
    depth = w_out.shape[0]
    alpha = (2 * depth) ** 0.25
    assert seq % TILE == 0 and TILE == MOBA_BLOCK

    tables = _rope_tables(positions)
    mkv_all = _memkv(mem, mem_ln_g, mem_ln_b, w_mem_kv.astype(BF16))
    xf = x.reshape(t, d)
    xb = xf
    dummy_ln = jnp.zeros((1, LANES), F32)
    m2, m3 = 2 * MIX_WIDTH, 3 * MIX_WIDTH
    b3 = lambda a: a.reshape(bsz, seq, a.shape[-1])
    k_section = [(0, MIX_WIDTH, "rope", 1.0)]
    qv_tsections = [(0, MIX_WIDTH, "rope_tiles", Q_SCALE), (MIX_WIDTH, MIX_WIDTH, "tiles", 1.0)]
    m1 = MIX_WIDTH

    for i in range(depth):
        kind, j = i % 3, i // 3
        wo = w_out[i]
        if kind == 0:
            w = w_in_a[j]
            c_wi = m3 + IDX_HEADS * IDX_DIM
            c_ki, c_qm = c_wi + IDX_HEADS, c_wi + IDX_HEADS + IDX_DIM
            w_ki = w[:, c_ki:c_qm]
            w_new = jnp.concatenate([w[:, m1:m2], w[:, c_qm:], w_ki, w_ki], axis=1).astype(BF16)
            wt_new = jnp.concatenate([w[:, :m1], w[:, m2:c_ki]], axis=1).T.astype(BF16)
            sections = k_section + [(m1, MEM_WIDTH, "plain", SCALE), (m1 + MEM_WIDTH, LANES, "ln_rope", 1.0)]
            tsections = qv_tsections + [(m2, IDX_HEADS * IDX_DIM, "rope_tiles", 1.0),
                                        (m2 + IDX_HEADS * IDX_DIM, IDX_HEADS, "flat",
                                         float((IDX_HEADS * IDX_DIM) ** -0.5))]
            lng = jnp.tile(idx_kn_g[j], 2).reshape(1, LANES)
            lnb = jnp.tile(idx_kn_b[j], 2).reshape(1, LANES)
            k, qm, ki, qt, vt, qit, wit = _inproj(xb, w_new, wt_new, tables, lng, lnb,
                                                  sections, [BF16] * 3, tsections, [BF16, BF16, BF16, F32])
            mix = _dsa(qt, b3(k), vt, qit, wit, b3(ki))
        else:
            w = w_in_b[j] if kind == 1 else w_in_c[j]
            w_new = jnp.concatenate([w[:, m1:m2], w[:, m3:]], axis=1).astype(BF16)
            wt_new = jnp.concatenate([w[:, :m1], w[:, m2:m3]], axis=1).T.astype(BF16)
            sections = k_section + [(m1, MEM_WIDTH, "plain", SCALE)]
            k, qm, qt, vt = _inproj(xb, w_new, wt_new, tables, dummy_ln, dummy_ln,
                                    sections, [BF16] * 2, qv_tsections, [BF16, BF16])
            if kind == 1:
                mix = _dilated(qt, b3(k), vt)
            else:
                k3 = b3(k)
                mix = _moba(qt, k3, vt, _kmean(k3))
        xf, xb = _tail(alpha, mix.reshape(t, MIX_WIDTH), qm, mkv_all[i], wo[:MIX_WIDTH].astype(BF16),
                       wo[MIX_WIDTH:].astype(BF16), xf, ln1_g[i], ln1_b[i],
                       w_gate_up[i].astype(BF16), w_down[i].astype(BF16), ln2_g[i], ln2_b[i], seq)
    return xf.reshape(bsz, seq, d)
```

```python
import functools

import jax
import jax.numpy as jnp
import numpy as np
from jax import lax
from jax.experimental import pallas as pl
from jax.experimental.pallas import tpu as pltpu

F32 = jnp.float32
BF16 = jnp.bfloat16
I32 = jnp.int32
I16 = jnp.int16

HEAD_DIM = 64
N_HEADS = 12
MIX_WIDTH = N_HEADS * HEAD_DIM
N_MEM_HEADS = 4
MEM_WIDTH = N_MEM_HEADS * HEAD_DIM
ROPE_DIM = HEAD_DIM // 4
ROPE_HALF = ROPE_DIM // 2
ROPE_THETA = 500000.0
IDX_HEADS = 8
IDX_DIM = 64
TOPK_MAX = 256
DIL_PAIRS = ((128, 1), (512, 4), (2048, 16))
B_GROUP_HEADS = 4
B_V_DIM = MIX_WIDTH // B_GROUP_HEADS
MOBA_BLOCK = 256
MOBA_TOPK = 3
LN_EPS = 1e-5
SCALE = HEAD_DIM ** -0.5
LOG2E = 1.4426950408889634
Q_SCALE = SCALE * LOG2E

LANES = 128
SUBLANES = 8
VMEM_LIMIT_BYTES = 56 * 1024 * 1024

TILE = 256
ROW_TILE = 512
FF_CHUNK = 256
HALF_BITS = 16
LOW_MASK = (1 << HALF_BITS) - 1
SIGN_FLIP = np.int32(0x7FFFFFFF)
ONES_ROWS = 16
HEADS_PER_STEP = 12
NEG = -1e30
INT_MIN = np.int32(-2 ** 31)
I16_MIN = -2 ** 15
PACKED_ROWS = 2 * SUBLANES


def _cparams(n_axes):
    return pltpu.CompilerParams(dimension_semantics=("arbitrary",) * n_axes,
                                vmem_limit_bytes=VMEM_LIMIT_BYTES)


def _layer_norm(y, g, b):
    mu = jnp.mean(y, axis=-1, keepdims=True)
    yc = y - mu
    var = jnp.mean(yc * yc, axis=-1, keepdims=True)
    return yc * lax.rsqrt(var + LN_EPS) * g + b


def _dot_nt(a, b):
    return lax.dot_general(a, b, (((1,), (1,)), ((), ())), preferred_element_type=F32)


def _half_masks(rows):
    lane = lax.broadcasted_iota(I32, (rows, LANES), 1)
    return lane < HEAD_DIM, lane >= HEAD_DIM


def _store_heads_t(qt_ref, qh_ref):
    row = lax.broadcasted_iota(I32, (LANES, qt_ref.shape[2]), 0)
    for head in range(qh_ref.shape[0]):
        qp = qt_ref[0, (head // 2) * LANES:(head // 2 + 1) * LANES, :]
        keep = (row >= HEAD_DIM) if head % 2 else (row < HEAD_DIM)
        qh_ref[head] = jnp.where(keep, qp, jnp.zeros_like(qp))


def _head_cols(head):
    return slice((head // 2) * LANES, (head // 2 + 1) * LANES)


def _key_tile(j, ntiles=1):
    return pl.ds(pl.multiple_of(j * TILE, TILE), ntiles * TILE)


def _flash_over_tiles(states, lo, hi, step):
    npairs = (hi - lo) // 2
    states = lax.fori_loop(0, npairs, lambda jj, st: step(st, lo + 2 * jj, 2), states)
    return lax.cond((hi - lo) % 2 == 1, lambda st: step(st, hi - 1, 1), lambda st: st, states)


def _values_t(vt_ref, j, ntiles, rows):
    return jnp.concatenate([vt_ref[j + c, rows, :] for c in range(ntiles)], axis=1)


def _flash_init(dv):
    return (jnp.full((1, TILE), NEG, F32), jnp.zeros((dv + ONES_ROWS, TILE), F32))


def _flash_step(states, qk, biases, vts):
    n = len(states)
    ones = jnp.ones((ONES_ROWS, vts[0].shape[1]), BF16)
    logits = []
    for h in range(n):
        raw = jnp.dot(qk[h][0], qk[h][1], preferred_element_type=F32)
        logits.append(jnp.concatenate([raw[c * TILE:(c + 1) * TILE] + b for c, b in enumerate(biases[h])],
                                      axis=0))
    mids = []
    for (m_old, acc), s in zip(states, logits):
        m_new = jnp.maximum(m_old, jnp.max(s, axis=0, keepdims=True))
        p = jnp.exp2(s - m_new).astype(BF16)
        mids.append((m_new, jnp.exp2(m_old - m_new), p))
    out = []
    for (m_new, alpha, p), (_, acc), vt in zip(mids, states, vts):
        pv = jnp.dot(jnp.concatenate([vt, ones], axis=0), p, preferred_element_type=F32)
        out.append((m_new, alpha * acc + pv))
    return tuple(out)


def _flash_finish(o_ref, states, col0):
    out_t = jnp.concatenate([acc[:-ONES_ROWS] * (1.0 / acc[-1:]) for _, acc in states], axis=0)
    for c in range(out_t.shape[0] // LANES):
        o_ref[0, :, col0 + c * LANES:col0 + (c + 1) * LANES] = (
            out_t[c * LANES:(c + 1) * LANES, :].T.astype(o_ref.dtype))


def _memkv_kernel(mem_ref, g_ref, b_ref, w_ref, o_ref):
    mn = _layer_norm(mem_ref[0], g_ref[...], b_ref[...])
    o_ref[0, 0] = jnp.dot(mn.astype(BF16), w_ref[0], preferred_element_type=F32).astype(BF16)


def _memkv(mem, g, b, w_bf16):
    depth = w_bf16.shape[0]
    bsz, n_mem, d = mem.shape
    wout = w_bf16.shape[2]
    return pl.pallas_call(
        _memkv_kernel,
        grid=(depth, bsz),
        in_specs=[pl.BlockSpec((1, n_mem, d), lambda i, b_: (b_, 0, 0)),
                  pl.BlockSpec((1, d), lambda i, b_: (0, 0)),
                  pl.BlockSpec((1, d), lambda i, b_: (0, 0)),
                  pl.BlockSpec((1, d, wout), lambda i, b_: (i, 0, 0))],
        out_specs=pl.BlockSpec((1, 1, n_mem, wout), lambda i, b_: (i, b_, 0, 0)),
        out_shape=jax.ShapeDtypeStruct((depth, bsz, n_mem, wout), BF16),
        compiler_params=_cparams(2), name="memkv",
    )(mem, g.reshape(1, d), b.reshape(1, d), w_bf16)


def _rope(h, c, s1, s2):
    return h * c + pltpu.roll(h, ROPE_HALF, 1) * s1 + pltpu.roll(h, LANES - ROPE_HALF, 1) * s2


def _rope_t(ht, cos_t, sin_t):
    parts = []
    for base in range(0, ht.shape[0], HEAD_DIM):
        x1, x2 = ht[base:base + ROPE_HALF], ht[base + ROPE_HALF:base + ROPE_DIM]
        parts += [x1 * cos_t - x2 * sin_t, x2 * cos_t + x1 * sin_t, ht[base + ROPE_DIM:base + HEAD_DIM]]
    return jnp.concatenate(parts, axis=0)


def _inproj_kernel(sections, tsections, x_ref, w_ref, wt_ref, c_ref, s1_ref, s2_ref, ct_ref, st_ref,
                   lng_ref, lnb_ref, *out_refs):
    x = x_ref[...].astype(BF16)
    c, s1, s2 = c_ref[...], s1_ref[...], s2_ref[...]
    for (start, width, kind, scale), o_ref in zip(sections, out_refs):
        h = jnp.dot(x, w_ref[:, start:start + width], preferred_element_type=F32)
        if kind == "ln_rope":
            h = _layer_norm(h, lng_ref[...], lnb_ref[...])
        for ch in range(width // LANES):
            hc = h[:, ch * LANES:(ch + 1) * LANES]
            if kind in ("rope", "ln_rope"):
                hc = _rope(hc, c, s1, s2)
            if scale != 1.0:
                hc = hc * scale
            o_ref[:, ch * LANES:(ch + 1) * LANES] = hc.astype(o_ref.dtype)
    for (start, rows, kind, scale), o_ref in zip(tsections, out_refs[len(sections):]):
        ht = _dot_nt(wt_ref[start:start + rows, :], x)
        if kind == "rope_tiles":
            ht = _rope_t(ht, ct_ref[...], st_ref[...])
        if scale != 1.0:
            ht = ht * scale
        if kind in ("tiles", "rope_tiles"):
            for tl in range(o_ref.shape[0]):
                o_ref[tl] = ht[:, tl * TILE:(tl + 1) * TILE].astype(o_ref.dtype)
        else:
            o_ref[...] = ht.astype(o_ref.dtype)


def _inproj(xb, w_bf16, wt_bf16, tables, lng, lnb, sections, out_dtypes, tsections, tout_dtypes, tm=ROW_TILE):
    t, d = xb.shape
    row = lambda i: (i, 0)
    fixed = lambda i: (0, 0)
    out_shape = [jax.ShapeDtypeStruct((t, sec[1]), dt) for sec, dt in zip(sections, out_dtypes)]
    out_specs = [pl.BlockSpec((tm, sec[1]), row) for sec in sections]
    for (_, rows, kind, _), dt in zip(tsections, tout_dtypes):
        if kind in ("tiles", "rope_tiles"):
            out_shape.append(jax.ShapeDtypeStruct((t // TILE, rows, TILE), dt))
            out_specs.append(pl.BlockSpec((tm // TILE, rows, TILE), lambda i: (i, 0, 0)))
        else:
            out_shape.append(jax.ShapeDtypeStruct((rows, t), dt))
            out_specs.append(pl.BlockSpec((rows, tm), lambda i: (0, i)))
    return pl.pallas_call(
        functools.partial(_inproj_kernel, tuple(sections), tuple(tsections)),
        grid=(t // tm,),
        in_specs=[pl.BlockSpec((tm, d), row),
                  pl.BlockSpec(w_bf16.shape, fixed),
                  pl.BlockSpec(wt_bf16.shape, fixed),
                  pl.BlockSpec((tm, LANES), row),
                  pl.BlockSpec((tm, LANES), row),
                  pl.BlockSpec((tm, LANES), row),
                  pl.BlockSpec((ROPE_HALF, tm), lambda i: (0, i)),
                  pl.BlockSpec((ROPE_HALF, tm), lambda i: (0, i)),
                  pl.BlockSpec((1, LANES), fixed),
                  pl.BlockSpec((1, LANES), fixed)],
        out_specs=out_specs,
        out_shape=out_shape,
        compiler_params=_cparams(1), name="inproj",
    )(xb, w_bf16, wt_bf16, *tables, lng, lnb)


def _dsa_kernel(topk, q_ref, k_ref, vt_ref, qi_ref, wit_ref, ki_ref, o_ref,
                keys_ref, hi_ref, lo_ref, qh_ref, qih_ref):
    i = pl.program_id(1)
    nkt = i + 1
    krow = lax.broadcasted_iota(I32, (TILE, TILE), 0)
    qcol = lax.broadcasted_iota(I32, (TILE, TILE), 1)
    _store_heads_t(q_ref, qh_ref)
    _store_heads_t(qi_ref, qih_ref)

    wt = wit_ref[...]

    last_tile = ki_ref.shape[1] // TILE - 1

    def score_pair(jj, carry):
        tiles = [2 * jj + c for c in range(2)]
        dots = [[jnp.dot(ki_ref[0, _key_tile(jnp.minimum(j, last_tile)), :], qih_ref[h], preferred_element_type=F32)
                 for h in range(IDX_HEADS)] for j in tiles]
        for j, dj in zip(tiles, dots):
            sc = jnp.zeros((TILE, TILE), F32)
            for h in range(IDX_HEADS):
                sc = sc + wt[h:h + 1, :] * jnp.maximum(dj[h], 0.0)
            bits = lax.bitcast_convert_type(sc, I32)
            key = bits ^ (lax.shift_right_arithmetic(bits, 31) & SIGN_FLIP)
            causal = (j * TILE + krow) <= (i * TILE + qcol)
            key = jnp.where(causal, key, INT_MIN)
            keys_ref[j] = key
            hi_ref[j] = lax.shift_right_arithmetic(key, HALF_BITS).astype(I16)
            lo_ref[j] = ((key & LOW_MASK) + I16_MIN).astype(I16)
        return carry

    lax.fori_loop(0, (nkt + 1) // 2, score_pair, 0)

    def count16(ref, pred):
        def body(jj, acc):
            part = None
            for c in range(2):
                hits = jnp.where(pred(ref[2 * jj + c]), jnp.ones((), BF16), jnp.zeros((), BF16))
                hits = hits.reshape(TILE // PACKED_ROWS, PACKED_ROWS, TILE)
                for r in range(TILE // PACKED_ROWS):
                    part = hits[r] if part is None else part + hits[r]
            return acc + part.astype(F32)
        acc = lax.fori_loop(0, (nkt + 1) // 2, body, jnp.zeros((PACKED_ROWS, TILE), F32))
        return jnp.sum(acc, axis=0, keepdims=True)

    n_walked = (2 * ((nkt + 1) // 2) * TILE).astype(F32)

    def kth16(ref, k_q):
        zero = jnp.zeros((1, TILE), I32)
        c0 = count16(ref, lambda x: x >= zero.astype(I16))
        ok0 = c0 >= k_q
        init = (jnp.where(ok0, zero, zero + I16_MIN), jnp.where(ok0, c0, n_walked), jnp.where(ok0, 0.0, c0))

        def bisect(it, carry):
            t, c_ge, c_gt = carry
            cand = t | lax.shift_left(np.int32(1), HALF_BITS - 2 - it)
            c = count16(ref, lambda x: x >= cand.astype(I16))
            ok = c >= k_q
            return jnp.where(ok, cand, t), jnp.where(ok, c, c_ge), jnp.where(ok, c_gt, c)

        return lax.fori_loop(0, HALF_BITS - 1, bisect, init)

    t_q = i * TILE + lax.broadcasted_iota(I32, (1, TILE), 1)
    k_q = jnp.minimum(topk, t_q + 1).astype(F32)
    thr_hi, _, above_hi = kth16(hi_ref, k_q)
    thr_hi16 = thr_hi.astype(I16)
    k_lo = k_q - above_hi

    def park(j, carry):
        lo_ref[j] = jnp.where(hi_ref[j] == thr_hi16, lo_ref[j], jnp.full((), I16_MIN, I16))
        return carry

    lax.fori_loop(0, nkt, park, 0)
    thr_lo, at_or_above_lo, above_lo = kth16(lo_ref, k_lo)
    thr = lax.shift_left(thr_hi, HALF_BITS) | ((thr_lo - I16_MIN) & LOW_MASK)

    need = k_lo - above_lo
    n_eq = at_or_above_lo - above_lo
    has_tie = jnp.max(jnp.where(n_eq > need, 1.0, 0.0))

    @pl.when(has_tie > 0.0)
    def _():
        tril = jnp.where(qcol <= krow, 1.0, 0.0).astype(BF16)

        def fix(j, seen):
            kt = keys_ref[j]
            eq = kt == thr
            eqf = jnp.where(eq, 1.0, 0.0)
            pref = jnp.dot(tril, eqf.astype(BF16), preferred_element_type=F32) + seen
            keys_ref[j] = jnp.where(eq & (pref > need), INT_MIN, kt)
            return seen + jnp.sum(eqf, axis=0, keepdims=True)

        lax.fori_loop(0, nkt, fix, jnp.zeros((1, TILE), F32))

    for h0 in range(0, N_HEADS, HEADS_PER_STEP):
        heads = range(h0, h0 + HEADS_PER_STEP)

        def step(states, j, ntiles, heads=heads):
            ks = _key_tile(j, ntiles)
            bias = tuple(jnp.where(keys_ref[j + c] >= thr, 0.0, NEG) for c in range(ntiles))
            return _flash_step(states, [(k_ref[0, ks, _head_cols(h)], qh_ref[h]) for h in heads],
                               [bias] * len(heads),
                               [_values_t(vt_ref, j, ntiles, slice(h * HEAD_DIM, (h + 1) * HEAD_DIM))
                                for h in heads])

        states = _flash_over_tiles(tuple(_flash_init(HEAD_DIM) for _ in heads), 0, nkt, step)
        _flash_finish(o_ref, states, h0 * HEAD_DIM)


def _dsa(q, k, vt, qi, wit, ki):
    bsz, s, _ = k.shape
    topk = min(TOPK_MAX, s // 4)
    nt = s // TILE
    qtile = lambda w: pl.BlockSpec((1, TILE, w), lambda b, i: (b, i, 0))
    full = lambda w: pl.BlockSpec((1, s, w), lambda b, i: (b, 0, 0))
    return pl.pallas_call(
        functools.partial(_dsa_kernel, topk),
        grid=(bsz, nt),
        in_specs=[pl.BlockSpec((1, MIX_WIDTH, TILE), lambda b, i: (b * nt + i, 0, 0)), full(MIX_WIDTH),
                  pl.BlockSpec((nt, MIX_WIDTH, TILE), lambda b, i: (b, 0, 0)),
                  pl.BlockSpec((1, IDX_HEADS * IDX_DIM, TILE), lambda b, i: (b * nt + i, 0, 0)),
                  pl.BlockSpec((IDX_HEADS, TILE), lambda b, i: (0, b * nt + i)),
                  full(LANES)],
        out_specs=qtile(MIX_WIDTH),
        out_shape=jax.ShapeDtypeStruct((bsz, s, MIX_WIDTH), BF16),
        scratch_shapes=[pltpu.VMEM((nt + nt % 2, TILE, TILE), I32),
                        pltpu.VMEM((nt + nt % 2, TILE, TILE), I16),
                        pltpu.VMEM((nt + nt % 2, TILE, TILE), I16),
                        pltpu.VMEM((N_HEADS, LANES, TILE), BF16),
                        pltpu.VMEM((IDX_HEADS, LANES, TILE), BF16)],
        compiler_params=_cparams(2), name="dsa",
    )(q, k, vt, qi, wit, ki)


def _dilated_kernel(q_ref, k_ref, vt_ref, o_ref, qh_ref):
    i = pl.program_id(1)
    rel = (lax.broadcasted_iota(I32, (TILE, TILE), 1)
           - lax.broadcasted_iota(I32, (TILE, TILE), 0))
    _store_heads_t(q_ref, qh_ref)
    states = tuple(_flash_init(B_V_DIM) for _ in range(B_GROUP_HEADS))
    for g, (window, dilation) in enumerate(DIL_PAIRS):
        heads = [g * B_GROUP_HEADS + hv for hv in range(B_GROUP_HEADS)]
        nback = -(-window // TILE)

        def step(states, j, ntiles, heads=heads, window=window, dilation=dilation):
            ks = _key_tile(j, ntiles)
            bias = []
            for c in range(ntiles):
                dist = rel + (i - j - c) * TILE
                ok = (dist >= 0) & (dist <= window) & ((dist & (dilation - 1)) == 0)
                bias.append(jnp.where(ok, 0.0, NEG))
            return _flash_step(states, [(k_ref[0, ks, _head_cols(h)], qh_ref[h]) for h in heads],
                               [tuple(bias)] * len(heads),
                               [_values_t(vt_ref, j, ntiles, slice(hv * B_V_DIM, (hv + 1) * B_V_DIM))
                                for hv in range(B_GROUP_HEADS)])

        states = _flash_over_tiles(states, jnp.maximum(i - nback, 0), i + 1, step)
    _flash_finish(o_ref, states, 0)


def _dilated(q, k, vt):
    bsz, s, w = k.shape
    nt = s // TILE
    return pl.pallas_call(
        _dilated_kernel,
        grid=(bsz, nt),
        in_specs=[pl.BlockSpec((1, w, TILE), lambda b, i: (b * nt + i, 0, 0)),
                  pl.BlockSpec((1, s, w), lambda b, i: (b, 0, 0)),
                  pl.BlockSpec((nt, w, TILE), lambda b, i: (b, 0, 0))],
        out_specs=pl.BlockSpec((1, TILE, w), lambda b, i: (b, i, 0)),
        out_shape=jax.ShapeDtypeStruct((bsz, s, w), BF16),
        scratch_shapes=[pltpu.VMEM((N_HEADS, LANES, TILE), BF16)],
        compiler_params=_cparams(2), name="dilated",
    )(q, k, vt)


def _kmean_kernel(nblk, k_ref, o_ref):
    o_ref[0] = jnp.zeros(o_ref.shape[1:], o_ref.dtype)
    for n in range(nblk):
        kb = k_ref[0, n * MOBA_BLOCK:(n + 1) * MOBA_BLOCK, :].astype(F32)
        o_ref[0, n:n + 1, :] = jnp.mean(kb, axis=0, keepdims=True).astype(o_ref.dtype)


def _kmean(k):
    bsz, s, w = k.shape
    return pl.pallas_call(
        functools.partial(_kmean_kernel, s // MOBA_BLOCK),
        grid=(bsz,),
        in_specs=[pl.BlockSpec((1, s, w), lambda b: (b, 0, 0))],
        out_specs=pl.BlockSpec((1, LANES, w), lambda b: (b, 0, 0)),
        out_shape=jax.ShapeDtypeStruct((bsz, LANES, w), BF16),
        compiler_params=_cparams(1), name="kmean",
    )(k)


def _moba_kernel(nbp, q_ref, k_ref, vt_ref, km_ref, o_ref, sel_ref, qh_ref):
    i = pl.program_id(1)
    blk = lax.broadcasted_iota(I32, (nbp, TILE), 0)
    blk_f = blk.astype(F32)
    krow = lax.broadcasted_iota(I32, (TILE, TILE), 0)
    qcol = lax.broadcasted_iota(I32, (TILE, TILE), 1)
    causal_bias = jnp.where(krow <= qcol, 0.0, NEG)
    _store_heads_t(q_ref, qh_ref)

    for h in range(N_HEADS):
        gate = jnp.dot(km_ref[0, :, _head_cols(h)], qh_ref[h], preferred_element_type=F32)
        g = jnp.where(blk < i, gate[:nbp], -jnp.inf)
        sel = jnp.zeros((nbp, TILE), F32)
        for r in range(MOBA_TOPK):
            mx = jnp.max(g, axis=0, keepdims=True)
            idx = jnp.min(jnp.where(g == mx, blk_f, float(nbp)), axis=0, keepdims=True)
            hit = blk_f == idx
            sel = jnp.where(hit & (jnp.full((nbp, TILE), r, I32) < i), 1.0, sel)
            g = jnp.where(hit, -jnp.inf, g)
        sel_ref[h] = sel

    for h0 in range(0, N_HEADS, HEADS_PER_STEP):
        heads = range(h0, h0 + HEADS_PER_STEP)

        def step(states, j, ntiles, bias_of, heads=heads):
            ks = _key_tile(j, ntiles)
            return _flash_step(states, [(k_ref[0, ks, _head_cols(h)], qh_ref[h]) for h in heads],
                               [tuple(bias_of(h, j + c) for c in range(ntiles)) for h in heads],
                               [_values_t(vt_ref, j, ntiles, slice(h * HEAD_DIM, (h + 1) * HEAD_DIM))
                                for h in heads])

        def past(states, n, ntiles, step=step):
            picked = lambda h, blk_n: jnp.where(sel_ref[h, pl.ds(blk_n, 1), :] > 0.5, 0.0, NEG)
            return step(states, n, ntiles, picked)

        states = _flash_over_tiles(tuple(_flash_init(HEAD_DIM) for _ in heads), 0, i, past)
        states = step(states, i, 1, lambda h, blk_n: causal_bias)
        _flash_finish(o_ref, states, h0 * HEAD_DIM)


def _moba(q, k, vt, kmean):
    bsz, s, w = k.shape
    nt = s // TILE
    nbp = -(-nt // SUBLANES) * SUBLANES
    return pl.pallas_call(
        functools.partial(_moba_kernel, nbp),
        grid=(bsz, nt),
        in_specs=[pl.BlockSpec((1, w, TILE), lambda b, i: (b * nt + i, 0, 0)),
                  pl.BlockSpec((1, s, w), lambda b, i: (b, 0, 0)),
                  pl.BlockSpec((nt, w, TILE), lambda b, i: (b, 0, 0)),
                  pl.BlockSpec((1, LANES, w), lambda b, i: (b, 0, 0))],
        out_specs=pl.BlockSpec((1, TILE, w), lambda b, i: (b, i, 0)),
        out_shape=jax.ShapeDtypeStruct((bsz, s, w), BF16),
        scratch_shapes=[pltpu.VMEM((N_HEADS, nbp, TILE), F32),
                        pltpu.VMEM((N_HEADS, LANES, TILE), BF16)],
        compiler_params=_cparams(2), name="moba",
    )(q, k, vt, kmean)


def _tail_kernel(alpha, tf, mix_ref, qm_ref, mkv_ref, wmix_ref, wmem_ref, x_ref, g1_ref, b1_ref,
                 wgu_ref, wd_ref, g2_ref, b2_ref, xo_ref, xb_ref, h_ref):
    rows = mix_ref.shape[0]
    lo, hi = _half_masks(rows)
    mo = []
    for p in range(MEM_WIDTH // LANES):
        qp = qm_ref[:, p * LANES:(p + 1) * LANES]
        mk = mkv_ref[0, :, p * LANES:(p + 1) * LANES]
        mv = mkv_ref[0, :, MEM_WIDTH + p * LANES:MEM_WIDTH + (p + 1) * LANES]
        outs = []
        for half in (lo, hi):
            s = _dot_nt(jnp.where(half, qp, jnp.zeros_like(qp)), mk)
            e = jnp.exp(s - jnp.max(s, axis=1, keepdims=True))
            pv = jnp.dot(e.astype(BF16), mv, preferred_element_type=F32)
            outs.append(pv / jnp.sum(e, axis=1, keepdims=True))
        mo.append(jnp.where(lo, outs[0], outs[1]).astype(BF16))
    mixed = jnp.dot(mix_ref[...], wmix_ref[...], preferred_element_type=F32)
    for p, mo_p in enumerate(mo):
        mixed = mixed + jnp.dot(mo_p, wmem_ref[p * LANES:(p + 1) * LANES, :],
                                preferred_element_type=F32)
    x1 = _layer_norm(alpha * x_ref[...] + mixed, g1_ref[...], b1_ref[...])

    x1b = x1.astype(BF16)
    dff = wd_ref.shape[0]
    for c in range(dff // tf):
        gate = jnp.dot(x1b, wgu_ref[:, c * tf:(c + 1) * tf], preferred_element_type=F32)
        up = jnp.dot(x1b, wgu_ref[:, dff + c * tf:dff + (c + 1) * tf], preferred_element_type=F32)
        h_ref[:, c * tf:(c + 1) * tf] = (gate * jax.nn.sigmoid(gate) * up).astype(BF16)
    y = jnp.dot(h_ref[...], wd_ref[...], preferred_element_type=F32)
    x2 = _layer_norm(alpha * x1 + y, g2_ref[...], b2_ref[...])
    xo_ref[...] = x2
    xb_ref[...] = x2.astype(BF16)


def _tail(alpha, mix, qm, mkv, wmix, wmem, xf, g1, b1, wgu, wd, g2, b2, seq, tm=ROW_TILE, tf=FF_CHUNK):
    t, d = xf.shape
    wm = mix.shape[1]
    dff = wd.shape[0]
    per_batch = seq // tm
    row = lambda i: (i, 0)
    fixed = lambda i: (0, 0)
    resident = lambda shape: pl.BlockSpec(shape, fixed, pipeline_mode=pl.Buffered(1))
    vec = lambda a: a.reshape(1, d)
    return pl.pallas_call(
        functools.partial(_tail_kernel, alpha, tf),
        grid=(t // tm,),
        in_specs=[pl.BlockSpec((tm, wm), row),
                  pl.BlockSpec((tm, MEM_WIDTH), row),
                  pl.BlockSpec((1,) + mkv.shape[1:], lambda i: (i // per_batch, 0, 0)),
                  resident(wmix.shape), resident(wmem.shape),
                  pl.BlockSpec((tm, d), row),
                  resident((1, d)), resident((1, d)),
                  resident(wgu.shape), resident(wd.shape),
                  resident((1, d)), resident((1, d))],
        out_specs=[pl.BlockSpec((tm, d), row), pl.BlockSpec((tm, d), row)],
        out_shape=[jax.ShapeDtypeStruct((t, d), F32), jax.ShapeDtypeStruct((t, d), BF16)],
        scratch_shapes=[pltpu.VMEM((tm, dff), BF16)],
        compiler_params=_cparams(1), name="tail",
    )(mix, qm, mkv, wmix, wmem, xf, vec(g1), vec(b1), wgu, wd, vec(g2), vec(b2))


def _rope_tables(positions):
    inv = ROPE_THETA ** (-jnp.arange(ROPE_HALF, dtype=F32) / ROPE_HALF)
    ang = positions.astype(F32)[..., None] * inv
    cos, sin = jnp.cos(ang), jnp.sin(ang)
    rest = HEAD_DIM - ROPE_DIM
    pad = lambda a, before, after, val: jnp.pad(a, ((0, 0), (0, 0), (before, after)), constant_values=val)
    c = pad(jnp.concatenate([cos, cos], -1), 0, rest, 1.0)
    s1 = pad(sin, ROPE_HALF, rest, 0.0)
    s2 = pad(-sin, 0, ROPE_HALF + rest, 0.0)
    t = positions.shape[0] * positions.shape[1]
    token_major = tuple(jnp.tile(a, (1, 1, LANES // HEAD_DIM)).reshape(t, LANES) for a in (c, s1, s2))
    return token_major + (cos.reshape(t, ROPE_HALF).T, sin.reshape(t, ROPE_HALF).T)


def kernel(x, mem, positions, mem_ln_g, mem_ln_b, w_in_a, idx_kn_g, idx_kn_b, w_in_b, w_in_c,
           w_mem_kv, w_out, ln1_g, ln1_b, w_gate_up, w_down, ln2_g, ln2_b):
    bsz, seq, d = x.shape
    t = bsz * seq
    depth = w_out.shape[0]
    alpha = (2 * depth) ** 0.25
    assert seq % TILE == 0 and TILE == MOBA_BLOCK

    tables = _rope_tables(positions)
    mkv_all = _memkv(mem, mem_ln_g, mem_ln_b, w_mem_kv.astype(BF16))
    xf = x.reshape(t, d)
    xb = xf
    dummy_ln = jnp.zeros((1, LANES), F32)
    m2, m3 = 2 * MIX_WIDTH, 3 * MIX_WIDTH
    b3 = lambda a: a.reshape(bsz, seq, a.shape[-1])
    k_section = [(0, MIX_WIDTH, "rope", 1.0)]
    qv_tsections = [(0, MIX_WIDTH, "rope_tiles", Q_SCALE), (MIX_WIDTH, MIX_WIDTH, "tiles", 1.0)]
    m1 = MIX_WIDTH

    for i in range(depth):
        kind, j = i % 3, i // 3
        wo = w_out[i]
        if kind == 0:
            w = w_in_a[j]
            c_wi = m3 + IDX_HEADS * IDX_DIM
            c_ki, c_qm = c_wi + IDX_HEADS, c_wi + IDX_HEADS + IDX_DIM
            w_ki = w[:, c_ki:c_qm]
            w_new = jnp.concatenate([w[:, m1:m2], w[:, c_qm:], w_ki, w_ki], axis=1).astype(BF16)
            wt_new = jnp.concatenate([w[:, :m1], w[:, m2:c_ki]], axis=1).T.astype(BF16)
            sections = k_section + [(m1, MEM_WIDTH, "plain", SCALE), (m1 + MEM_WIDTH, LANES, "ln_rope", 1.0)]
            tsections = qv_tsections + [(m2, IDX_HEADS * IDX_DIM, "rope_tiles", 1.0),
                                        (m2 + IDX_HEADS * IDX_DIM, IDX_HEADS, "flat",
                                         float((IDX_HEADS * IDX_DIM) ** -0.5))]
            lng = jnp.tile(idx_kn_g[j], 2).reshape(1, LANES)
            lnb = jnp.tile(idx_kn_b[j], 2).reshape(1, LANES)
            k, qm, ki, qt, vt, qit, wit = _inproj(xb, w_new, wt_new, tables, lng, lnb,
                                                  sections, [BF16] * 3, tsections, [BF16, BF16, BF16, F32])
            mix = _dsa(qt, b3(k), vt, qit, wit, b3(ki))
        else:
            w = w_in_b[j] if kind == 1 else w_in_c[j]
            w_new = jnp.concatenate([w[:, m1:m2], w[:, m3:]], axis=1).astype(BF16)
            wt_new = jnp.concatenate([w[:, :m1], w[:, m2:m3]], axis=1).T.astype(BF16)
            sections = k_section + [(m1, MEM_WIDTH, "plain", SCALE)]
            k, qm, qt, vt = _inproj(xb, w_new, wt_new, tables, dummy_ln, dummy_ln,
                                    sections, [BF16] * 2, qv_tsections, [BF16, BF16])
            if kind == 1:
                mix = _dilated(qt, b3(k), vt)
            else:
                k3 = b3(k)
                mix = _moba(qt, k3, vt, _kmean(k3))
        xf, xb = _tail(alpha, mix.reshape(t, MIX_WIDTH), qm, mkv_all[i], wo[:MIX_WIDTH].astype(BF16),
                       wo[MIX_WIDTH:].astype(BF16), xf, ln1_g[i], ln1_b[i],
                       w_gate_up[i].astype(BF16), w_down[i].astype(BF16), ln2_g[i], ln2_b[i], seq)
    return xf.reshape(bsz, seq, d)
```
